```python
import math
import jax, jax.numpy as jnp
from jax import lax
import numpy as np

D_MODEL = 1024
BATCH = 8
SEQ = 4096
DEPTH = 2

MEM_LEN = 256
CHUNK = 64
NORM_EPS = 1e-6

GLA_HEADS = 4
GLA_DK = 64
GLA_DV = 96
GLA_RANK = 16
GLA_GATE_NORMALIZER = 16.0
GDN_HEADS = 4
GDN_DK = 96
GDN_DV = 96
GDN_CONV = 5
XA_HEADS = 4
XA_DH = 64

GLA_W = GLA_HEADS * GLA_DV
GDN_W = GDN_HEADS * GDN_DV
XA_W = XA_HEADS * XA_DH
MIX_W = GLA_W + GDN_W + XA_W
GDN_QKV_W = 2 * GDN_HEADS * GDN_DK + GDN_W

IN_SIZES = (
    GLA_HEADS * GLA_DK,
    GLA_HEADS * GLA_DK,
    GLA_W,
    GLA_W,
    2 * GLA_RANK,
    GDN_QKV_W,
    GDN_W,
    2 * GDN_HEADS,
    2 * GDN_HEADS,
    XA_W,
    XA_W,
)
IN_W = sum(IN_SIZES)

kernel_name = "hymba_gla_gdn_memxattn_encoder"


def _rmsnorm(x, w):
    xf = x.astype(jnp.float32)
    y = xf * lax.rsqrt(jnp.mean(xf * xf, axis=-1, keepdims=True) + NORM_EPS)
    return y * w.astype(jnp.float32)


def _l2norm(x):
    return x * lax.rsqrt(jnp.sum(x * x, axis=-1, keepdims=True) + NORM_EPS)


def _split_cols(t, sizes):
    out, start = [], 0
    for s in sizes:
        out.append(t[..., start:start + s])
        start += s
    return out


def _to_chunks(t):
    b, tl = t.shape[0], t.shape[1]
    t = t.reshape((b, tl // CHUNK, CHUNK) + t.shape[2:])
    return jnp.moveaxis(t, 3, 1)


def _from_chunks(t):
    t = jnp.moveaxis(t, 1, 3)
    b, n, c = t.shape[:3]
    return t.reshape((b, n * c) + t.shape[3:])


def _run_bidir(fn, fwd_args, bwd_args):
    o_f = _from_chunks(fn(*[_to_chunks(a) for a in fwd_args]))
    o_b = _from_chunks(fn(*[_to_chunks(a[:, ::-1]) for a in bwd_args]))[:, ::-1]
    return o_f + o_b


def _gla_chunked(q, k, v, g):
    c = q.shape[-2]
    b = jnp.cumsum(g, axis=-2)
    q_e = q * jnp.exp(b)
    k_e = k * jnp.exp(-b)
    incl = jnp.tril(jnp.ones((c, c), dtype=bool))
    a = jnp.where(incl, jnp.einsum('bhncd,bhnsd->bhncs', q_e, k_e), 0.0)
    o_intra = jnp.einsum('bhncs,bhnsv->bhncv', a, v)
    b_last = b[..., -1:, :]
    chunk_state = jnp.einsum('bhncd,bhncv->bhndv', k * jnp.exp(b_last - b), v)
    chunk_decay = jnp.exp(b_last[..., 0, :])

    def step(s, inp):
        dec, st = inp
        return dec[..., None] * s + st, s

    bsz, h, _, dk = chunk_decay.shape
    s0 = jnp.zeros((bsz, h, dk, v.shape[-1]), jnp.float32)
    _, s_prev = lax.scan(step, s0, (jnp.moveaxis(chunk_decay, 2, 0), jnp.moveaxis(chunk_state, 2, 0)))
    s_prev = jnp.moveaxis(s_prev, 0, 2)
    return o_intra + jnp.einsum('bhncd,bhndv->bhncv', q_e, s_prev)


def _gated_delta_chunked(q, k, v, g, beta):
    c = q.shape[-2]
    gc = jnp.cumsum(g, axis=-1)
    incl = jnp.tril(jnp.ones((c, c), dtype=bool))
    strict = jnp.tril(jnp.ones((c, c), dtype=bool), k=-1)
    diff = gc[..., :, None] - gc[..., None, :]
    decay = jnp.where(incl, jnp.exp(jnp.where(incl, diff, 0.0)), 0.0)
    kk = jnp.einsum('bhncd,bhnsd->bhncs', k, k)
    a_strict = jnp.where(strict, kk * decay * beta[..., :, None], 0.0)
    t_mat = a_strict + jnp.eye(c, dtype=jnp.float32)
    rhs = jnp.concatenate([v * beta[..., None], k * (beta * jnp.exp(gc))[..., None]], axis=-1)
    sol = lax.linalg.triangular_solve(t_mat, rhs, left_side=True, lower=True, unit_diagonal=True)
    dv = v.shape[-1]
    u, w = sol[..., :dv], sol[..., dv:]
    qk = jnp.where(incl, jnp.einsum('bhncd,bhnsd->bhncs', q, k) * decay, 0.0)
    q_dec = q * jnp.exp(gc)[..., None]
    k_to_end = k * jnp.exp(gc[..., -1:] - gc)[..., None]
    chunk_decay = jnp.exp(gc[..., -1])

    def step(s, inp):
        u_c, w_c, q_c, qk_c, k_c, dec_c = inp
        v_new = u_c - jnp.einsum('bhcd,bhdv->bhcv', w_c, s)
        o = jnp.einsum('bhcd,bhdv->bhcv', q_c, s) + jnp.einsum('bhcs,bhsv->bhcv', qk_c, v_new)
        s = dec_c[..., None, None] * s + jnp.einsum('bhcd,bhcv->bhdv', k_c, v_new)
        return s, o

    bsz, h = q.shape[:2]
    s0 = jnp.zeros((bsz, h, k.shape[-1], dv), jnp.float32)
    xs = tuple(jnp.moveaxis(t, 2, 0) for t in (u, w, q_dec, qk, k_to_end, chunk_decay))
    _, o = lax.scan(step, s0, xs)
    return jnp.moveaxis(o, 0, 2)


def _centred_depthwise_conv(x, w):
    ch, kw = w.shape
    rhs = jnp.transpose(w, (1, 0))[:, None, :]
    return lax.conv_general_dilated(x, rhs.astype(x.dtype), window_strides=(1,),
                                    padding=[(kw // 2, kw // 2)],
                                    dimension_numbers=('NWC', 'WIO', 'NWC'),
                                    feature_group_count=ch)


def _layer(x, mem, norm_w, w_in, gla_w2, gla_b, gla_norm_w, gdn_conv_w, gdn_a_log, gdn_dt_bias,
           gdn_norm_w, mem_norm_w, xa_w_kv, xa_norm_w, w_out):
    bsz, tl, _ = x.shape
    h = _rmsnorm(x, norm_w)
    proj = jnp.einsum('btd,de->bte', h, w_in.astype(jnp.float32))
    (gla_q, gla_k, gla_v, gla_z, gla_lr, gdn_qkv, gdn_z, gdn_b, gdn_a, xa_q, xa_z) = _split_cols(proj, IN_SIZES)

    lr = gla_lr.reshape(bsz, tl, 2, GLA_RANK)
    gate_logits = jnp.einsum('btzr,zre->btze', lr, gla_w2.astype(jnp.float32)) + gla_b.astype(jnp.float32)
    log_alpha = (jax.nn.log_sigmoid(gate_logits) / GLA_GATE_NORMALIZER).reshape(bsz, tl, 2, GLA_HEADS, GLA_DK)
    q1 = gla_q.reshape(bsz, tl, GLA_HEADS, GLA_DK) * (GLA_DK ** -0.5)
    k1 = gla_k.reshape(bsz, tl, GLA_HEADS, GLA_DK)
    v1 = gla_v.reshape(bsz, tl, GLA_HEADS, GLA_DV)
    o1 = _run_bidir(_gla_chunked, (q1, k1, v1, log_alpha[:, :, 0]), (q1, k1, v1, log_alpha[:, :, 1]))
    o1 = _rmsnorm(o1, gla_norm_w) * jax.nn.silu(gla_z).reshape(bsz, tl, GLA_HEADS, GLA_DV)

    qkv = jax.nn.silu(_centred_depthwise_conv(gdn_qkv, gdn_conv_w.astype(jnp.float32)))
    q2, k2, v2 = _split_cols(qkv, (GDN_HEADS * GDN_DK, GDN_HEADS * GDN_DK, GDN_W))
    q2 = _l2norm(q2.reshape(bsz, tl, GDN_HEADS, GDN_DK)) * (GDN_DK ** -0.5)
    k2 = _l2norm(k2.reshape(bsz, tl, GDN_HEADS, GDN_DK))
    v2 = v2.reshape(bsz, tl, GDN_HEADS, GDN_DV)
    beta = jax.nn.sigmoid(gdn_b.reshape(bsz, tl, 2, GDN_HEADS))
    g = -jnp.exp(gdn_a_log.astype(jnp.float32)) * jax.nn.softplus(
        gdn_a.reshape(bsz, tl, 2, GDN_HEADS) + gdn_dt_bias.astype(jnp.float32))
    o2 = _run_bidir(_gated_delta_chunked,
                    (q2, k2, v2, g[:, :, 0], beta[:, :, 0]),
                    (q2, k2, v2, g[:, :, 1], beta[:, :, 1]))
    o2 = _rmsnorm(o2, gdn_norm_w) * jax.nn.silu(gdn_z).reshape(bsz, tl, GDN_HEADS, GDN_DV)

    m = _rmsnorm(mem, mem_norm_w)
    mkv = jnp.einsum('bmd,de->bme', m, xa_w_kv.astype(jnp.float32))
    mk = mkv[..., :XA_W].reshape(bsz, -1, XA_HEADS, XA_DH)
    mv = mkv[..., XA_W:].reshape(bsz, -1, XA_HEADS, XA_DH)
    q3 = xa_q.reshape(bsz, tl, XA_HEADS, XA_DH)
    scores = jnp.einsum('bthd,bmhd->bhtm', q3, mk) * (XA_DH ** -0.5)
    p = jax.nn.softmax(scores, axis=-1)
    o3 = jnp.einsum('bhtm,bmhd->bthd', p, mv)
    o3 = _rmsnorm(o3, xa_norm_w) * jax.nn.silu(xa_z).reshape(bsz, tl, XA_HEADS, XA_DH)

    o_cat = jnp.concatenate([o1.reshape(bsz, tl, GLA_W), o2.reshape(bsz, tl, GDN_W),
                             o3.reshape(bsz, tl, XA_W)], axis=-1)
    y = jnp.einsum('bte,ed->btd', o_cat, w_out.astype(jnp.float32))
    return x + y.astype(x.dtype)


def setup_inputs(seed: int = 0) -> dict:
    key = jax.random.key(seed)
    ks = jax.random.split(key, 20)
    f32 = jnp.float32
    x = jax.random.normal(ks[0], (BATCH, SEQ, D_MODEL), f32)
    mem = jax.random.normal(ks[1], (BATCH, MEM_LEN, D_MODEL), f32)
    norm_w = 1.0 + 0.02 * jax.random.normal(ks[2], (DEPTH, D_MODEL), f32)
    w_in = jax.random.normal(ks[3], (DEPTH, D_MODEL, IN_W), f32) * D_MODEL ** -0.5
    gla_w2 = jax.random.normal(ks[4], (DEPTH, 2, GLA_RANK, GLA_HEADS * GLA_DK), f32) * GLA_RANK ** -0.5
    gla_b = 0.1 * jax.random.normal(ks[5], (DEPTH, 2, GLA_HEADS * GLA_DK), f32)
    gla_norm_w = 1.0 + 0.02 * jax.random.normal(ks[6], (DEPTH, GLA_DV), f32)
    gdn_conv_w = jax.random.normal(ks[7], (DEPTH, GDN_QKV_W, GDN_CONV), f32) * GDN_CONV ** -0.5
    gdn_a_log = jnp.log(jax.random.uniform(ks[8], (DEPTH, 2, GDN_HEADS), f32, 1.0, 16.0))
    dt = jnp.exp(jax.random.uniform(ks[9], (DEPTH, 2, GDN_HEADS), f32,
                                    math.log(1e-3), math.log(1e-1)))
    gdn_dt_bias = dt + jnp.log(-jnp.expm1(-dt))
    gdn_norm_w = 1.0 + 0.02 * jax.random.normal(ks[10], (DEPTH, GDN_DV), f32)
    mem_norm_w = 1.0 + 0.02 * jax.random.normal(ks[11], (DEPTH, D_MODEL), f32)
    xa_w_kv = jax.random.normal(ks[12], (DEPTH, D_MODEL, 2 * XA_W), f32) * D_MODEL ** -0.5
    xa_norm_w = 1.0 + 0.02 * jax.random.normal(ks[13], (DEPTH, XA_DH), f32)
    w_out = jax.random.normal(ks[14], (DEPTH, MIX_W, D_MODEL), f32) * MIX_W ** -0.5
    final_norm_w = 1.0 + 0.02 * jax.random.normal(ks[15], (D_MODEL,), f32)
    return {"x": x, "mem": mem, "norm_w": norm_w, "w_in": w_in, "gla_w2": gla_w2, "gla_b": gla_b,
            "gla_norm_w": gla_norm_w, "gdn_conv_w": gdn_conv_w, "gdn_a_log": gdn_a_log,
            "gdn_dt_bias": gdn_dt_bias, "gdn_norm_w": gdn_norm_w, "mem_norm_w": mem_norm_w,
            "xa_w_kv": xa_w_kv, "xa_norm_w": xa_norm_w, "w_out": w_out, "final_norm_w": final_norm_w}


def reference(x, mem, norm_w, w_in, gla_w2, gla_b, gla_norm_w, gdn_conv_w, gdn_a_log, gdn_dt_bias,
              gdn_norm_w, mem_norm_w, xa_w_kv, xa_norm_w, w_out, final_norm_w):
    h = x
    for l in range(DEPTH):
        h = _layer(h, mem, norm_w[l], w_in[l], gla_w2[l], gla_b[l], gla_norm_w[l], gdn_conv_w[l],
                   gdn_a_log[l], gdn_dt_bias[l], gdn_norm_w[l], mem_norm_w[l], xa_w_kv[l],
                   xa_norm_w[l], w_out[l])
    return _rmsnorm(h, final_norm_w).astype(x.dtype)
```

```python
import functools

import jax
import jax.numpy as jnp
from jax import lax
from jax.experimental import pallas as pl
from jax.experimental.pallas import tpu as pltpu

F32 = jnp.float32
BF16 = jnp.bfloat16

D_MODEL = 1024
BATCH = 8
SEQ = 4096
DEPTH = 2
MEM_LEN = 256
CHUNK = 64
NORM_EPS = 1e-6
GLA_HEADS = 4
GLA_DK = 64
GLA_DV = 96
GLA_RANK = 16
GLA_GATE_NORMALIZER = 16.0
GDN_HEADS = 4
GDN_DK = 96
GDN_DV = 96
GDN_CONV = 5
XA_HEADS = 4
XA_DH = 64

LANES = 128
HEADS = 4
HW = HEADS * LANES
GLA_QK_W = GLA_HEADS * GLA_DK
XA_W = XA_HEADS * XA_DH
MIX_PAD_W = 2 * HW + XA_W

C_GQ = 0
C_GK = C_GQ + GLA_QK_W
C_GV = C_GK + GLA_QK_W
C_DQKV = C_GV + HW
C_Z = C_DQKV + 3 * HW
C_XQ = C_Z + MIX_PAD_W
C_MISC = C_XQ + XA_W
IN_PAD_W = C_MISC + LANES

M_BETA = 0
M_A = 8
M_A_COPIES = 4
M_LR = 64
G_BETA, G_GC, G_EGC, G_EKEND, G_DEC = 0, 8, 16, 24, 32
ROW_SLAB = 16

TM_IN = 256
TM_OUT = 256
NC = 4
TB = NC * CHUNK
HALO = 8
VMEM_LIMIT = 56 * 1024 * 1024


def _dot(a, b):
    return jnp.dot(a, b, preferred_element_type=F32)


def _dot_nt(a, b):
    return lax.dot_general(a, b, (((1,), (1,)), ((), ())), preferred_element_type=F32)


def _dot_tn(a, b):
    return lax.dot_general(a, b, (((0,), (0,)), ((), ())), preferred_element_type=F32)


def _sigmoid(x):
    return 1.0 / (1.0 + jnp.exp(-x))


def _silu(x):
    return x * _sigmoid(x)


def _softplus(x):
    return jnp.maximum(x, 0.0) + jnp.log1p(jnp.exp(-jnp.abs(x)))


def _log_sigmoid(x):
    return jnp.minimum(x, 0.0) - jnp.log1p(jnp.exp(-jnp.abs(x)))


def _split2(x):
    hi = x.astype(BF16)
    lo = (x - hi.astype(F32)).astype(BF16)
    return hi, lo


def _tri_sum(tri, x):
    hi, lo = _split2(x)
    return _dot(tri, hi) + _dot(tri, lo)


def _tri_masks():
    ri = lax.broadcasted_iota(jnp.int32, (CHUNK, CHUNK), 0)
    ci = lax.broadcasted_iota(jnp.int32, (CHUNK, CHUNK), 1)
    return ri, ci


def _memkv_kernel(mem_ref, nw_ref, w_ref, out_ref):
    m = mem_ref[0]
    ms = jnp.mean(m * m, axis=-1, keepdims=True)
    mn = (m * lax.rsqrt(ms + NORM_EPS) * nw_ref[0]).astype(BF16)
    out_ref[0, 0] = _dot(mn, w_ref[0]).astype(BF16)


def _memkv(mem, mem_norm_w, xa_w_kv_bf16):
    return pl.pallas_call(
        _memkv_kernel,
        grid=(DEPTH, BATCH),
        in_specs=[
            pl.BlockSpec((1, MEM_LEN, D_MODEL), lambda l, b: (b, 0, 0)),
            pl.BlockSpec((1, 1, D_MODEL), lambda l, b: (l, 0, 0)),
            pl.BlockSpec((1, D_MODEL, 2 * XA_W), lambda l, b: (l, 0, 0)),
        ],
        out_specs=pl.BlockSpec((1, 1, MEM_LEN, 2 * XA_W), lambda l, b: (l, b, 0, 0)),
        out_shape=jax.ShapeDtypeStruct((DEPTH, BATCH, MEM_LEN, 2 * XA_W), BF16),
        compiler_params=pltpu.CompilerParams(dimension_semantics=("parallel", "parallel")),
        name="memkv",
    )(mem, mem_norm_w.reshape(DEPTH, 1, D_MODEL), xa_w_kv_bf16)


def _inproj_kernel(x_ref, xp_ref, xn_ref, nw_ref, w_ref, cw_ref, w2_ref, gb_ref, gp_ref,
                   gla_ref, v1_ref, dec_ref, zg_ref, q2_ref, k2_ref, v2_ref, gcol_ref, grow_ref,
                   xaq_ref, ext_ref):
    nw = nw_ref[...]

    def norm(ref):
        x = ref[...]
        ms = jnp.mean(x * x, axis=-1, keepdims=True)
        return (x * lax.rsqrt(ms + NORM_EPS) * nw).astype(BF16)

    h = norm(x_ref)

    def proj(hh, lo, width):
        return _dot(hh, w_ref[:, lo:lo + width])

    blocks_per_seq = SEQ // TM_IN
    j = lax.rem(pl.program_id(0), blocks_per_seq)
    prev = proj(norm(xp_ref), C_DQKV, 3 * HW)
    nxt = proj(norm(xn_ref), C_DQKV, 3 * HW)
    ext_ref[0:HALO, :] = jnp.where(j == 0, 0.0, prev)
    ext_ref[HALO:HALO + TM_IN, :] = proj(h, C_DQKV, 3 * HW)
    ext_ref[HALO + TM_IN:2 * HALO + TM_IN, :] = jnp.where(j == blocks_per_seq - 1, 0.0, nxt)
    cw = cw_ref[...]
    base = HALO - GDN_CONV // 2
    acc = ext_ref[base:base + TM_IN, :] * cw[0:1, :]
    for t in range(1, GDN_CONV):
        acc = acc + ext_ref[base + t:base + t + TM_IN, :] * cw[t:t + 1, :]
    y = _silu(acc)
    for hd in range(HEADS):
        for which, (ref, scale) in enumerate(((q2_ref, GDN_DK ** -0.5), (k2_ref, 1.0))):
            s = y[:, which * HW + hd * LANES:which * HW + (hd + 1) * LANES]
            ss = jnp.sum(s * s, axis=-1, keepdims=True)
            ref[:, hd * LANES:(hd + 1) * LANES] = (s * lax.rsqrt(ss + NORM_EPS) * scale).astype(BF16)
    v2_ref[...] = y[:, 2 * HW:3 * HW].astype(BF16)

    zg_ref[...] = _silu(proj(h, C_Z, MIX_PAD_W)).astype(BF16)
    xaq_ref[...] = (proj(h, C_XQ, XA_W) * (XA_DH ** -0.5)).astype(BF16)

    pg = proj(h, C_GQ, 2 * GLA_QK_W + HW)
    q1 = pg[:, 0:GLA_QK_W] * (GLA_DK ** -0.5)
    k1 = pg[:, GLA_QK_W:2 * GLA_QK_W]
    v1_ref[...] = pg[:, 2 * GLA_QK_W:].astype(BF16)
    m = proj(h, C_MISC, LANES)
    logits = _dot(m.astype(BF16), w2_ref[...]) + gb_ref[...]
    g = _log_sigmoid(logits) * (1.0 / GLA_GATE_NORMALIZER)

    lane = lax.broadcasted_iota(jnp.int32, (1, LANES), 1)
    is_a = (lane >= M_A) & (lane < M_A + 8 * M_A_COPIES)
    neg_a = jnp.where(is_a, -jnp.exp(gp_ref[0:1, :]), 0.0)
    gg = neg_a * _softplus(m + gp_ref[1:2, :])
    beta = _sigmoid(m)
    fwd_lane = lax.rem(lane, 8) < GDN_HEADS

    ri, ci = _tri_masks()
    lower = jnp.where(ri >= ci, 1.0, 0.0).astype(BF16)
    upper = jnp.where(ri <= ci, 1.0, 0.0).astype(BF16)

    for c in range(TM_IN // CHUNK):
        rows = slice(c * CHUNK, (c + 1) * CHUNK)
        gch = g[rows]
        q1c = q1[rows]
        k1c = k1[rows]
        bf = _tri_sum(lower, gch[:, 0:GLA_QK_W])
        br = _tri_sum(upper, gch[:, GLA_QK_W:])
        for d, (b, last) in enumerate(((bf, bf[CHUNK - 1:CHUNK]), (br, br[0:1]))):
            off = d * 3 * GLA_QK_W
            gla_ref[rows, off:off + GLA_QK_W] = (q1c * jnp.exp(b)).astype(BF16)
            gla_ref[rows, off + GLA_QK_W:off + 2 * GLA_QK_W] = (k1c * jnp.exp(-b)).astype(BF16)
            gla_ref[rows, off + 2 * GLA_QK_W:off + 3 * GLA_QK_W] = (k1c * jnp.exp(last - b)).astype(BF16)
            dec_ref[c, :, d * GLA_QK_W:(d + 1) * GLA_QK_W] = jnp.exp(last)

        ggc = gg[rows]
        pf = _tri_sum(lower, ggc)
        sf = _tri_sum(upper, ggc)
        gc = jnp.where(fwd_lane, pf, sf)
        last = jnp.where(fwd_lane, pf[CHUNK - 1:CHUNK], sf[0:1])
        col = jnp.where(lane < G_GC, beta[rows],
              jnp.where(lane < G_EGC, gc,
              jnp.where(lane < G_EKEND, jnp.exp(gc),
              jnp.where(lane < G_DEC, jnp.exp(last - gc), jnp.exp(last)))))
        gcol_ref[rows, :] = col
        grow_ref[c] = col.T[0:ROW_SLAB, :]


def _inproj(x2d, norm_w, w_all, conv_w, w2bd, gla_b, gdn_params):
    n_tok = x2d.shape[0]
    nblk = n_tok // TM_IN
    halo_blocks = TM_IN // HALO
    n_halo = n_tok // HALO
    nchunk = TM_IN // CHUNK
    tok = lambda w: pl.BlockSpec((TM_IN, w), lambda i: (i, 0))
    const = lambda shape: pl.BlockSpec(shape, lambda i: tuple(0 for _ in shape))
    out_shapes = (
        jax.ShapeDtypeStruct((n_tok, 6 * GLA_QK_W), BF16),
        jax.ShapeDtypeStruct((n_tok, HW), BF16),
        jax.ShapeDtypeStruct((n_tok // CHUNK, 1, 2 * GLA_QK_W), F32),
        jax.ShapeDtypeStruct((n_tok, MIX_PAD_W), BF16),
        jax.ShapeDtypeStruct((n_tok, HW), BF16),
        jax.ShapeDtypeStruct((n_tok, HW), BF16),
        jax.ShapeDtypeStruct((n_tok, HW), BF16),
        jax.ShapeDtypeStruct((n_tok, LANES), F32),
        jax.ShapeDtypeStruct((n_tok // CHUNK, ROW_SLAB, CHUNK), F32),
        jax.ShapeDtypeStruct((n_tok, XA_W), BF16),
    )
    out_specs = (
        tok(6 * GLA_QK_W), tok(HW),
        pl.BlockSpec((nchunk, 1, 2 * GLA_QK_W), lambda i: (i, 0, 0)),
        tok(MIX_PAD_W), tok(HW), tok(HW), tok(HW), tok(LANES),
        pl.BlockSpec((nchunk, ROW_SLAB, CHUNK), lambda i: (i, 0, 0)),
        tok(XA_W),
    )
    return pl.pallas_call(
        _inproj_kernel,
        grid=(nblk,),
        in_specs=[
            tok(D_MODEL),
            pl.BlockSpec((HALO, D_MODEL), lambda i: (jnp.maximum(i * halo_blocks - 1, 0), 0)),
            pl.BlockSpec((HALO, D_MODEL), lambda i: (jnp.minimum((i + 1) * halo_blocks, n_halo - 1), 0)),
            const((1, D_MODEL)),
            const((D_MODEL, IN_PAD_W)),
            const((8, 3 * HW)),
            const((LANES, 2 * GLA_QK_W)),
            const((1, 2 * GLA_QK_W)),
            const((8, LANES)),
        ],
        out_specs=out_specs,
        out_shape=out_shapes,
        scratch_shapes=[pltpu.VMEM((TM_IN + 2 * HALO, 3 * HW), F32)],
        compiler_params=pltpu.CompilerParams(dimension_semantics=("parallel",),
                                             vmem_limit_bytes=VMEM_LIMIT),
        name="inproj",
    )(x2d, x2d, x2d, norm_w, w_all, conv_w, w2bd, gla_b, gdn_params)


def _gla_kernel(gf_ref, gb_ref, vf_ref, vb_ref, decf_ref, decb_ref, of_ref, ob_ref, s_ref):
    @pl.when(pl.program_id(1) == 0)
    def _():
        s_ref[...] = jnp.zeros_like(s_ref)

    ri, ci = _tri_masks()
    incl = (ri >= ci, ri <= ci)
    lane_tok = lax.broadcasted_iota(jnp.int32, (CHUNK, LANES), 1)
    head_lanes = (lane_tok < GLA_DK, lane_tok >= GLA_DK)
    lane_sq = lax.broadcasted_iota(jnp.int32, (LANES, LANES), 1) < GLA_DK
    zero_bf = jnp.zeros((CHUNK, LANES), BF16)

    for step in range(NC):
        for d in range(2):
            c = step if d == 0 else NC - 1 - step
            g_ref = (gf_ref, gb_ref)[d]
            v_ref = (vf_ref, vb_ref)[d]
            dec_ref = (decf_ref, decb_ref)[d]
            o_ref = (of_ref, ob_ref)[d]
            rows = slice(c * CHUNK, (c + 1) * CHUNK)
            for p in range(GLA_HEADS // 2):
                lanes = slice(p * LANES, (p + 1) * LANES)
                qe = g_ref[rows, p * LANES:(p + 1) * LANES]
                ke = g_ref[rows, GLA_QK_W + p * LANES:GLA_QK_W + (p + 1) * LANES]
                kend = g_ref[rows, 2 * GLA_QK_W + p * LANES:2 * GLA_QK_W + (p + 1) * LANES]
                st = s_ref[d, p]
                stb = st.astype(BF16)
                kvs = []
                for hh in range(2):
                    hd = 2 * p + hh
                    qm = jnp.where(head_lanes[hh], qe, zero_bf)
                    a = jnp.where(incl[d], _dot_nt(qm, ke), 0.0).astype(BF16)
                    v = v_ref[rows, hd * LANES:(hd + 1) * LANES]
                    o_ref[rows, hd * LANES:(hd + 1) * LANES] = _dot(a, v) + _dot_nt(qm, stb)
                    kvs.append(_dot_tn(v, kend))
                dec = dec_ref[c, :, lanes]
                s_ref[d, p] = dec * st + jnp.where(lane_sq, kvs[0], kvs[1])


def _gla_scan(gla, v1, dec):
    nb = SEQ // TB
    fwd = lambda b, i: (b * nb + i, 0)
    bwd = lambda b, i: (b * nb + nb - 1 - i, 0)
    n_tok = gla.shape[0]
    return pl.pallas_call(
        _gla_kernel,
        grid=(BATCH, nb),
        in_specs=[
            pl.BlockSpec((TB, 3 * GLA_QK_W), fwd),
            pl.BlockSpec((TB, 3 * GLA_QK_W), lambda b, i: (b * nb + nb - 1 - i, 1)),
            pl.BlockSpec((TB, HW), fwd),
            pl.BlockSpec((TB, HW), bwd),
            pl.BlockSpec((NC, 1, GLA_QK_W), lambda b, i: (b * nb + i, 0, 0)),
            pl.BlockSpec((NC, 1, GLA_QK_W), lambda b, i: (b * nb + nb - 1 - i, 0, 1)),
        ],
        out_specs=(pl.BlockSpec((TB, HW), fwd), pl.BlockSpec((TB, HW), bwd)),
        out_shape=(jax.ShapeDtypeStruct((n_tok, HW), F32), jax.ShapeDtypeStruct((n_tok, HW), F32)),
        scratch_shapes=[pltpu.VMEM((2, GLA_HEADS // 2, LANES, LANES), F32)],
        compiler_params=pltpu.CompilerParams(dimension_semantics=("parallel", "arbitrary"),
                                             vmem_limit_bytes=VMEM_LIMIT),
        name="gla_scan",
    )(gla, gla, v1, v1, dec, dec)


def _gdn_kernel(qf_ref, kf_ref, vf_ref, colf_ref, rowf_ref, qb_ref, kb_ref, vb_ref, colb_ref, rowb_ref,
                of_ref, ob_ref, s_ref):
    @pl.when(pl.program_id(1) == 0)
    def _():
        s_ref[...] = jnp.zeros_like(s_ref)

    ri, ci = _tri_masks()
    incl = (ri >= ci, ri <= ci)
    strict = (ri > ci, ri < ci)
    eye = jnp.where(ri == ci, 1.0, 0.0)

    for step in range(NC):
        for d in range(2):
            c = step if d == 0 else NC - 1 - step
            q_ref, k_ref, v_ref, col_ref, row_ref, o_ref = (
                (qf_ref, kf_ref, vf_ref, colf_ref, rowf_ref, of_ref),
                (qb_ref, kb_ref, vb_ref, colb_ref, rowb_ref, ob_ref))[d]
            rows = slice(c * CHUNK, (c + 1) * CHUNK)
            col = col_ref[rows, :]
            rowt = row_ref[c]
            for hd in range(GDN_HEADS):
                idx = d * GDN_HEADS + hd
                lanes = slice(hd * LANES, (hd + 1) * LANES)
                q = q_ref[rows, lanes]
                k = k_ref[rows, lanes]
                v = v_ref[rows, lanes]
                beta_c = col[:, G_BETA + idx:G_BETA + idx + 1]
                gc_c = col[:, G_GC + idx:G_GC + idx + 1]
                egc_c = col[:, G_EGC + idx:G_EGC + idx + 1]
                ekend_c = col[:, G_EKEND + idx:G_EKEND + idx + 1]
                dec = col[0:1, G_DEC + idx:G_DEC + idx + 1]
                beta_r = rowt[idx:idx + 1, :]
                gc_r = rowt[8 + idx:8 + idx + 1, :]

                kq = _dot_nt(jnp.concatenate([k, q], axis=0), k)
                dm = incl[d]
                decay = jnp.where(dm, jnp.exp(jnp.where(dm, gc_c - gc_r, 0.0)), 0.0)
                n = jnp.where(strict[d], -(kq[0:CHUNK] * decay * beta_c), 0.0)
                nb16 = n.astype(BF16)
                pm = eye + n
                cur = _dot(nb16, nb16)
                for _ in range(4):
                    cb = cur.astype(BF16)
                    st = _dot(jnp.concatenate([pm.astype(BF16), cb], axis=0), cb)
                    pm = pm + st[0:CHUNK]
                    cur = st[CHUNK:2 * CHUNK]
                tinv = pm + _dot(pm.astype(BF16), cur.astype(BF16))

                u = _dot((tinv * beta_r).astype(BF16), v)
                w = _dot((tinv * (beta_r * jnp.exp(gc_r))).astype(BF16), k)
                s = s_ref[d, hd]
                r = _dot(jnp.concatenate([w.astype(BF16), q], axis=0), s.astype(BF16))
                vn = u - r[0:CHUNK]
                qk = jnp.where(dm, kq[CHUNK:2 * CHUNK] * decay, 0.0).astype(BF16)
                o_ref[rows, lanes] = egc_c * r[CHUNK:2 * CHUNK] + _dot(qk, vn.astype(BF16))
                s_ref[d, hd] = dec * s + _dot_tn(k, (ekend_c * vn).astype(BF16))


def _gdn_scan(q2, k2, v2, gcol, grow):
    nb = SEQ // TB
    n_tok = q2.shape[0]
    fwd = lambda b, i: (b * nb + i, 0)
    bwd = lambda b, i: (b * nb + nb - 1 - i, 0)
    fwd3 = lambda b, i: (b * nb + i, 0, 0)
    bwd3 = lambda b, i: (b * nb + nb - 1 - i, 0, 0)

    def specs(m2, m3):
        return [pl.BlockSpec((TB, HW), m2), pl.BlockSpec((TB, HW), m2), pl.BlockSpec((TB, HW), m2),
                pl.BlockSpec((TB, LANES), m2), pl.BlockSpec((NC, ROW_SLAB, CHUNK), m3)]

    return pl.pallas_call(
        _gdn_kernel,
        grid=(BATCH, nb),
        in_specs=specs(fwd, fwd3) + specs(bwd, bwd3),
        out_specs=(pl.BlockSpec((TB, HW), fwd), pl.BlockSpec((TB, HW), bwd)),
        out_shape=(jax.ShapeDtypeStruct((n_tok, HW), F32), jax.ShapeDtypeStruct((n_tok, HW), F32)),
        scratch_shapes=[pltpu.VMEM((2, GDN_HEADS, LANES, LANES), F32)],
        compiler_params=pltpu.CompilerParams(dimension_semantics=("parallel", "arbitrary"),
                                             vmem_limit_bytes=VMEM_LIMIT),
        name="gdn_scan",
    )(q2, k2, v2, gcol, grow, q2, k2, v2, gcol, grow)


def _outproj_kernel(o1f_ref, o1b_ref, o2f_ref, o2b_ref, zg_ref, xaq_ref, mkv_ref, x_ref, wout_ref,
                    n1_ref, n2_ref, n3_ref, fn_ref, out_ref, *, final):
    def head_norm(o, nw_ref, width):
        parts = []
        for hd in range(HEADS):
            s = o[:, hd * LANES:(hd + 1) * LANES]
            ms = jnp.sum(s * s, axis=-1, keepdims=True) * (1.0 / width)
            parts.append(s * lax.rsqrt(ms + NORM_EPS))
        return jnp.concatenate(parts, axis=-1) * nw_ref[...]

    zg = zg_ref[...].astype(F32)
    o1 = head_norm(o1f_ref[...] + o1b_ref[...], n1_ref, GLA_DV) * zg[:, 0:HW]
    o2 = head_norm(o2f_ref[...] + o2b_ref[...], n2_ref, GDN_DV) * zg[:, HW:2 * HW]
    y = _dot(o1.astype(BF16), wout_ref[0:HW, :]) + _dot(o2.astype(BF16), wout_ref[HW:2 * HW, :])

    lane_q = lax.broadcasted_iota(jnp.int32, (TM_OUT, LANES), 1)
    lane_m = lax.broadcasted_iota(jnp.int32, (MEM_LEN, LANES), 1)
    for p in range(XA_HEADS // 2):
        lanes = slice(p * LANES, (p + 1) * LANES)
        qpair = xaq_ref[:, lanes]
        mk = mkv_ref[0, :, p * LANES:(p + 1) * LANES]
        mv = mkv_ref[0, :, XA_W + p * LANES:XA_W + (p + 1) * LANES]
        acc = jnp.zeros((TM_OUT, LANES), F32)
        for hh in range(2):
            in_head_q = (lane_q >= hh * XA_DH) & (lane_q < (hh + 1) * XA_DH)
            in_head_m = (lane_m >= hh * XA_DH) & (lane_m < (hh + 1) * XA_DH)
            qm = jnp.where(in_head_q, qpair, jnp.zeros_like(qpair))
            sc = _dot_nt(qm, mk)
            e = jnp.exp(sc - jnp.max(sc, axis=-1, keepdims=True))
            l = jnp.sum(e, axis=-1, keepdims=True)
            mvm = jnp.where(in_head_m, mv, jnp.zeros_like(mv))
            acc = acc + _dot(e.astype(BF16), mvm) * (1.0 / l)
        sq = acc * acc
        first = lane_q < XA_DH
        ss0 = jnp.sum(jnp.where(first, sq, 0.0), axis=-1, keepdims=True)
        ss1 = jnp.sum(jnp.where(first, 0.0, sq), axis=-1, keepdims=True)
        ms = jnp.where(first, ss0, ss1) * (1.0 / XA_DH)
        o3 = acc * lax.rsqrt(ms + NORM_EPS) * n3_ref[:, lanes] * zg[:, 2 * HW + p * LANES:2 * HW + (p + 1) * LANES]
        y = y + _dot(o3.astype(BF16), wout_ref[2 * HW + p * LANES:2 * HW + (p + 1) * LANES, :])

    xo = x_ref[...] + y
    if final:
        ms = jnp.mean(xo * xo, axis=-1, keepdims=True)
        xo = xo * lax.rsqrt(ms + NORM_EPS) * fn_ref[...]
    out_ref[...] = xo


def _outproj(o1f, o1b, o2f, o2b, zg, xaq, mkv, x2d, wout, n1, n2, n3, fnw, final):
    n_tok = x2d.shape[0]
    blocks_per_seq = SEQ // TM_OUT
    tok = lambda w: pl.BlockSpec((TM_OUT, w), lambda i: (i, 0))
    const = lambda shape: pl.BlockSpec(shape, lambda i: tuple(0 for _ in shape))
    return pl.pallas_call(
        functools.partial(_outproj_kernel, final=final),
        grid=(n_tok // TM_OUT,),
        in_specs=[
            tok(HW), tok(HW), tok(HW), tok(HW), tok(MIX_PAD_W), tok(XA_W),
            pl.BlockSpec((1, MEM_LEN, 2 * XA_W), lambda i: (i // blocks_per_seq, 0, 0)),
            tok(D_MODEL),
            const((MIX_PAD_W, D_MODEL)),
            const((1, HW)), const((1, HW)), const((1, XA_W)), const((1, D_MODEL)),
        ],
        out_specs=tok(D_MODEL),
        out_shape=jax.ShapeDtypeStruct((n_tok, D_MODEL), F32),
        compiler_params=pltpu.CompilerParams(dimension_semantics=("parallel",),
                                             vmem_limit_bytes=VMEM_LIMIT),
        name="outproj",
    )(o1f, o1b, o2f, o2b, zg, xaq, mkv, x2d, wout, n1, n2, n3, fnw)


def _pad_heads(w, axis):
    shp = w.shape
    w = w.reshape(shp[:axis] + (HEADS, GLA_DV) + shp[axis + 1:])
    pad = [(0, 0)] * w.ndim
    pad[axis + 1] = (0, LANES - GLA_DV)
    w = jnp.pad(w, pad)
    return w.reshape(shp[:axis] + (HW,) + shp[axis + 1:])


def _pack_layer(w_in, gla_w2, gla_b, gla_norm_w, gdn_conv_w, gdn_a_log, gdn_dt_bias, gdn_norm_w,
                xa_norm_w, w_out):
    sizes = (GLA_QK_W, GLA_QK_W, HEADS * GLA_DV, HEADS * GLA_DV, 2 * GLA_RANK, 3 * HEADS * GDN_DK,
             HEADS * GDN_DV, 2 * GDN_HEADS, 2 * GDN_HEADS, XA_W, XA_W)
    cols, start = [], 0
    for s in sizes:
        cols.append(w_in[:, start:start + s])
        start += s
    gq, gk, gv, gz, glr, dqkv, dz, db, da, xq, xz = cols
    hd = HEADS * GDN_DK
    misc = jnp.concatenate(
        [db] + [da] * M_A_COPIES
        + [jnp.zeros((D_MODEL, M_LR - M_A - 8 * M_A_COPIES), F32), glr,
           jnp.zeros((D_MODEL, LANES - M_LR - 2 * GLA_RANK), F32)], axis=1)
    w_all = jnp.concatenate(
        [gq, gk, _pad_heads(gv, 1),
         _pad_heads(dqkv[:, 0:hd], 1), _pad_heads(dqkv[:, hd:2 * hd], 1), _pad_heads(dqkv[:, 2 * hd:], 1),
         _pad_heads(gz, 1), _pad_heads(dz, 1), xz, xq, misc], axis=1).astype(BF16)

    cw = jnp.transpose(gdn_conv_w, (1, 0))
    cw = jnp.concatenate([_pad_heads(cw[:, 0:hd], 1), _pad_heads(cw[:, hd:2 * hd], 1),
                          _pad_heads(cw[:, 2 * hd:], 1)], axis=1)
    cw = jnp.pad(cw, ((0, 8 - GDN_CONV), (0, 0)))

    w2bd = jnp.zeros((LANES, 2 * GLA_QK_W), F32)
    w2bd = w2bd.at[M_LR:M_LR + GLA_RANK, 0:GLA_QK_W].set(gla_w2[0])
    w2bd = w2bd.at[M_LR + GLA_RANK:M_LR + 2 * GLA_RANK, GLA_QK_W:].set(gla_w2[1])
    w2bd = w2bd.astype(BF16)
    glab = gla_b.reshape(1, 2 * GLA_QK_W)

    def a_slab(p):
        flat = p.reshape(2 * GDN_HEADS)
        return jnp.concatenate([jnp.zeros((M_A,), F32)] + [flat] * M_A_COPIES
                               + [jnp.zeros((LANES - M_A - 8 * M_A_COPIES,), F32)])

    gparams = jnp.zeros((8, LANES), F32).at[0].set(a_slab(gdn_a_log)).at[1].set(a_slab(gdn_dt_bias))

    wout = jnp.concatenate([_pad_heads(w_out[0:HEADS * GLA_DV], 0),
                            _pad_heads(w_out[HEADS * GLA_DV:HEADS * (GLA_DV + GDN_DV)], 0),
                            w_out[HEADS * (GLA_DV + GDN_DV):]], axis=0).astype(BF16)
    pad_norm = lambda w: jnp.tile(jnp.pad(w, (0, LANES - w.shape[0])), HEADS).reshape(1, HW)
    n1 = pad_norm(gla_norm_w)
    n2 = pad_norm(gdn_norm_w)
    n3 = jnp.tile(xa_norm_w, XA_HEADS).reshape(1, XA_W)
    return w_all, cw, w2bd, glab, gparams, wout, n1, n2, n3


def kernel(x, mem, norm_w, w_in, gla_w2, gla_b, gla_norm_w, gdn_conv_w, gdn_a_log, gdn_dt_bias,
           gdn_norm_w, mem_norm_w, xa_w_kv, xa_norm_w, w_out, final_norm_w):
    assert x.shape == (BATCH, SEQ, D_MODEL) and mem.shape == (BATCH, MEM_LEN, D_MODEL)
    mkv = _memkv(mem, mem_norm_w, xa_w_kv.astype(BF16))
    h = x.reshape(BATCH * SEQ, D_MODEL)
    fnw = final_norm_w.reshape(1, D_MODEL)
    for l in range(DEPTH):
        w_all, cw, w2bd, glab, gparams, wout, n1, n2, n3 = _pack_layer(
            w_in[l], gla_w2[l], gla_b[l], gla_norm_w[l], gdn_conv_w[l], gdn_a_log[l], gdn_dt_bias[l],
            gdn_norm_w[l], xa_norm_w[l], w_out[l])
        gla, v1, dec, zg, q2, k2, v2, gcol, grow, xaq = _inproj(
            h, norm_w[l].reshape(1, D_MODEL), w_all, cw, w2bd, glab, gparams)
        o1f, o1b = _gla_scan(gla, v1, dec)
        o2f, o2b = _gdn_scan(q2, k2, v2, gcol, grow)
        h = _outproj(o1f, o1b, o2f, o2b, zg, xaq, mkv[l], h, wout, n1, n2, n3, fnw, l == DEPTH - 1)
    return h.reshape(BATCH, SEQ, D_MODEL)
```

```python
import functools

import jax
import jax.numpy as jnp
from jax import lax
from jax.experimental import pallas as pl
from jax.experimental.pallas import tpu as pltpu

F32 = jnp.float32
BF16 = jnp.bfloat16

D_MODEL = 1024
BATCH = 8
SEQ = 4096
DEPTH = 2
MEM_LEN = 256
CHUNK = 64
NORM_EPS = 1e-6
GLA_HEADS = 4
GLA_DK = 64
GLA_DV = 96
GLA_RANK = 16
GLA_GATE_NORMALIZER = 16.0
GDN_HEADS = 4
GDN_DK = 96
GDN_DV = 96
GDN_CONV = 5
XA_HEADS = 4
XA_DH = 64

LANES = 128
HEADS = 4
HW = HEADS * LANES
GLA_QK_W = GLA_HEADS * GLA_DK
XA_W = XA_HEADS * XA_DH
MIX_PAD_W = 2 * HW + XA_W

C_GQ = 0
C_GK = C_GQ + GLA_QK_W
C_GV = C_GK + GLA_QK_W
C_DQKV = C_GV + HW
C_Z = C_DQKV + 3 * HW
C_XQ = C_Z + MIX_PAD_W
C_MISC = C_XQ + XA_W
IN_PAD_W = C_MISC + LANES

M_BETA = 0
M_A = 8
M_A_COPIES = 4
M_LR = 64
G_BETA, G_GC, G_EGC, G_EKEND, G_DEC = 0, 8, 16, 24, 32
ROW_SLAB = 16

TM_IN = 256
TM_OUT = 256
NC = 4
TB = NC * CHUNK
HALO = 8
VMEM_LIMIT = 56 * 1024 * 1024


def _dot(a, b):
    return jnp.dot(a, b, preferred_element_type=F32)


def _dot_nt(a, b):
    return lax.dot_general(a, b, (((1,), (1,)), ((), ())), preferred_element_type=F32)


def _dot_tn(a, b):
    return lax.dot_general(a, b, (((0,), (0,)), ((), ())), preferred_element_type=F32)


def _sigmoid(x):
    return 1.0 / (1.0 + jnp.exp(-x))


def _silu(x):
    return x * _sigmoid(x)


def _softplus(x):
    return jnp.maximum(x, 0.0) + jnp.log1p(jnp.exp(-jnp.abs(x)))


def _log_sigmoid(x):
    return jnp.minimum(x, 0.0) - jnp.log1p(jnp.exp(-jnp.abs(x)))


def _split2(x):
    hi = x.astype(BF16)
    lo = (x - hi.astype(F32)).astype(BF16)
    return hi, lo


def _tri_sum(tri, x):
    hi, lo = _split2(x)
    return _dot(tri, hi) + _dot(tri, lo)


def _tri_masks():
    ri = lax.broadcasted_iota(jnp.int32, (CHUNK, CHUNK), 0)
    ci = lax.broadcasted_iota(jnp.int32, (CHUNK, CHUNK), 1)
    return ri, ci


def _memkv_kernel(mem_ref, nw_ref, w_ref, out_ref):
    m = mem_ref[0]
    ms = jnp.mean(m * m, axis=-1, keepdims=True)
    mn = (m * lax.rsqrt(ms + NORM_EPS) * nw_ref[0]).astype(BF16)
    out_ref[0, 0] = _dot(mn, w_ref[0]).astype(BF16)


def _memkv(mem, mem_norm_w, xa_w_kv_bf16):
    return pl.pallas_call(
        _memkv_kernel,
        grid=(DEPTH, BATCH),
        in_specs=[
            pl.BlockSpec((1, MEM_LEN, D_MODEL), lambda l, b: (b, 0, 0)),
            pl.BlockSpec((1, 1, D_MODEL), lambda l, b: (l, 0, 0)),
            pl.BlockSpec((1, D_MODEL, 2 * XA_W), lambda l, b: (l, 0, 0)),
        ],
        out_specs=pl.BlockSpec((1, 1, MEM_LEN, 2 * XA_W), lambda l, b: (l, b, 0, 0)),
        out_shape=jax.ShapeDtypeStruct((DEPTH, BATCH, MEM_LEN, 2 * XA_W), BF16),
        compiler_params=pltpu.CompilerParams(dimension_semantics=("parallel", "parallel")),
        name="memkv",
    )(mem, mem_norm_w.reshape(DEPTH, 1, D_MODEL), xa_w_kv_bf16)


def _inproj_kernel(x_ref, xp_ref, xn_ref, nw_ref, w_ref, cw_ref, w2_ref, gb_ref, gp_ref,
                   gla_ref, v1_ref, dec_ref, zg_ref, q2_ref, k2_ref, v2_ref, gcol_ref, grow_ref,
                   xaq_ref, ext_ref):
    nw = nw_ref[...]

    def norm(ref):
        x = ref[...]
        ms = jnp.mean(x * x, axis=-1, keepdims=True)
        return (x * lax.rsqrt(ms + NORM_EPS) * nw).astype(BF16)

    h = norm(x_ref)

    def proj(hh, lo, width):
        return _dot(hh, w_ref[:, lo:lo + width])

    blocks_per_seq = SEQ // TM_IN
    j = lax.rem(pl.program_id(0), blocks_per_seq)
    prev = proj(norm(xp_ref), C_DQKV, 3 * HW)
    nxt = proj(norm(xn_ref), C_DQKV, 3 * HW)
    ext_ref[0:HALO, :] = jnp.where(j == 0, 0.0, prev)
    ext_ref[HALO:HALO + TM_IN, :] = proj(h, C_DQKV, 3 * HW)
    ext_ref[HALO + TM_IN:2 * HALO + TM_IN, :] = jnp.where(j == blocks_per_seq - 1, 0.0, nxt)
    cw = cw_ref[...]
    base = HALO - GDN_CONV // 2
    acc = ext_ref[base:base + TM_IN, :] * cw[0:1, :]
    for t in range(1, GDN_CONV):
        acc = acc + ext_ref[base + t:base + t + TM_IN, :] * cw[t:t + 1, :]
    y = _silu(acc)
    for hd in range(HEADS):
        for which, (ref, scale) in enumerate(((q2_ref, GDN_DK ** -0.5), (k2_ref, 1.0))):
            s = y[:, which * HW + hd * LANES:which * HW + (hd + 1) * LANES]
            ss = jnp.sum(s * s, axis=-1, keepdims=True)
            ref[:, hd * LANES:(hd + 1) * LANES] = (s * lax.rsqrt(ss + NORM_EPS) * scale).astype(BF16)
    v2_ref[...] = y[:, 2 * HW:3 * HW].astype(BF16)

    zg_ref[...] = _silu(proj(h, C_Z, MIX_PAD_W)).astype(BF16)
    xaq_ref[...] = (proj(h, C_XQ, XA_W) * (XA_DH ** -0.5)).astype(BF16)

    pg = proj(h, C_GQ, 2 * GLA_QK_W + HW)
    q1 = pg[:, 0:GLA_QK_W] * (GLA_DK ** -0.5)
    k1 = pg[:, GLA_QK_W:2 * GLA_QK_W]
    v1_ref[...] = pg[:, 2 * GLA_QK_W:].astype(BF16)
    m = proj(h, C_MISC, LANES)
    logits = _dot(m.astype(BF16), w2_ref[...]) + gb_ref[...]
    g = _log_sigmoid(logits) * (1.0 / GLA_GATE_NORMALIZER)

    lane = lax.broadcasted_iota(jnp.int32, (1, LANES), 1)
    is_a = (lane >= M_A) & (lane < M_A + 8 * M_A_COPIES)
    neg_a = jnp.where(is_a, -jnp.exp(gp_ref[0:1, :]), 0.0)
    gg = neg_a * _softplus(m + gp_ref[1:2, :])
    beta = _sigmoid(m)
    fwd_lane = lax.rem(lane, 8) < GDN_HEADS

    ri, ci = _tri_masks()
    lower = jnp.where(ri >= ci, 1.0, 0.0).astype(BF16)
    upper = jnp.where(ri <= ci, 1.0, 0.0).astype(BF16)

    for c in range(TM_IN // CHUNK):
        rows = slice(c * CHUNK, (c + 1) * CHUNK)
        gch = g[rows]
        q1c = q1[rows]
        k1c = k1[rows]
        bf = _tri_sum(lower, gch[:, 0:GLA_QK_W])
        br = _tri_sum(upper, gch[:, GLA_QK_W:])
        for d, (b, last) in enumerate(((bf, bf[CHUNK - 1:CHUNK]), (br, br[0:1]))):
            off = d * 3 * GLA_QK_W
            gla_ref[rows, off:off + GLA_QK_W] = (q1c * jnp.exp(b)).astype(BF16)
            gla_ref[rows, off + GLA_QK_W:off + 2 * GLA_QK_W] = (k1c * jnp.exp(-b)).astype(BF16)
            gla_ref[rows, off + 2 * GLA_QK_W:off + 3 * GLA_QK_W] = (k1c * jnp.exp(last - b)).astype(BF16)
            dec_ref[c, :, d * GLA_QK_W:(d + 1) * GLA_QK_W] = jnp.exp(last)

        ggc = gg[rows]
        pf = _tri_sum(lower, ggc)
        sf = _tri_sum(upper, ggc)
        gc = jnp.where(fwd_lane, pf, sf)
        last = jnp.where(fwd_lane, pf[CHUNK - 1:CHUNK], sf[0:1])
        col = jnp.where(lane < G_GC, beta[rows],
              jnp.where(lane < G_EGC, gc,
              jnp.where(lane < G_EKEND, jnp.exp(gc),
              jnp.where(lane < G_DEC, jnp.exp(last - gc), jnp.exp(last)))))
        gcol_ref[rows, :] = col
        grow_ref[c] = col.T[0:ROW_SLAB, :]


def _inproj(x2d, norm_w, w_all, conv_w, w2bd, gla_b, gdn_params):
    n_tok = x2d.shape[0]
    nblk = n_tok // TM_IN
    halo_blocks = TM_IN // HALO
    n_halo = n_tok // HALO
    nchunk = TM_IN // CHUNK
    tok = lambda w: pl.BlockSpec((TM_IN, w), lambda i: (i, 0))
    const = lambda shape: pl.BlockSpec(shape, lambda i: tuple(0 for _ in shape))
    out_shapes = (
        jax.ShapeDtypeStruct((n_tok, 6 * GLA_QK_W), BF16),
        jax.ShapeDtypeStruct((n_tok, HW), BF16),
        jax.ShapeDtypeStruct((n_tok // CHUNK, 1, 2 * GLA_QK_W), F32),
        jax.ShapeDtypeStruct((n_tok, MIX_PAD_W), BF16),
        jax.ShapeDtypeStruct((n_tok, HW), BF16),
        jax.ShapeDtypeStruct((n_tok, HW), BF16),
        jax.ShapeDtypeStruct((n_tok, HW), BF16),
        jax.ShapeDtypeStruct((n_tok, LANES), F32),
        jax.ShapeDtypeStruct((n_tok // CHUNK, ROW_SLAB, CHUNK), F32),
        jax.ShapeDtypeStruct((n_tok, XA_W), BF16),
    )
    out_specs = (
        tok(6 * GLA_QK_W), tok(HW),
        pl.BlockSpec((nchunk, 1, 2 * GLA_QK_W), lambda i: (i, 0, 0)),
        tok(MIX_PAD_W), tok(HW), tok(HW), tok(HW), tok(LANES),
        pl.BlockSpec((nchunk, ROW_SLAB, CHUNK), lambda i: (i, 0, 0)),
        tok(XA_W),
    )
    return pl.pallas_call(
        _inproj_kernel,
        grid=(nblk,),
        in_specs=[
            tok(D_MODEL),
            pl.BlockSpec((HALO, D_MODEL), lambda i: (jnp.maximum(i * halo_blocks - 1, 0), 0)),
            pl.BlockSpec((HALO, D_MODEL), lambda i: (jnp.minimum((i + 1) * halo_blocks, n_halo - 1), 0)),
            const((1, D_MODEL)),
            const((D_MODEL, IN_PAD_W)),
            const((8, 3 * HW)),
            const((LANES, 2 * GLA_QK_W)),
            const((1, 2 * GLA_QK_W)),
            const((8, LANES)),
        ],
        out_specs=out_specs,
        out_shape=out_shapes,
        scratch_shapes=[pltpu.VMEM((TM_IN + 2 * HALO, 3 * HW), F32)],
        compiler_params=pltpu.CompilerParams(dimension_semantics=("parallel",),
                                             vmem_limit_bytes=VMEM_LIMIT),
        name="inproj",
    )(x2d, x2d, x2d, norm_w, w_all, conv_w, w2bd, gla_b, gdn_params)


def _gla_kernel(gf_ref, gb_ref, vf_ref, vb_ref, decf_ref, decb_ref, of_ref, ob_ref, s_ref):
    @pl.when(pl.program_id(1) == 0)
    def _():
        s_ref[...] = jnp.zeros_like(s_ref)

    ri, ci = _tri_masks()
    incl = (ri >= ci, ri <= ci)
    lane_tok = lax.broadcasted_iota(jnp.int32, (CHUNK, LANES), 1)
    head_lanes = (lane_tok < GLA_DK, lane_tok >= GLA_DK)
    lane_sq = lax.broadcasted_iota(jnp.int32, (LANES, LANES), 1) < GLA_DK
    zero_bf = jnp.zeros((CHUNK, LANES), BF16)

    for step in range(NC):
        for d in range(2):
            c = step if d == 0 else NC - 1 - step
            g_ref = (gf_ref, gb_ref)[d]
            v_ref = (vf_ref, vb_ref)[d]
            dec_ref = (decf_ref, decb_ref)[d]
            o_ref = (of_ref, ob_ref)[d]
            rows = slice(c * CHUNK, (c + 1) * CHUNK)
            for p in range(GLA_HEADS // 2):
                lanes = slice(p * LANES, (p + 1) * LANES)
                qe = g_ref[rows, p * LANES:(p + 1) * LANES]
                ke = g_ref[rows, GLA_QK_W + p * LANES:GLA_QK_W + (p + 1) * LANES]
                kend = g_ref[rows, 2 * GLA_QK_W + p * LANES:2 * GLA_QK_W + (p + 1) * LANES]
                st = s_ref[d, p]
                stb = st.astype(BF16)
                kvs = []
                for hh in range(2):
                    hd = 2 * p + hh
                    qm = jnp.where(head_lanes[hh], qe, zero_bf)
                    a = jnp.where(incl[d], _dot_nt(qm, ke), 0.0).astype(BF16)
                    v = v_ref[rows, hd * LANES:(hd + 1) * LANES]
                    o_ref[rows, hd * LANES:(hd + 1) * LANES] = _dot(a, v) + _dot_nt(qm, stb)
                    kvs.append(_dot_tn(v, kend))
                dec = dec_ref[c, :, lanes]
                s_ref[d, p] = dec * st + jnp.where(lane_sq, kvs[0], kvs[1])


def _gla_scan(gla, v1, dec):
    nb = SEQ // TB
    fwd = lambda b, i: (b * nb + i, 0)
    bwd = lambda b, i: (b * nb + nb - 1 - i, 0)
    n_tok = gla.shape[0]
    return pl.pallas_call(
        _gla_kernel,
        grid=(BATCH, nb),
        in_specs=[
            pl.BlockSpec((TB, 3 * GLA_QK_W), fwd),
            pl.BlockSpec((TB, 3 * GLA_QK_W), lambda b, i: (b * nb + nb - 1 - i, 1)),
            pl.BlockSpec((TB, HW), fwd),
            pl.BlockSpec((TB, HW), bwd),
            pl.BlockSpec((NC, 1, GLA_QK_W), lambda b, i: (b * nb + i, 0, 0)),
            pl.BlockSpec((NC, 1, GLA_QK_W), lambda b, i: (b * nb + nb - 1 - i, 0, 1)),
        ],
        out_specs=(pl.BlockSpec((TB, HW), fwd), pl.BlockSpec((TB, HW), bwd)),
        out_shape=(jax.ShapeDtypeStruct((n_tok, HW), F32), jax.ShapeDtypeStruct((n_tok, HW), F32)),
        scratch_shapes=[pltpu.VMEM((2, GLA_HEADS // 2, LANES, LANES), F32)],
        compiler_params=pltpu.CompilerParams(dimension_semantics=("parallel", "arbitrary"),
                                             vmem_limit_bytes=VMEM_LIMIT),
        name="gla_scan",
    )(gla, gla, v1, v1, dec, dec)


def _gdn_kernel(qf_ref, kf_ref, vf_ref, colf_ref, rowf_ref, qb_ref, kb_ref, vb_ref, colb_ref, rowb_ref,
                of_ref, ob_ref, s_ref):
    @pl.when(pl.program_id(1) == 0)
    def _():
        s_ref[...] = jnp.zeros_like(s_ref)

    ri, ci = _tri_masks()
    incl = (ri >= ci, ri <= ci)
    strict = (ri > ci, ri < ci)
    eye = jnp.where(ri == ci, 1.0, 0.0)

    refs = ((qf_ref, kf_ref, vf_ref, colf_ref, rowf_ref, of_ref),
            (qb_ref, kb_ref, vb_ref, colb_ref, rowb_ref, ob_ref))
    units = [(d, hd) for d in range(2) for hd in range(GDN_HEADS)]
    each = lambda fn, *lists: [fn(*args) for args in zip(*lists)]
    top = lambda m: m[0:CHUNK]
    bot = lambda m: m[CHUNK:2 * CHUNK]

    for step in range(NC):
        q, k, v, gates, dm, sm = [], [], [], [], [], []
        for d, hd in units:
            c = step if d == 0 else NC - 1 - step
            q_ref, k_ref, v_ref, col_ref, row_ref, _ = refs[d]
            rows = slice(c * CHUNK, (c + 1) * CHUNK)
            lanes = slice(hd * LANES, (hd + 1) * LANES)
            idx = d * GDN_HEADS + hd
            q.append(q_ref[rows, lanes])
            k.append(k_ref[rows, lanes])
            v.append(v_ref[rows, lanes])
            col = col_ref[rows, :]
            rowt = row_ref[c]
            gates.append(dict(
                beta_c=col[:, G_BETA + idx:G_BETA + idx + 1], gc_c=col[:, G_GC + idx:G_GC + idx + 1],
                egc_c=col[:, G_EGC + idx:G_EGC + idx + 1], ekend_c=col[:, G_EKEND + idx:G_EKEND + idx + 1],
                dec=col[0:1, G_DEC + idx:G_DEC + idx + 1],
                beta_r=rowt[idx:idx + 1, :], gc_r=rowt[8 + idx:8 + idx + 1, :]))
            dm.append(incl[d])
            sm.append(strict[d])

        kq = each(lambda kk, qq: _dot_nt(jnp.concatenate([kk, qq], axis=0), kk), k, q)
        decay = each(lambda g, m: jnp.where(m, jnp.exp(jnp.where(m, g["gc_c"] - g["gc_r"], 0.0)), 0.0),
                     gates, dm)
        n = each(lambda a, dc, g, m: jnp.where(m, -(top(a) * dc * g["beta_c"]), 0.0), kq, decay, gates, sm)
        pm = each(lambda a: eye + a, n)
        cur = each(lambda a: _dot(a.astype(BF16), a.astype(BF16)), n)
        for _ in range(4):
            st = each(lambda p, cc: _dot(jnp.concatenate([p.astype(BF16), cc.astype(BF16)], axis=0),
                                         cc.astype(BF16)), pm, cur)
            pm = each(lambda p, s_: p + top(s_), pm, st)
            cur = each(bot, st)
        tinv = each(lambda p, cc: p + _dot(p.astype(BF16), cc.astype(BF16)), pm, cur)

        u = each(lambda t, g, vv: _dot((t * g["beta_r"]).astype(BF16), vv), tinv, gates, v)
        w = each(lambda t, g, kk: _dot((t * (g["beta_r"] * jnp.exp(g["gc_r"]))).astype(BF16), kk),
                 tinv, gates, k)
        s = [s_ref[d, hd] for d, hd in units]
        r = each(lambda ww, qq, ss: _dot(jnp.concatenate([ww.astype(BF16), qq], axis=0), ss.astype(BF16)),
                 w, q, s)
        vn = each(lambda uu, rr: uu - top(rr), u, r)
        qk = each(lambda a, dc, m: jnp.where(m, bot(a) * dc, 0.0).astype(BF16), kq, decay, dm)
        o = each(lambda g, rr, a, x: g["egc_c"] * bot(rr) + _dot(a, x.astype(BF16)), gates, r, qk, vn)
        s_new = each(lambda g, ss, kk, x: g["dec"] * ss + _dot_tn(kk, (g["ekend_c"] * x).astype(BF16)),
                     gates, s, k, vn)
        for (d, hd), oo, ss in zip(units, o, s_new):
            c = step if d == 0 else NC - 1 - step
            refs[d][5][c * CHUNK:(c + 1) * CHUNK, hd * LANES:(hd + 1) * LANES] = oo
            s_ref[d, hd] = ss


def _gdn_scan(q2, k2, v2, gcol, grow):
    nb = SEQ // TB
    n_tok = q2.shape[0]
    fwd = lambda b, i: (b * nb + i, 0)
    bwd = lambda b, i: (b * nb + nb - 1 - i, 0)
    fwd3 = lambda b, i: (b * nb + i, 0, 0)
    bwd3 = lambda b, i: (b * nb + nb - 1 - i, 0, 0)

    def specs(m2, m3):
        return [pl.BlockSpec((TB, HW), m2), pl.BlockSpec((TB, HW), m2), pl.BlockSpec((TB, HW), m2),
                pl.BlockSpec((TB, LANES), m2), pl.BlockSpec((NC, ROW_SLAB, CHUNK), m3)]

    return pl.pallas_call(
        _gdn_kernel,
        grid=(BATCH, nb),
        in_specs=specs(fwd, fwd3) + specs(bwd, bwd3),
        out_specs=(pl.BlockSpec((TB, HW), fwd), pl.BlockSpec((TB, HW), bwd)),
        out_shape=(jax.ShapeDtypeStruct((n_tok, HW), F32), jax.ShapeDtypeStruct((n_tok, HW), F32)),
        scratch_shapes=[pltpu.VMEM((2, GDN_HEADS, LANES, LANES), F32)],
        compiler_params=pltpu.CompilerParams(dimension_semantics=("parallel", "arbitrary"),
                                             vmem_limit_bytes=VMEM_LIMIT),
        name="gdn_scan",
    )(q2, k2, v2, gcol, grow, q2, k2, v2, gcol, grow)


def _outproj_kernel(o1f_ref, o1b_ref, o2f_ref, o2b_ref, zg_ref, xaq_ref, mkv_ref, x_ref, wout_ref,
                    n1_ref, n2_ref, n3_ref, fn_ref, out_ref, *, final):
    def head_norm(o, nw_ref, width):
        parts = []
        for hd in range(HEADS):
            s = o[:, hd * LANES:(hd + 1) * LANES]
            ms = jnp.sum(s * s, axis=-1, keepdims=True) * (1.0 / width)
            parts.append(s * lax.rsqrt(ms + NORM_EPS))
        return jnp.concatenate(parts, axis=-1) * nw_ref[...]

    zg = zg_ref[...].astype(F32)
    o1 = head_norm(o1f_ref[...] + o1b_ref[...], n1_ref, GLA_DV) * zg[:, 0:HW]
    o2 = head_norm(o2f_ref[...] + o2b_ref[...], n2_ref, GDN_DV) * zg[:, HW:2 * HW]
    y = _dot(o1.astype(BF16), wout_ref[0:HW, :]) + _dot(o2.astype(BF16), wout_ref[HW:2 * HW, :])

    lane_q = lax.broadcasted_iota(jnp.int32, (TM_OUT, LANES), 1)
    lane_m = lax.broadcasted_iota(jnp.int32, (MEM_LEN, LANES), 1)
    for p in range(XA_HEADS // 2):
        lanes = slice(p * LANES, (p + 1) * LANES)
        qpair = xaq_ref[:, lanes]
        mk = mkv_ref[0, :, p * LANES:(p + 1) * LANES]
        mv = mkv_ref[0, :, XA_W + p * LANES:XA_W + (p + 1) * LANES]
        acc = jnp.zeros((TM_OUT, LANES), F32)
        for hh in range(2):
            in_head_q = (lane_q >= hh * XA_DH) & (lane_q < (hh + 1) * XA_DH)
            in_head_m = (lane_m >= hh * XA_DH) & (lane_m < (hh + 1) * XA_DH)
            qm = jnp.where(in_head_q, qpair, jnp.zeros_like(qpair))
            sc = _dot_nt(qm, mk)
            e = jnp.exp(sc - jnp.max(sc, axis=-1, keepdims=True))
            l = jnp.sum(e, axis=-1, keepdims=True)
            mvm = jnp.where(in_head_m, mv, jnp.zeros_like(mv))
            acc = acc + _dot(e.astype(BF16), mvm) * (1.0 / l)
        sq = acc * acc
        first = lane_q < XA_DH
        ss0 = jnp.sum(jnp.where(first, sq, 0.0), axis=-1, keepdims=True)
        ss1 = jnp.sum(jnp.where(first, 0.0, sq), axis=-1, keepdims=True)
        ms = jnp.where(first, ss0, ss1) * (1.0 / XA_DH)
        o3 = acc * lax.rsqrt(ms + NORM_EPS) * n3_ref[:, lanes] * zg[:, 2 * HW + p * LANES:2 * HW + (p + 1) * LANES]
        y = y + _dot(o3.astype(BF16), wout_ref[2 * HW + p * LANES:2 * HW + (p + 1) * LANES, :])

    xo = x_ref[...] + y
    if final:
        ms = jnp.mean(xo * xo, axis=-1, keepdims=True)
        xo = xo * lax.rsqrt(ms + NORM_EPS) * fn_ref[...]
    out_ref[...] = xo


def _outproj(o1f, o1b, o2f, o2b, zg, xaq, mkv, x2d, wout, n1, n2, n3, fnw, final):
    n_tok = x2d.shape[0]
    blocks_per_seq = SEQ // TM_OUT
    tok = lambda w: pl.BlockSpec((TM_OUT, w), lambda i: (i, 0))
    const = lambda shape: pl.BlockSpec(shape, lambda i: tuple(0 for _ in shape))
    return pl.pallas_call(
        functools.partial(_outproj_kernel, final=final),
        grid=(n_tok // TM_OUT,),
        in_specs=[
            tok(HW), tok(HW), tok(HW), tok(HW), tok(MIX_PAD_W), tok(XA_W),
            pl.BlockSpec((1, MEM_LEN, 2 * XA_W), lambda i: (i // blocks_per_seq, 0, 0)),
            tok(D_MODEL),
            const((MIX_PAD_W, D_MODEL)),
            const((1, HW)), const((1, HW)), const((1, XA_W)), const((1, D_MODEL)),
        ],
        out_specs=tok(D_MODEL),
        out_shape=jax.ShapeDtypeStruct((n_tok, D_MODEL), F32),
        compiler_params=pltpu.CompilerParams(dimension_semantics=("parallel",),
                                             vmem_limit_bytes=VMEM_LIMIT),
        name="outproj",
    )(o1f, o1b, o2f, o2b, zg, xaq, mkv, x2d, wout, n1, n2, n3, fnw)


def _pad_heads(w, axis):
    shp = w.shape
    w = w.reshape(shp[:axis] + (HEADS, GLA_DV) + shp[axis + 1:])
    pad = [(0, 0)] * w.ndim
    pad[axis + 1] = (0, LANES - GLA_DV)
    w = jnp.pad(w, pad)
    return w.reshape(shp[:axis] + (HW,) + shp[axis + 1:])


def _pack_layer(w_in, gla_w2, gla_b, gla_norm_w, gdn_conv_w, gdn_a_log, gdn_dt_bias, gdn_norm_w,
                xa_norm_w, w_out):
    sizes = (GLA_QK_W, GLA_QK_W, HEADS * GLA_DV, HEADS * GLA_DV, 2 * GLA_RANK, 3 * HEADS * GDN_DK,
             HEADS * GDN_DV, 2 * GDN_HEADS, 2 * GDN_HEADS, XA_W, XA_W)
    cols, start = [], 0
    for s in sizes:
        cols.append(w_in[:, start:start + s])
        start += s
    gq, gk, gv, gz, glr, dqkv, dz, db, da, xq, xz = cols
    hd = HEADS * GDN_DK
    misc = jnp.concatenate(
        [db] + [da] * M_A_COPIES
        + [jnp.zeros((D_MODEL, M_LR - M_A - 8 * M_A_COPIES), F32), glr,
           jnp.zeros((D_MODEL, LANES - M_LR - 2 * GLA_RANK), F32)], axis=1)
    w_all = jnp.concatenate(
        [gq, gk, _pad_heads(gv, 1),
         _pad_heads(dqkv[:, 0:hd], 1), _pad_heads(dqkv[:, hd:2 * hd], 1), _pad_heads(dqkv[:, 2 * hd:], 1),
         _pad_heads(gz, 1), _pad_heads(dz, 1), xz, xq, misc], axis=1).astype(BF16)

    cw = jnp.transpose(gdn_conv_w, (1, 0))
    cw = jnp.concatenate([_pad_heads(cw[:, 0:hd], 1), _pad_heads(cw[:, hd:2 * hd], 1),
                          _pad_heads(cw[:, 2 * hd:], 1)], axis=1)
    cw = jnp.pad(cw, ((0, 8 - GDN_CONV), (0, 0)))

    w2bd = jnp.zeros((LANES, 2 * GLA_QK_W), F32)
    w2bd = w2bd.at[M_LR:M_LR + GLA_RANK, 0:GLA_QK_W].set(gla_w2[0])
    w2bd = w2bd.at[M_LR + GLA_RANK:M_LR + 2 * GLA_RANK, GLA_QK_W:].set(gla_w2[1])
    w2bd = w2bd.astype(BF16)
    glab = gla_b.reshape(1, 2 * GLA_QK_W)

    def a_slab(p):
        flat = p.reshape(2 * GDN_HEADS)
        return jnp.concatenate([jnp.zeros((M_A,), F32)] + [flat] * M_A_COPIES
                               + [jnp.zeros((LANES - M_A - 8 * M_A_COPIES,), F32)])

    gparams = jnp.zeros((8, LANES), F32).at[0].set(a_slab(gdn_a_log)).at[1].set(a_slab(gdn_dt_bias))

    wout = jnp.concatenate([_pad_heads(w_out[0:HEADS * GLA_DV], 0),
                            _pad_heads(w_out[HEADS * GLA_DV:HEADS * (GLA_DV + GDN_DV)], 0),
                            w_out[HEADS * (GLA_DV + GDN_DV):]], axis=0).astype(BF16)
    pad_norm = lambda w: jnp.tile(jnp.pad(w, (0, LANES - w.shape[0])), HEADS).reshape(1, HW)
    n1 = pad_norm(gla_norm_w)
    n2 = pad_norm(gdn_norm_w)
    n3 = jnp.tile(xa_norm_w, XA_HEADS).reshape(1, XA_W)
    return w_all, cw, w2bd, glab, gparams, wout, n1, n2, n3


def kernel(x, mem, norm_w, w_in, gla_w2, gla_b, gla_norm_w, gdn_conv_w, gdn_a_log, gdn_dt_bias,
           gdn_norm_w, mem_norm_w, xa_w_kv, xa_norm_w, w_out, final_norm_w):
    assert x.shape == (BATCH, SEQ, D_MODEL) and mem.shape == (BATCH, MEM_LEN, D_MODEL)
    mkv = _memkv(mem, mem_norm_w, xa_w_kv.astype(BF16))
    h = x.reshape(BATCH * SEQ, D_MODEL)
    fnw = final_norm_w.reshape(1, D_MODEL)
    for l in range(DEPTH):
        w_all, cw, w2bd, glab, gparams, wout, n1, n2, n3 = _pack_layer(
            w_in[l], gla_w2[l], gla_b[l], gla_norm_w[l], gdn_conv_w[l], gdn_a_log[l], gdn_dt_bias[l],
            gdn_norm_w[l], xa_norm_w[l], w_out[l])
        gla, v1, dec, zg, q2, k2, v2, gcol, grow, xaq = _inproj(
            h, norm_w[l].reshape(1, D_MODEL), w_all, cw, w2bd, glab, gparams)
        o1f, o1b = _gla_scan(gla, v1, dec)
        o2f, o2b = _gdn_scan(q2, k2, v2, gcol, grow)
        h = _outproj(o1f, o1b, o2f, o2b, zg, xaq, mkv[l], h, wout, n1, n2, n3, fnw, l == DEPTH - 1)
    return h.reshape(BATCH, SEQ, D_MODEL)
```

```python
import functools

import jax
import jax.numpy as jnp
from jax import lax
from jax.experimental import pallas as pl
from jax.experimental.pallas import tpu as pltpu

F32 = jnp.float32
BF16 = jnp.bfloat16

D_MODEL = 1024
BATCH = 8
SEQ = 4096
DEPTH = 2
MEM_LEN = 256
CHUNK = 64
NORM_EPS = 1e-6
GLA_HEADS = 4
GLA_DK = 64
GLA_DV = 96
GLA_RANK = 16
GLA_GATE_NORMALIZER = 16.0
GDN_HEADS = 4
GDN_DK = 96
GDN_DV = 96
GDN_CONV = 5
XA_HEADS = 4
XA_DH = 64

LANES = 128
HEADS = 4
HW = HEADS * LANES
GLA_QK_W = GLA_HEADS * GLA_DK
XA_W = XA_HEADS * XA_DH
MIX_PAD_W = 2 * HW + XA_W

C_GQ = 0
C_GK = C_GQ + GLA_QK_W
C_GV = C_GK + GLA_QK_W
C_DQKV = C_GV + HW
C_Z = C_DQKV + 3 * HW
C_XQ = C_Z + MIX_PAD_W
C_MISC = C_XQ + XA_W
IN_PAD_W = C_MISC + LANES

M_BETA = 0
M_A = 8
M_A_COPIES = 4
M_LR = 64
G_BETA, G_GC, G_EGC, G_EKEND, G_DEC = 0, 8, 16, 24, 32
ROW_SLAB = 16

TM_IN = 256
TM_OUT = 256
NC = 4
TB = NC * CHUNK
HALO = 8
VMEM_LIMIT = 56 * 1024 * 1024


def _dot(a, b):
    return jnp.dot(a, b, preferred_element_type=F32)


def _dot_nt(a, b):
    return lax.dot_general(a, b, (((1,), (1,)), ((), ())), preferred_element_type=F32)


def _dot_tn(a, b):
    return lax.dot_general(a, b, (((0,), (0,)), ((), ())), preferred_element_type=F32)


def _sigmoid(x):
    return 1.0 / (1.0 + jnp.exp(-x))


def _silu(x):
    return x * _sigmoid(x)


def _softplus(x):
    return jnp.maximum(x, 0.0) + jnp.log1p(jnp.exp(-jnp.abs(x)))


def _log_sigmoid(x):
    return jnp.minimum(x, 0.0) - jnp.log1p(jnp.exp(-jnp.abs(x)))


def _split2(x):
    hi = x.astype(BF16)
    lo = (x - hi.astype(F32)).astype(BF16)
    return hi, lo


def _tri_sum(tri, x):
    hi, lo = _split2(x)
    return _dot(tri, hi) + _dot(tri, lo)


def _tri_masks():
    ri = lax.broadcasted_iota(jnp.int32, (CHUNK, CHUNK), 0)
    ci = lax.broadcasted_iota(jnp.int32, (CHUNK, CHUNK), 1)
    return ri, ci


def _memkv_kernel(mem_ref, nw_ref, w_ref, out_ref):
    m = mem_ref[0]
    ms = jnp.mean(m * m, axis=-1, keepdims=True)
    mn = (m * lax.rsqrt(ms + NORM_EPS) * nw_ref[0]).astype(BF16)
    out_ref[0, 0] = _dot(mn, w_ref[0]).astype(BF16)


def _memkv(mem, mem_norm_w, xa_w_kv_bf16):
    return pl.pallas_call(
        _memkv_kernel,
        grid=(DEPTH, BATCH),
        in_specs=[
            pl.BlockSpec((1, MEM_LEN, D_MODEL), lambda l, b: (b, 0, 0)),
            pl.BlockSpec((1, 1, D_MODEL), lambda l, b: (l, 0, 0)),
            pl.BlockSpec((1, D_MODEL, 2 * XA_W), lambda l, b: (l, 0, 0)),
        ],
        out_specs=pl.BlockSpec((1, 1, MEM_LEN, 2 * XA_W), lambda l, b: (l, b, 0, 0)),
        out_shape=jax.ShapeDtypeStruct((DEPTH, BATCH, MEM_LEN, 2 * XA_W), BF16),
        compiler_params=pltpu.CompilerParams(dimension_semantics=("parallel", "parallel")),
        name="memkv",
    )(mem, mem_norm_w.reshape(DEPTH, 1, D_MODEL), xa_w_kv_bf16)


def _inproj_kernel(x_ref, xp_ref, xn_ref, nw_ref, w_ref, cw_ref, w2_ref, gb_ref, gp_ref,
                   gla_ref, v1_ref, dec_ref, zg_ref, q2_ref, k2_ref, v2_ref, gcol_ref, grow_ref,
                   xaq_ref, ext_ref):
    nw = nw_ref[...]

    def norm(ref):
        x = ref[...]
        ms = jnp.mean(x * x, axis=-1, keepdims=True)
        return (x * lax.rsqrt(ms + NORM_EPS) * nw).astype(BF16)

    h = norm(x_ref)
    h_prev = norm(xp_ref)
    h_next = norm(xn_ref)

    def proj(hh, lo, width):
        return _dot(hh, w_ref[:, lo:lo + width])

    blocks_per_seq = SEQ // TM_IN
    j = lax.rem(pl.program_id(0), blocks_per_seq)
    cw = cw_ref[...]
    base = HALO - GDN_CONV // 2

    def gdn_mm(grp):
        lo = C_DQKV + grp * HW
        return proj(h, lo, HW), proj(h_prev, lo, HW), proj(h_next, lo, HW)

    def gdn_vpu(grp, res):
        main, prev, nxt = res
        cols = slice(grp * HW, (grp + 1) * HW)
        ext_ref[0:HALO, cols] = jnp.where(j == 0, 0.0, prev)
        ext_ref[HALO:HALO + TM_IN, cols] = main
        ext_ref[HALO + TM_IN:2 * HALO + TM_IN, cols] = jnp.where(j == blocks_per_seq - 1, 0.0, nxt)
        acc = ext_ref[base:base + TM_IN, cols] * cw[0:1, cols]
        for t in range(1, GDN_CONV):
            acc = acc + ext_ref[base + t:base + t + TM_IN, cols] * cw[t:t + 1, cols]
        y = _silu(acc)
        if grp == 2:
            v2_ref[...] = y.astype(BF16)
            return
        ref, scale = ((q2_ref, GDN_DK ** -0.5), (k2_ref, 1.0))[grp]
        for hd in range(HEADS):
            s = y[:, hd * LANES:(hd + 1) * LANES]
            ss = jnp.sum(s * s, axis=-1, keepdims=True)
            ref[:, hd * LANES:(hd + 1) * LANES] = (s * lax.rsqrt(ss + NORM_EPS) * scale).astype(BF16)

    def z_vpu(lo, width, res):
        zg_ref[:, lo:lo + width] = _silu(res).astype(BF16)

    def xaq_vpu(res):
        xaq_ref[...] = (res * (XA_DH ** -0.5)).astype(BF16)

    lane = lax.broadcasted_iota(jnp.int32, (1, LANES), 1)
    fwd_lane = lax.rem(lane, 8) < GDN_HEADS
    ri, ci = _tri_masks()
    lower = jnp.where(ri >= ci, 1.0, 0.0).astype(BF16)
    upper = jnp.where(ri <= ci, 1.0, 0.0).astype(BF16)
    gate = {}

    def misc_vpu(m):
        logits = _dot(m.astype(BF16), w2_ref[...]) + gb_ref[...]
        gate["g"] = _log_sigmoid(logits) * (1.0 / GLA_GATE_NORMALIZER)
        is_a = (lane >= M_A) & (lane < M_A + 8 * M_A_COPIES)
        neg_a = jnp.where(is_a, -jnp.exp(gp_ref[0:1, :]), 0.0)
        gg = neg_a * _softplus(m + gp_ref[1:2, :])
        beta = _sigmoid(m)
        for c in range(TM_IN // CHUNK):
            rows = slice(c * CHUNK, (c + 1) * CHUNK)
            ggc = gg[rows]
            pf = _tri_sum(lower, ggc)
            sf = _tri_sum(upper, ggc)
            gc = jnp.where(fwd_lane, pf, sf)
            last = jnp.where(fwd_lane, pf[CHUNK - 1:CHUNK], sf[0:1])
            col = jnp.where(lane < G_GC, beta[rows],
                  jnp.where(lane < G_EGC, gc,
                  jnp.where(lane < G_EKEND, jnp.exp(gc),
                  jnp.where(lane < G_DEC, jnp.exp(last - gc), jnp.exp(last)))))
            gcol_ref[rows, :] = col
            grow_ref[c] = col.T[0:ROW_SLAB, :]

    def gla_v_vpu(res):
        v1_ref[...] = res.astype(BF16)

    def gla_qk_vpu(pg):
        q1 = pg[:, 0:GLA_QK_W] * (GLA_DK ** -0.5)
        k1 = pg[:, GLA_QK_W:2 * GLA_QK_W]
        g = gate["g"]
        for c in range(TM_IN // CHUNK):
            rows = slice(c * CHUNK, (c + 1) * CHUNK)
            gch = g[rows]
            q1c = q1[rows]
            k1c = k1[rows]
            bf = _tri_sum(lower, gch[:, 0:GLA_QK_W])
            br = _tri_sum(upper, gch[:, GLA_QK_W:])
            for d, (b, last) in enumerate(((bf, bf[CHUNK - 1:CHUNK]), (br, br[0:1]))):
                off = d * 3 * GLA_QK_W
                gla_ref[rows, off:off + GLA_QK_W] = (q1c * jnp.exp(b)).astype(BF16)
                gla_ref[rows, off + GLA_QK_W:off + 2 * GLA_QK_W] = (k1c * jnp.exp(-b)).astype(BF16)
                gla_ref[rows, off + 2 * GLA_QK_W:off + 3 * GLA_QK_W] = (k1c * jnp.exp(last - b)).astype(BF16)
                dec_ref[c, :, d * GLA_QK_W:(d + 1) * GLA_QK_W] = jnp.exp(last)

    part = functools.partial
    tasks = [
        (part(proj, h, C_MISC, LANES), misc_vpu),
        (part(gdn_mm, 0), part(gdn_vpu, 0)),
        (part(gdn_mm, 1), part(gdn_vpu, 1)),
        (part(gdn_mm, 2), part(gdn_vpu, 2)),
        (part(proj, h, C_GQ, 2 * GLA_QK_W), gla_qk_vpu),
        (part(proj, h, C_Z, HW), part(z_vpu, 0, HW)),
        (part(proj, h, C_Z + HW, HW), part(z_vpu, HW, HW)),
        (part(proj, h, C_Z + 2 * HW, XA_W), part(z_vpu, 2 * HW, XA_W)),
        (part(proj, h, C_GV, HW), gla_v_vpu),
        (part(proj, h, C_XQ, XA_W), xaq_vpu),
    ]
    res = tasks[0][0]()
    for t, (_, vpu) in enumerate(tasks):
        nxt_res = tasks[t + 1][0]() if t + 1 < len(tasks) else None
        vpu(res)
        res = nxt_res


def _inproj(x2d, norm_w, w_all, conv_w, w2bd, gla_b, gdn_params):
    n_tok = x2d.shape[0]
    nblk = n_tok // TM_IN
    halo_blocks = TM_IN // HALO
    n_halo = n_tok // HALO
    nchunk = TM_IN // CHUNK
    tok = lambda w: pl.BlockSpec((TM_IN, w), lambda i: (i, 0))
    const = lambda shape: pl.BlockSpec(shape, lambda i: tuple(0 for _ in shape))
    out_shapes = (
        jax.ShapeDtypeStruct((n_tok, 6 * GLA_QK_W), BF16),
        jax.ShapeDtypeStruct((n_tok, HW), BF16),
        jax.ShapeDtypeStruct((n_tok // CHUNK, 1, 2 * GLA_QK_W), F32),
        jax.ShapeDtypeStruct((n_tok, MIX_PAD_W), BF16),
        jax.ShapeDtypeStruct((n_tok, HW), BF16),
        jax.ShapeDtypeStruct((n_tok, HW), BF16),
        jax.ShapeDtypeStruct((n_tok, HW), BF16),
        jax.ShapeDtypeStruct((n_tok, LANES), F32),
        jax.ShapeDtypeStruct((n_tok // CHUNK, ROW_SLAB, CHUNK), F32),
        jax.ShapeDtypeStruct((n_tok, XA_W), BF16),
    )
    out_specs = (
        tok(6 * GLA_QK_W), tok(HW),
        pl.BlockSpec((nchunk, 1, 2 * GLA_QK_W), lambda i: (i, 0, 0)),
        tok(MIX_PAD_W), tok(HW), tok(HW), tok(HW), tok(LANES),
        pl.BlockSpec((nchunk, ROW_SLAB, CHUNK), lambda i: (i, 0, 0)),
        tok(XA_W),
    )
    return pl.pallas_call(
        _inproj_kernel,
        grid=(nblk,),
        in_specs=[
            tok(D_MODEL),
            pl.BlockSpec((HALO, D_MODEL), lambda i: (jnp.maximum(i * halo_blocks - 1, 0), 0)),
            pl.BlockSpec((HALO, D_MODEL), lambda i: (jnp.minimum((i + 1) * halo_blocks, n_halo - 1), 0)),
            const((1, D_MODEL)),
            const((D_MODEL, IN_PAD_W)),
            const((8, 3 * HW)),
            const((LANES, 2 * GLA_QK_W)),
            const((1, 2 * GLA_QK_W)),
            const((8, LANES)),
        ],
        out_specs=out_specs,
        out_shape=out_shapes,
        scratch_shapes=[pltpu.VMEM((TM_IN + 2 * HALO, 3 * HW), F32)],
        compiler_params=pltpu.CompilerParams(dimension_semantics=("parallel",),
                                             vmem_limit_bytes=VMEM_LIMIT),
        name="inproj",
    )(x2d, x2d, x2d, norm_w, w_all, conv_w, w2bd, gla_b, gdn_params)


def _gla_kernel(gf_ref, gb_ref, vf_ref, vb_ref, decf_ref, decb_ref, of_ref, ob_ref, s_ref):
    @pl.when(pl.program_id(1) == 0)
    def _():
        s_ref[...] = jnp.zeros_like(s_ref)

    ri, ci = _tri_masks()
    incl = (ri >= ci, ri <= ci)
    lane_tok = lax.broadcasted_iota(jnp.int32, (CHUNK, LANES), 1)
    head_lanes = (lane_tok < GLA_DK, lane_tok >= GLA_DK)
    lane_sq = lax.broadcasted_iota(jnp.int32, (LANES, LANES), 1) < GLA_DK
    zero_bf = jnp.zeros((CHUNK, LANES), BF16)

    g_refs, v_refs = (gf_ref, gb_ref), (vf_ref, vb_ref)
    dec_refs, o_refs = (decf_ref, decb_ref), (of_ref, ob_ref)
    pairs = [(d, p) for d in range(2) for p in range(GLA_HEADS // 2)]
    units = [(d, p, hh) for d, p in pairs for hh in range(2)]
    each = lambda fn, *lists: [fn(*args) for args in zip(*lists)]

    for step in range(NC):
        chunk = lambda d: step if d == 0 else NC - 1 - step
        rows = lambda d: slice(chunk(d) * CHUNK, (chunk(d) + 1) * CHUNK)
        st = [s_ref[d, p] for d, p in pairs]
        stb = {dp: s.astype(BF16) for dp, s in zip(pairs, st)}
        qm = [jnp.where(head_lanes[hh], g_refs[d][rows(d), p * LANES:(p + 1) * LANES], zero_bf)
              for d, p, hh in units]
        ke = [g_refs[d][rows(d), GLA_QK_W + p * LANES:GLA_QK_W + (p + 1) * LANES] for d, p, hh in units]
        kend = [g_refs[d][rows(d), 2 * GLA_QK_W + p * LANES:2 * GLA_QK_W + (p + 1) * LANES]
                for d, p, hh in units]
        v = [v_refs[d][rows(d), (2 * p + hh) * LANES:(2 * p + hh + 1) * LANES] for d, p, hh in units]
        a = [jnp.where(incl[d], _dot_nt(q_, k_), 0.0).astype(BF16) for (d, p, hh), q_, k_ in zip(units, qm, ke)]
        inter = [_dot_nt(q_, stb[(d, p)]) for (d, p, hh), q_ in zip(units, qm)]
        o = each(lambda a_, v_, i_: _dot(a_, v_) + i_, a, v, inter)
        kv = each(_dot_tn, v, kend)
        for (d, p, hh), o_ in zip(units, o):
            o_refs[d][rows(d), (2 * p + hh) * LANES:(2 * p + hh + 1) * LANES] = o_.astype(o_refs[d].dtype)
        for i, (d, p) in enumerate(pairs):
            dec = dec_refs[d][chunk(d), :, p * LANES:(p + 1) * LANES]
            s_ref[d, p] = dec * st[i] + jnp.where(lane_sq, kv[2 * i], kv[2 * i + 1])


def _gla_scan(gla, v1, dec):
    nb = SEQ // TB
    fwd = lambda b, i: (b * nb + i, 0)
    bwd = lambda b, i: (b * nb + nb - 1 - i, 0)
    n_tok = gla.shape[0]
    return pl.pallas_call(
        _gla_kernel,
        grid=(BATCH, nb),
        in_specs=[
            pl.BlockSpec((TB, 3 * GLA_QK_W), fwd),
            pl.BlockSpec((TB, 3 * GLA_QK_W), lambda b, i: (b * nb + nb - 1 - i, 1)),
            pl.BlockSpec((TB, HW), fwd),
            pl.BlockSpec((TB, HW), bwd),
            pl.BlockSpec((NC, 1, GLA_QK_W), lambda b, i: (b * nb + i, 0, 0)),
            pl.BlockSpec((NC, 1, GLA_QK_W), lambda b, i: (b * nb + nb - 1 - i, 0, 1)),
        ],
        out_specs=(pl.BlockSpec((TB, HW), fwd), pl.BlockSpec((TB, HW), bwd)),
        out_shape=(jax.ShapeDtypeStruct((n_tok, HW), BF16), jax.ShapeDtypeStruct((n_tok, HW), BF16)),
        scratch_shapes=[pltpu.VMEM((2, GLA_HEADS // 2, LANES, LANES), F32)],
        compiler_params=pltpu.CompilerParams(dimension_semantics=("parallel", "arbitrary"),
                                             vmem_limit_bytes=VMEM_LIMIT),
        name="gla_scan",
    )(gla, gla, v1, v1, dec, dec)


def _gdn_kernel(qf_ref, kf_ref, vf_ref, colf_ref, rowf_ref, qb_ref, kb_ref, vb_ref, colb_ref, rowb_ref,
                of_ref, ob_ref, s_ref):
    @pl.when(pl.program_id(1) == 0)
    def _():
        s_ref[...] = jnp.zeros_like(s_ref)

    ri, ci = _tri_masks()
    incl = (ri >= ci, ri <= ci)
    strict = (ri > ci, ri < ci)
    eye = jnp.where(ri == ci, 1.0, 0.0)

    refs = ((qf_ref, kf_ref, vf_ref, colf_ref, rowf_ref, of_ref),
            (qb_ref, kb_ref, vb_ref, colb_ref, rowb_ref, ob_ref))
    units = [(d, hd) for d in range(2) for hd in range(GDN_HEADS)]
    each = lambda fn, *lists: [fn(*args) for args in zip(*lists)]
    top = lambda m: m[0:CHUNK]
    bot = lambda m: m[CHUNK:2 * CHUNK]

    for step in range(NC):
        q, k, v, gates, dm, sm = [], [], [], [], [], []
        for d, hd in units:
            c = step if d == 0 else NC - 1 - step
            q_ref, k_ref, v_ref, col_ref, row_ref, _ = refs[d]
            rows = slice(c * CHUNK, (c + 1) * CHUNK)
            lanes = slice(hd * LANES, (hd + 1) * LANES)
            idx = d * GDN_HEADS + hd
            q.append(q_ref[rows, lanes])
            k.append(k_ref[rows, lanes])
            v.append(v_ref[rows, lanes])
            col = col_ref[rows, :]
            rowt = row_ref[c]
            gates.append(dict(
                beta_c=col[:, G_BETA + idx:G_BETA + idx + 1], gc_c=col[:, G_GC + idx:G_GC + idx + 1],
                egc_c=col[:, G_EGC + idx:G_EGC + idx + 1], ekend_c=col[:, G_EKEND + idx:G_EKEND + idx + 1],
                dec=col[0:1, G_DEC + idx:G_DEC + idx + 1],
                beta_r=rowt[idx:idx + 1, :], gc_r=rowt[8 + idx:8 + idx + 1, :]))
            dm.append(incl[d])
            sm.append(strict[d])

        kq = each(lambda kk, qq: _dot_nt(jnp.concatenate([kk, qq], axis=0), kk), k, q)
        decay = each(lambda g, m: jnp.where(m, jnp.exp(jnp.where(m, g["gc_c"] - g["gc_r"], 0.0)), 0.0),
                     gates, dm)
        n = each(lambda a, dc, g, m: jnp.where(m, -(top(a) * dc * g["beta_c"]), 0.0), kq, decay, gates, sm)
        pm = each(lambda a: eye + a, n)
        cur = each(lambda a: _dot(a.astype(BF16), a.astype(BF16)), n)
        for _ in range(4):
            st = each(lambda p, cc: _dot(jnp.concatenate([p.astype(BF16), cc.astype(BF16)], axis=0),
                                         cc.astype(BF16)), pm, cur)
            pm = each(lambda p, s_: p + top(s_), pm, st)
            cur = each(bot, st)
        tinv = each(lambda p, cc: p + _dot(p.astype(BF16), cc.astype(BF16)), pm, cur)

        rhs = each(lambda vv, kk, g: jnp.concatenate(
            [(vv.astype(F32) * g["beta_c"]).astype(BF16),
             (kk.astype(F32) * (g["beta_c"] * g["egc_c"])).astype(BF16)], axis=1), v, k, gates)
        uw = each(lambda t, x: _dot(t.astype(BF16), x), tinv, rhs)
        s = [s_ref[d, hd] for d, hd in units]
        r = each(lambda x, qq, ss: _dot(jnp.concatenate([x[:, LANES:].astype(BF16), qq], axis=0),
                                        ss.astype(BF16)), uw, q, s)
        vn = each(lambda x, rr: x[:, 0:LANES] - top(rr), uw, r)
        qk = each(lambda a, dc, m: jnp.where(m, bot(a) * dc, 0.0).astype(BF16), kq, decay, dm)
        o = each(lambda g, rr, a, x: g["egc_c"] * bot(rr) + _dot(a, x.astype(BF16)), gates, r, qk, vn)
        s_new = each(lambda g, ss, kk, x: g["dec"] * ss + _dot_tn(kk, (g["ekend_c"] * x).astype(BF16)),
                     gates, s, k, vn)
        for (d, hd), oo, ss in zip(units, o, s_new):
            c = step if d == 0 else NC - 1 - step
            refs[d][5][c * CHUNK:(c + 1) * CHUNK, hd * LANES:(hd + 1) * LANES] = oo.astype(refs[d][5].dtype)
            s_ref[d, hd] = ss


def _gdn_scan(q2, k2, v2, gcol, grow):
    nb = SEQ // TB
    n_tok = q2.shape[0]
    fwd = lambda b, i: (b * nb + i, 0)
    bwd = lambda b, i: (b * nb + nb - 1 - i, 0)
    fwd3 = lambda b, i: (b * nb + i, 0, 0)
    bwd3 = lambda b, i: (b * nb + nb - 1 - i, 0, 0)

    def specs(m2, m3):
        return [pl.BlockSpec((TB, HW), m2), pl.BlockSpec((TB, HW), m2), pl.BlockSpec((TB, HW), m2),
                pl.BlockSpec((TB, LANES), m2), pl.BlockSpec((NC, ROW_SLAB, CHUNK), m3)]

    return pl.pallas_call(
        _gdn_kernel,
        grid=(BATCH, nb),
        in_specs=specs(fwd, fwd3) + specs(bwd, bwd3),
        out_specs=(pl.BlockSpec((TB, HW), fwd), pl.BlockSpec((TB, HW), bwd)),
        out_shape=(jax.ShapeDtypeStruct((n_tok, HW), BF16), jax.ShapeDtypeStruct((n_tok, HW), BF16)),
        scratch_shapes=[pltpu.VMEM((2, GDN_HEADS, LANES, LANES), F32)],
        compiler_params=pltpu.CompilerParams(dimension_semantics=("parallel", "arbitrary"),
                                             vmem_limit_bytes=VMEM_LIMIT),
        name="gdn_scan",
    )(q2, k2, v2, gcol, grow, q2, k2, v2, gcol, grow)


def _outproj_kernel(o1f_ref, o1b_ref, o2f_ref, o2b_ref, zg_ref, xaq_ref, mkv_ref, x_ref, wout_ref,
                    n1_ref, n2_ref, n3_ref, fn_ref, out_ref, *, final):
    def head_norm(o, nw_ref, width):
        parts = []
        for hd in range(HEADS):
            s = o[:, hd * LANES:(hd + 1) * LANES]
            ms = jnp.sum(s * s, axis=-1, keepdims=True) * (1.0 / width)
            parts.append(s * lax.rsqrt(ms + NORM_EPS))
        return jnp.concatenate(parts, axis=-1) * nw_ref[...]

    zg = zg_ref[...].astype(F32)
    both = lambda f_ref, b_ref: f_ref[...].astype(F32) + b_ref[...].astype(F32)
    o1 = head_norm(both(o1f_ref, o1b_ref), n1_ref, GLA_DV) * zg[:, 0:HW]
    o2 = head_norm(both(o2f_ref, o2b_ref), n2_ref, GDN_DV) * zg[:, HW:2 * HW]
    y = _dot(o1.astype(BF16), wout_ref[0:HW, :]) + _dot(o2.astype(BF16), wout_ref[HW:2 * HW, :])

    lane_q = lax.broadcasted_iota(jnp.int32, (TM_OUT, LANES), 1)
    lane_m = lax.broadcasted_iota(jnp.int32, (MEM_LEN, LANES), 1)
    for p in range(XA_HEADS // 2):
        lanes = slice(p * LANES, (p + 1) * LANES)
        qpair = xaq_ref[:, lanes]
        mk = mkv_ref[0, :, p * LANES:(p + 1) * LANES]
        mv = mkv_ref[0, :, XA_W + p * LANES:XA_W + (p + 1) * LANES]
        acc = jnp.zeros((TM_OUT, LANES), F32)
        for hh in range(2):
            in_head_q = (lane_q >= hh * XA_DH) & (lane_q < (hh + 1) * XA_DH)
            in_head_m = (lane_m >= hh * XA_DH) & (lane_m < (hh + 1) * XA_DH)
            qm = jnp.where(in_head_q, qpair, jnp.zeros_like(qpair))
            sc = _dot_nt(qm, mk)
            e = jnp.exp(sc - jnp.max(sc, axis=-1, keepdims=True))
            l = jnp.sum(e, axis=-1, keepdims=True)
            mvm = jnp.where(in_head_m, mv, jnp.zeros_like(mv))
            acc = acc + _dot(e.astype(BF16), mvm) * (1.0 / l)
        sq = acc * acc
        first = lane_q < XA_DH
        ss0 = jnp.sum(jnp.where(first, sq, 0.0), axis=-1, keepdims=True)
        ss1 = jnp.sum(jnp.where(first, 0.0, sq), axis=-1, keepdims=True)
        ms = jnp.where(first, ss0, ss1) * (1.0 / XA_DH)
        o3 = acc * lax.rsqrt(ms + NORM_EPS) * n3_ref[:, lanes] * zg[:, 2 * HW + p * LANES:2 * HW + (p + 1) * LANES]
        y = y + _dot(o3.astype(BF16), wout_ref[2 * HW + p * LANES:2 * HW + (p + 1) * LANES, :])

    xo = x_ref[...] + y
    if final:
        ms = jnp.mean(xo * xo, axis=-1, keepdims=True)
        xo = xo * lax.rsqrt(ms + NORM_EPS) * fn_ref[...]
    out_ref[...] = xo


def _outproj(o1f, o1b, o2f, o2b, zg, xaq, mkv, x2d, wout, n1, n2, n3, fnw, final):
    n_tok = x2d.shape[0]
    blocks_per_seq = SEQ // TM_OUT
    tok = lambda w: pl.BlockSpec((TM_OUT, w), lambda i: (i, 0))
    const = lambda shape: pl.BlockSpec(shape, lambda i: tuple(0 for _ in shape))
    return pl.pallas_call(
        functools.partial(_outproj_kernel, final=final),
        grid=(n_tok // TM_OUT,),
        in_specs=[
            tok(HW), tok(HW), tok(HW), tok(HW), tok(MIX_PAD_W), tok(XA_W),
            pl.BlockSpec((1, MEM_LEN, 2 * XA_W), lambda i: (i // blocks_per_seq, 0, 0)),
            tok(D_MODEL),
            const((MIX_PAD_W, D_MODEL)),
            const((1, HW)), const((1, HW)), const((1, XA_W)), const((1, D_MODEL)),
        ],
        out_specs=tok(D_MODEL),
        out_shape=jax.ShapeDtypeStruct((n_tok, D_MODEL), F32),
        compiler_params=pltpu.CompilerParams(dimension_semantics=("parallel",),
                                             vmem_limit_bytes=VMEM_LIMIT),
        name="outproj",
    )(o1f, o1b, o2f, o2b, zg, xaq, mkv, x2d, wout, n1, n2, n3, fnw)


def _pad_heads(w, axis):
    shp = w.shape
    w = w.reshape(shp[:axis] + (HEADS, GLA_DV) + shp[axis + 1:])
    pad = [(0, 0)] * w.ndim
    pad[axis + 1] = (0, LANES - GLA_DV)
    w = jnp.pad(w, pad)
    return w.reshape(shp[:axis] + (HW,) + shp[axis + 1:])


def _pack_layer(w_in, gla_w2, gla_b, gla_norm_w, gdn_conv_w, gdn_a_log, gdn_dt_bias, gdn_norm_w,
                xa_norm_w, w_out):
    sizes = (GLA_QK_W, GLA_QK_W, HEADS * GLA_DV, HEADS * GLA_DV, 2 * GLA_RANK, 3 * HEADS * GDN_DK,
             HEADS * GDN_DV, 2 * GDN_HEADS, 2 * GDN_HEADS, XA_W, XA_W)
    cols, start = [], 0
    for s in sizes:
        cols.append(w_in[:, start:start + s])
        start += s
    gq, gk, gv, gz, glr, dqkv, dz, db, da, xq, xz = cols
    hd = HEADS * GDN_DK
    misc = jnp.concatenate(
        [db] + [da] * M_A_COPIES
        + [jnp.zeros((D_MODEL, M_LR - M_A - 8 * M_A_COPIES), F32), glr,
           jnp.zeros((D_MODEL, LANES - M_LR - 2 * GLA_RANK), F32)], axis=1)
    w_all = jnp.concatenate(
        [gq, gk, _pad_heads(gv, 1),
         _pad_heads(dqkv[:, 0:hd], 1), _pad_heads(dqkv[:, hd:2 * hd], 1), _pad_heads(dqkv[:, 2 * hd:], 1),
         _pad_heads(gz, 1), _pad_heads(dz, 1), xz, xq, misc], axis=1).astype(BF16)

    cw = jnp.transpose(gdn_conv_w, (1, 0))
    cw = jnp.concatenate([_pad_heads(cw[:, 0:hd], 1), _pad_heads(cw[:, hd:2 * hd], 1),
                          _pad_heads(cw[:, 2 * hd:], 1)], axis=1)
    cw = jnp.pad(cw, ((0, 8 - GDN_CONV), (0, 0)))

    w2bd = jnp.zeros((LANES, 2 * GLA_QK_W), F32)
    w2bd = w2bd.at[M_LR:M_LR + GLA_RANK, 0:GLA_QK_W].set(gla_w2[0])
    w2bd = w2bd.at[M_LR + GLA_RANK:M_LR + 2 * GLA_RANK, GLA_QK_W:].set(gla_w2[1])
    w2bd = w2bd.astype(BF16)
    glab = gla_b.reshape(1, 2 * GLA_QK_W)

    def a_slab(p):
        flat = p.reshape(2 * GDN_HEADS)
        return jnp.concatenate([jnp.zeros((M_A,), F32)] + [flat] * M_A_COPIES
                               + [jnp.zeros((LANES - M_A - 8 * M_A_COPIES,), F32)])

    gparams = jnp.zeros((8, LANES), F32).at[0].set(a_slab(gdn_a_log)).at[1].set(a_slab(gdn_dt_bias))

    wout = jnp.concatenate([_pad_heads(w_out[0:HEADS * GLA_DV], 0),
                            _pad_heads(w_out[HEADS * GLA_DV:HEADS * (GLA_DV + GDN_DV)], 0),
                            w_out[HEADS * (GLA_DV + GDN_DV):]], axis=0).astype(BF16)
    pad_norm = lambda w: jnp.tile(jnp.pad(w, (0, LANES - w.shape[0])), HEADS).reshape(1, HW)
    n1 = pad_norm(gla_norm_w)
    n2 = pad_norm(gdn_norm_w)
    n3 = jnp.tile(xa_norm_w, XA_HEADS).reshape(1, XA_W)
    return w_all, cw, w2bd, glab, gparams, wout, n1, n2, n3


def kernel(x, mem, norm_w, w_in, gla_w2, gla_b, gla_norm_w, gdn_conv_w, gdn_a_log, gdn_dt_bias,
           gdn_norm_w, mem_norm_w, xa_w_kv, xa_norm_w, w_out, final_norm_w):
    assert x.shape == (BATCH, SEQ, D_MODEL) and mem.shape == (BATCH, MEM_LEN, D_MODEL)
    mkv = _memkv(mem, mem_norm_w, xa_w_kv.astype(BF16))
    h = x.reshape(BATCH * SEQ, D_MODEL)
    fnw = final_norm_w.reshape(1, D_MODEL)
    for l in range(DEPTH):
        w_all, cw, w2bd, glab, gparams, wout, n1, n2, n3 = _pack_layer(
            w_in[l], gla_w2[l], gla_b[l], gla_norm_w[l], gdn_conv_w[l], gdn_a_log[l], gdn_dt_bias[l],
            gdn_norm_w[l], xa_norm_w[l], w_out[l])
        gla, v1, dec, zg, q2, k2, v2, gcol, grow, xaq = _inproj(
            h, norm_w[l].reshape(1, D_MODEL), w_all, cw, w2bd, glab, gparams)
        o1f, o1b = _gla_scan(gla, v1, dec)
        o2f, o2b = _gdn_scan(q2, k2, v2, gcol, grow)
        h = _outproj(o1f, o1b, o2f, o2b, zg, xaq, mkv[l], h, wout, n1, n2, n3, fnw, l == DEPTH - 1)
    return h.reshape(BATCH, SEQ, D_MODEL)
```

```python
import functools

import jax
import jax.numpy as jnp
from jax import lax
from jax.experimental import pallas as pl
from jax.experimental.pallas import tpu as pltpu

F32 = jnp.float32
BF16 = jnp.bfloat16

D_MODEL = 1024
BATCH = 8
SEQ = 4096
DEPTH = 2
MEM_LEN = 256
CHUNK = 64
NORM_EPS = 1e-6
GLA_HEADS = 4
GLA_DK = 64
GLA_DV = 96
GLA_RANK = 16
GLA_GATE_NORMALIZER = 16.0
GDN_HEADS = 4
GDN_DK = 96
GDN_DV = 96
GDN_CONV = 5
XA_HEADS = 4
XA_DH = 64

LANES = 128
HEADS = 4
HW = HEADS * LANES
GLA_QK_W = GLA_HEADS * GLA_DK
XA_W = XA_HEADS * XA_DH
MIX_PAD_W = 2 * HW + XA_W

C_GQ = 0
C_GK = C_GQ + GLA_QK_W
C_GV = C_GK + GLA_QK_W
C_DQKV = C_GV + HW
C_Z = C_DQKV + 3 * HW
C_XQ = C_Z + MIX_PAD_W
C_MISC = C_XQ + XA_W
IN_PAD_W = C_MISC + LANES

M_BETA = 0
M_A = 8
M_A_COPIES = 4
M_LR = 64
G_BETA, G_GC, G_EGC, G_EKEND, G_DEC = 0, 8, 16, 24, 32
ROW_SLAB = 16

TM_IN = 512
TM_OUT = 512
NC = 4
TB = NC * CHUNK
HALO = 8
VMEM_LIMIT = 56 * 1024 * 1024


def _dot(a, b):
    return jnp.dot(a, b, preferred_element_type=F32)


def _dot_nt(a, b):
    return lax.dot_general(a, b, (((1,), (1,)), ((), ())), preferred_element_type=F32)


def _dot_tn(a, b):
    return lax.dot_general(a, b, (((0,), (0,)), ((), ())), preferred_element_type=F32)


def _sigmoid(x):
    return 1.0 / (1.0 + jnp.exp(-x))


def _silu(x):
    return x * _sigmoid(x)


def _softplus(x):
    return jnp.maximum(x, 0.0) + jnp.log(1.0 + jnp.exp(-jnp.abs(x)))


def _log_sigmoid(x):
    return jnp.minimum(x, 0.0) - jnp.log(1.0 + jnp.exp(-jnp.abs(x)))


def _split2(x):
    hi = pltpu.bitcast(pltpu.bitcast(x, jnp.int32) & jnp.int32(-65536), F32)
    return hi.astype(BF16), (x - hi).astype(BF16)


def _tri_sum(tri, x):
    hi, lo = _split2(x)
    return _dot(tri, hi) + _dot(tri, lo)


def _tri_masks():
    ri = lax.broadcasted_iota(jnp.int32, (CHUNK, CHUNK), 0)
    ci = lax.broadcasted_iota(jnp.int32, (CHUNK, CHUNK), 1)
    return ri, ci


def _memkv_kernel(mem_ref, nw_ref, w_ref, out_ref):
    m = mem_ref[0]
    ms = jnp.mean(m * m, axis=-1, keepdims=True)
    mn = (m * lax.rsqrt(ms + NORM_EPS) * nw_ref[0]).astype(BF16)
    out_ref[0, 0] = _dot(mn, w_ref[0]).astype(BF16)


def _memkv(mem, mem_norm_w, xa_w_kv_bf16):
    return pl.pallas_call(
        _memkv_kernel,
        grid=(DEPTH, BATCH),
        in_specs=[
            pl.BlockSpec((1, MEM_LEN, D_MODEL), lambda l, b: (b, 0, 0)),
            pl.BlockSpec((1, 1, D_MODEL), lambda l, b: (l, 0, 0)),
            pl.BlockSpec((1, D_MODEL, 2 * XA_W), lambda l, b: (l, 0, 0)),
        ],
        out_specs=pl.BlockSpec((1, 1, MEM_LEN, 2 * XA_W), lambda l, b: (l, b, 0, 0)),
        out_shape=jax.ShapeDtypeStruct((DEPTH, BATCH, MEM_LEN, 2 * XA_W), BF16),
        compiler_params=pltpu.CompilerParams(dimension_semantics=("parallel", "parallel")),
        name="memkv",
    )(mem, mem_norm_w.reshape(DEPTH, 1, D_MODEL), xa_w_kv_bf16)


def _inproj_kernel(x_ref, xp_ref, xn_ref, nw_ref, w_ref, cw_ref, w2_ref, gb_ref, gp_ref,
                   gla_ref, v1_ref, dec_ref, zg_ref, q2_ref, k2_ref, v2_ref, gcol_ref, grow_ref,
                   xaq_ref, ext_ref, h_ref):
    nw = nw_ref[...]

    def norm(ref):
        x = ref[...]
        ms = jnp.mean(x * x, axis=-1, keepdims=True)
        return (x * lax.rsqrt(ms + NORM_EPS) * nw).astype(BF16)

    h_ref[...] = norm(x_ref)
    h = None
    h_prev = norm(xp_ref)
    h_next = norm(xn_ref)

    def proj(hh, lo, width):
        return _dot(h_ref[...] if hh is None else hh, w_ref[:, lo:lo + width])

    blocks_per_seq = SEQ // TM_IN
    j = lax.rem(pl.program_id(0), blocks_per_seq)
    cw = cw_ref[...]
    base = HALO - GDN_CONV // 2

    def gdn_mm(grp):
        lo = C_DQKV + grp * HW
        return proj(h, lo, HW), proj(h_prev, lo, HW), proj(h_next, lo, HW)

    def gdn_vpu(grp, res):
        main, prev, nxt = res
        cols = slice(grp * HW, (grp + 1) * HW)
        ext_ref[0:HALO, cols] = jnp.where(j == 0, 0.0, prev)
        ext_ref[HALO:HALO + TM_IN, cols] = main
        ext_ref[HALO + TM_IN:2 * HALO + TM_IN, cols] = jnp.where(j == blocks_per_seq - 1, 0.0, nxt)
        acc = ext_ref[base:base + TM_IN, cols] * cw[0:1, cols]
        for t in range(1, GDN_CONV):
            acc = acc + ext_ref[base + t:base + t + TM_IN, cols] * cw[t:t + 1, cols]
        y = _silu(acc)
        if grp == 2:
            v2_ref[...] = y.astype(BF16)
            return
        ref, scale = ((q2_ref, GDN_DK ** -0.5), (k2_ref, 1.0))[grp]
        for hd in range(HEADS):
            s = y[:, hd * LANES:(hd + 1) * LANES]
            ss = jnp.sum(s * s, axis=-1, keepdims=True)
            ref[:, hd * LANES:(hd + 1) * LANES] = (s * lax.rsqrt(ss + NORM_EPS) * scale).astype(BF16)

    def z_vpu(lo, width, res):
        zg_ref[:, lo:lo + width] = _silu(res).astype(BF16)

    def xaq_vpu(res):
        xaq_ref[...] = (res * (XA_DH ** -0.5)).astype(BF16)

    lane = lax.broadcasted_iota(jnp.int32, (1, LANES), 1)
    fwd_lane = lax.rem(lane, 8) < GDN_HEADS
    ri, ci = _tri_masks()
    lower = jnp.where(ri >= ci, 1.0, 0.0).astype(BF16)
    upper = jnp.where(ri <= ci, 1.0, 0.0).astype(BF16)
    gate = {}

    def misc_vpu(m):
        logits = _dot(m.astype(BF16), w2_ref[...]) + gb_ref[...]
        gate["g"] = _log_sigmoid(logits) * (1.0 / GLA_GATE_NORMALIZER)
        is_a = (lane >= M_A) & (lane < M_A + 8 * M_A_COPIES)
        neg_a = jnp.where(is_a, -jnp.exp(gp_ref[0:1, :]), 0.0)
        gg = neg_a * _softplus(m + gp_ref[1:2, :])
        beta = _sigmoid(m)
        for c in range(TM_IN // CHUNK):
            rows = slice(c * CHUNK, (c + 1) * CHUNK)
            ggc = gg[rows]
            pf = _tri_sum(lower, ggc)
            sf = _tri_sum(upper, ggc)
            gc = jnp.where(fwd_lane, pf, sf)
            last = jnp.where(fwd_lane, pf[CHUNK - 1:CHUNK], sf[0:1])
            col = jnp.where(lane < G_GC, beta[rows],
                  jnp.where(lane < G_EGC, gc,
                  jnp.where(lane < G_EKEND, jnp.exp(gc),
                  jnp.where(lane < G_DEC, jnp.exp(last - gc), jnp.exp(last)))))
            gcol_ref[rows, :] = col
            grow_ref[c] = col.T[0:ROW_SLAB, :]

    def gla_v_vpu(res):
        v1_ref[...] = res.astype(BF16)

    def gla_qk_vpu(pg):
        q1 = pg[:, 0:GLA_QK_W] * (GLA_DK ** -0.5)
        k1 = pg[:, GLA_QK_W:2 * GLA_QK_W]
        g = gate["g"]
        for c in range(TM_IN // CHUNK):
            rows = slice(c * CHUNK, (c + 1) * CHUNK)
            gch = g[rows]
            q1c = q1[rows]
            k1c = k1[rows]
            bf = _tri_sum(lower, gch[:, 0:GLA_QK_W])
            br = _tri_sum(upper, gch[:, GLA_QK_W:])
            for d, (b, last) in enumerate(((bf, bf[CHUNK - 1:CHUNK]), (br, br[0:1]))):
                off = d * 3 * GLA_QK_W
                gla_ref[rows, off:off + GLA_QK_W] = (q1c * jnp.exp(b)).astype(BF16)
                gla_ref[rows, off + GLA_QK_W:off + 2 * GLA_QK_W] = (k1c * jnp.exp(-b)).astype(BF16)
                gla_ref[rows, off + 2 * GLA_QK_W:off + 3 * GLA_QK_W] = (k1c * jnp.exp(last - b)).astype(BF16)
                dec_ref[c, :, d * GLA_QK_W:(d + 1) * GLA_QK_W] = jnp.exp(last)

    part = functools.partial
    tasks = [
        (part(proj, h, C_MISC, LANES), misc_vpu),
        (part(gdn_mm, 0), part(gdn_vpu, 0)),
        (part(proj, h, C_Z, HW), part(z_vpu, 0, HW)),
        (part(gdn_mm, 1), part(gdn_vpu, 1)),
        (part(proj, h, C_Z + HW, HW), part(z_vpu, HW, HW)),
        (part(gdn_mm, 2), part(gdn_vpu, 2)),
        (part(proj, h, C_Z + 2 * HW, XA_W), part(z_vpu, 2 * HW, XA_W)),
        (part(proj, h, C_GQ, 2 * GLA_QK_W), gla_qk_vpu),
        (part(proj, h, C_GV, HW), gla_v_vpu),
        (part(proj, h, C_XQ, XA_W), xaq_vpu),
    ]
    res = tasks[0][0]()
    for t, (_, vpu) in enumerate(tasks):
        nxt_res = tasks[t + 1][0]() if t + 1 < len(tasks) else None
        vpu(res)
        res = nxt_res


def _inproj(x2d, norm_w, w_all, conv_w, w2bd, gla_b, gdn_params):
    n_tok = x2d.shape[0]
    nblk = n_tok // TM_IN
    halo_blocks = TM_IN // HALO
    n_halo = n_tok // HALO
    nchunk = TM_IN // CHUNK
    tok = lambda w: pl.BlockSpec((TM_IN, w), lambda i: (i, 0))
    const = lambda shape: pl.BlockSpec(shape, lambda i: tuple(0 for _ in shape))
    out_shapes = (
        jax.ShapeDtypeStruct((n_tok, 6 * GLA_QK_W), BF16),
        jax.ShapeDtypeStruct((n_tok, HW), BF16),
        jax.ShapeDtypeStruct((n_tok // CHUNK, 1, 2 * GLA_QK_W), F32),
        jax.ShapeDtypeStruct((n_tok, MIX_PAD_W), BF16),
        jax.ShapeDtypeStruct((n_tok, HW), BF16),
        jax.ShapeDtypeStruct((n_tok, HW), BF16),
        jax.ShapeDtypeStruct((n_tok, HW), BF16),
        jax.ShapeDtypeStruct((n_tok, LANES), F32),
        jax.ShapeDtypeStruct((n_tok // CHUNK, ROW_SLAB, CHUNK), F32),
        jax.ShapeDtypeStruct((n_tok, XA_W), BF16),
    )
    out_specs = (
        tok(6 * GLA_QK_W), tok(HW),
        pl.BlockSpec((nchunk, 1, 2 * GLA_QK_W), lambda i: (i, 0, 0)),
        tok(MIX_PAD_W), tok(HW), tok(HW), tok(HW), tok(LANES),
        pl.BlockSpec((nchunk, ROW_SLAB, CHUNK), lambda i: (i, 0, 0)),
        tok(XA_W),
    )
    return pl.pallas_call(
        _inproj_kernel,
        grid=(nblk,),
        in_specs=[
            tok(D_MODEL),
            pl.BlockSpec((HALO, D_MODEL), lambda i: (jnp.maximum(i * halo_blocks - 1, 0), 0)),
            pl.BlockSpec((HALO, D_MODEL), lambda i: (jnp.minimum((i + 1) * halo_blocks, n_halo - 1), 0)),
            const((1, D_MODEL)),
            const((D_MODEL, IN_PAD_W)),
            const((8, 3 * HW)),
            const((LANES, 2 * GLA_QK_W)),
            const((1, 2 * GLA_QK_W)),
            const((8, LANES)),
        ],
        out_specs=out_specs,
        out_shape=out_shapes,
        scratch_shapes=[pltpu.VMEM((TM_IN + 2 * HALO, 3 * HW), F32), pltpu.VMEM((TM_IN, D_MODEL), BF16)],
        compiler_params=pltpu.CompilerParams(dimension_semantics=("parallel",),
                                             vmem_limit_bytes=VMEM_LIMIT),
        name="inproj",
    )(x2d, x2d, x2d, norm_w, w_all, conv_w, w2bd, gla_b, gdn_params)


def _gla_kernel(gf_ref, gb_ref, vf_ref, vb_ref, decf_ref, decb_ref, of_ref, ob_ref, s_ref):
    @pl.when(pl.program_id(1) == 0)
    def _():
        s_ref[...] = jnp.zeros_like(s_ref)

    ri, ci = _tri_masks()
    incl = (ri >= ci, ri <= ci)
    lane_tok = lax.broadcasted_iota(jnp.int32, (CHUNK, LANES), 1)
    head_lanes = (lane_tok < GLA_DK, lane_tok >= GLA_DK)
    lane_sq = lax.broadcasted_iota(jnp.int32, (LANES, LANES), 1) < GLA_DK
    zero_bf = jnp.zeros((CHUNK, LANES), BF16)

    g_refs, v_refs = (gf_ref, gb_ref), (vf_ref, vb_ref)
    dec_refs, o_refs = (decf_ref, decb_ref), (of_ref, ob_ref)
    pairs = [(d, p) for d in range(2) for p in range(GLA_HEADS // 2)]
    units = [(d, p, hh) for d, p in pairs for hh in range(2)]
    each = lambda fn, *lists: [fn(*args) for args in zip(*lists)]

    for step in range(NC):
        chunk = lambda d: step if d == 0 else NC - 1 - step
        rows = lambda d: slice(chunk(d) * CHUNK, (chunk(d) + 1) * CHUNK)
        st = [s_ref[d, p] for d, p in pairs]
        stb = {dp: s.astype(BF16) for dp, s in zip(pairs, st)}
        qm = [jnp.where(head_lanes[hh], g_refs[d][rows(d), p * LANES:(p + 1) * LANES], zero_bf)
              for d, p, hh in units]
        ke = [g_refs[d][rows(d), GLA_QK_W + p * LANES:GLA_QK_W + (p + 1) * LANES] for d, p, hh in units]
        kend = [g_refs[d][rows(d), 2 * GLA_QK_W + p * LANES:2 * GLA_QK_W + (p + 1) * LANES]
                for d, p, hh in units]
        v = [v_refs[d][rows(d), (2 * p + hh) * LANES:(2 * p + hh + 1) * LANES] for d, p, hh in units]
        a = [jnp.where(incl[d], _dot_nt(q_, k_), 0.0).astype(BF16) for (d, p, hh), q_, k_ in zip(units, qm, ke)]
        inter = [_dot_nt(q_, stb[(d, p)]) for (d, p, hh), q_ in zip(units, qm)]
        o = each(lambda a_, v_, i_: _dot(a_, v_) + i_, a, v, inter)
        kv = each(_dot_tn, v, kend)
        for (d, p, hh), o_ in zip(units, o):
            o_refs[d][rows(d), (2 * p + hh) * LANES:(2 * p + hh + 1) * LANES] = o_.astype(o_refs[d].dtype)
        for i, (d, p) in enumerate(pairs):
            dec = dec_refs[d][chunk(d), :, p * LANES:(p + 1) * LANES]
            s_ref[d, p] = dec * st[i] + jnp.where(lane_sq, kv[2 * i], kv[2 * i + 1])


def _gla_scan(gla, v1, dec):
    nb = SEQ // TB
    fwd = lambda b, i: (b * nb + i, 0)
    bwd = lambda b, i: (b * nb + nb - 1 - i, 0)
    n_tok = gla.shape[0]
    return pl.pallas_call(
        _gla_kernel,
        grid=(BATCH, nb),
        in_specs=[
            pl.BlockSpec((TB, 3 * GLA_QK_W), fwd),
            pl.BlockSpec((TB, 3 * GLA_QK_W), lambda b, i: (b * nb + nb - 1 - i, 1)),
            pl.BlockSpec((TB, HW), fwd),
            pl.BlockSpec((TB, HW), bwd),
            pl.BlockSpec((NC, 1, GLA_QK_W), lambda b, i: (b * nb + i, 0, 0)),
            pl.BlockSpec((NC, 1, GLA_QK_W), lambda b, i: (b * nb + nb - 1 - i, 0, 1)),
        ],
        out_specs=(pl.BlockSpec((TB, HW), fwd), pl.BlockSpec((TB, HW), bwd)),
        out_shape=(jax.ShapeDtypeStruct((n_tok, HW), BF16), jax.ShapeDtypeStruct((n_tok, HW), BF16)),
        scratch_shapes=[pltpu.VMEM((2, GLA_HEADS // 2, LANES, LANES), F32)],
        compiler_params=pltpu.CompilerParams(dimension_semantics=("parallel", "arbitrary"),
                                             vmem_limit_bytes=VMEM_LIMIT),
        name="gla_scan",
    )(gla, gla, v1, v1, dec, dec)


def _gdn_kernel(qf_ref, kf_ref, vf_ref, colf_ref, rowf_ref, qb_ref, kb_ref, vb_ref, colb_ref, rowb_ref,
                of_ref, ob_ref, s_ref):
    @pl.when(pl.program_id(1) == 0)
    def _():
        s_ref[...] = jnp.zeros_like(s_ref)

    pk = GDN_HEADS * CHUNK
    ri = lax.broadcasted_iota(jnp.int32, (CHUNK, pk), 0)
    ci = lax.broadcasted_iota(jnp.int32, (CHUNK, pk), 1) & (CHUNK - 1)
    incl = (ri >= ci, ri <= ci)
    strict = (ri > ci, ri < ci)
    eye = jnp.where(ri == ci, 1.0, 0.0)
    blk = lambda axis: lax.shift_right_logical(lax.broadcasted_iota(jnp.int32, (pk, pk), axis), 6)
    same_blk = blk(0) == blk(1)
    bd_mask = jnp.where(same_blk, 1.0, 0.0).astype(BF16)
    left = lax.broadcasted_iota(jnp.int32, (CHUNK, LANES), 1) < CHUNK
    zero_tok = jnp.zeros((CHUNK, LANES), BF16)
    zero_sq = jnp.zeros((LANES, LANES), BF16)

    refs = ((qf_ref, kf_ref, vf_ref, colf_ref, rowf_ref, of_ref),
            (qb_ref, kb_ref, vb_ref, colb_ref, rowb_ref, ob_ref))
    groups = [(d, step) for step in range(NC) for d in range(2)]
    pairs = range(GDN_HEADS // 2)
    each = lambda fn, *lists: [fn(*args) for args in zip(*lists)]
    top = lambda m: m[0:CHUNK]
    bot = lambda m: m[CHUNK:2 * CHUNK]
    cat = jnp.concatenate
    chunk_of = lambda d, step: step if d == 0 else NC - 1 - step

    def block_diag(x):
        return cat([x, x, x, x], axis=0) * bd_mask

    def diag2(a, b, zero):
        return cat([cat([a, zero], axis=1), cat([zero, b], axis=1)], axis=0)

    q, k, v, col, gc_r = [], [], [], [], []
    for d, step in groups:
        q_ref, k_ref, v_ref, col_ref, row_ref, _ = refs[d]
        c = chunk_of(d, step)
        rows = slice(c * CHUNK, (c + 1) * CHUNK)
        heads = [slice(hd * LANES, (hd + 1) * LANES) for hd in range(GDN_HEADS)]
        q.append([q_ref[rows, h] for h in heads])
        k.append([k_ref[rows, h] for h in heads])
        v.append([v_ref[rows, h] for h in heads])
        col.append(col_ref[rows, :])
        rowt = row_ref[c]
        gc_r.append(cat([rowt[8 + d * GDN_HEADS + hd:9 + d * GDN_HEADS + hd, :] for hd in range(GDN_HEADS)],
                        axis=1))

    def colv(g, base, hd):
        lane = base + groups[g][0] * GDN_HEADS + hd
        return col[g][:, lane:lane + 1]

    def col_packed(g, base):
        tiles = [jnp.where(left, jnp.broadcast_to(colv(g, base, 2 * j), (CHUNK, LANES)),
                           jnp.broadcast_to(colv(g, base, 2 * j + 1), (CHUNK, LANES))) for j in pairs]
        return cat(tiles, axis=1)

    ng = len(groups)
    kq = [[_dot_nt(cat([cat([k[g][2 * j], k[g][2 * j + 1]], axis=1),
                        cat([q[g][2 * j], q[g][2 * j + 1]], axis=1)], axis=0),
                   diag2(k[g][2 * j], k[g][2 * j + 1], zero_tok)) for j in pairs] for g in range(ng)]
    kk = [cat([top(kq[g][j]) for j in pairs], axis=1) for g in range(ng)]
    qkr = [cat([bot(kq[g][j]) for j in pairs], axis=1) for g in range(ng)]
    dirs = [d for d, _ in groups]
    decay = [jnp.where(incl[d], jnp.exp(jnp.where(incl[d], col_packed(g, G_GC) - gc_r[g], 0.0)), 0.0)
             for g, d in enumerate(dirs)]
    n = [jnp.where(strict[d], -(kk[g] * decay[g] * col_packed(g, G_BETA)), 0.0) for g, d in enumerate(dirs)]
    qk = [jnp.where(incl[d], qkr[g] * decay[g], 0.0).astype(BF16) for g, d in enumerate(dirs)]
    pm = each(lambda a: eye + a, n)
    cur = each(lambda a: _dot(a.astype(BF16), block_diag(a.astype(BF16))), n)
    for _ in range(4):
        st = each(lambda p, cc: _dot(cat([p.astype(BF16), cc.astype(BF16)], axis=0),
                                     block_diag(cc.astype(BF16))), pm, cur)
        pm = each(lambda p, s_: p + top(s_), pm, st)
        cur = each(bot, st)
    tinv = each(lambda p, cc: p + _dot(p.astype(BF16), block_diag(cc.astype(BF16))), pm, cur)

    def rhs_of(g, hd):
        beta = colv(g, G_BETA, hd)
        return cat([(v[g][hd].astype(F32) * beta).astype(BF16),
                    (k[g][hd].astype(F32) * (beta * colv(g, G_EGC, hd))).astype(BF16)], axis=1)

    def uw_of(g, j):
        t = tinv[g][:, j * LANES:(j + 1) * LANES]
        lhs = cat([jnp.where(left, t, 0.0), jnp.where(left, 0.0, t)], axis=0).astype(BF16)
        return _dot(lhs, cat([rhs_of(g, 2 * j), rhs_of(g, 2 * j + 1)], axis=0))

    uw = [[uw_of(g, j) for j in pairs] for g in range(ng)]

    for step in range(NC):
        gs = [g for g, (_, st_) in enumerate(groups) if st_ == step]
        units = [(g, j) for g in gs for j in pairs]
        s = {(g, hd): s_ref[groups[g][0], hd] for g in gs for hd in range(GDN_HEADS)}
        r = {}
        for g, j in units:
            a, b = 2 * j, 2 * j + 1
            x = uw[g][j]
            lhs = cat([cat([x[0:CHUNK, LANES:].astype(BF16), x[CHUNK:, LANES:].astype(BF16)], axis=1),
                       cat([q[g][a], q[g][b]], axis=1)], axis=0)
            r[(g, j)] = _dot(lhs, diag2(s[(g, a)].astype(BF16), s[(g, b)].astype(BF16), zero_sq))
        vn = {}
        for g, j in units:
            x, rr = uw[g][j], r[(g, j)]
            vn[(g, 2 * j)] = x[0:CHUNK, 0:LANES] - rr[0:CHUNK, 0:LANES]
            vn[(g, 2 * j + 1)] = x[CHUNK:, 0:LANES] - rr[0:CHUNK, LANES:]
        for g, j in units:
            d = groups[g][0]
            a, b = 2 * j, 2 * j + 1
            intra = _dot(qk[g][:, j * LANES:(j + 1) * LANES],
                         diag2(vn[(g, a)].astype(BF16), vn[(g, b)].astype(BF16), zero_tok))
            c = chunk_of(d, step)
            for hd, lanes in ((a, slice(0, LANES)), (b, slice(LANES, 2 * LANES))):
                o = colv(g, G_EGC, hd) * r[(g, j)][CHUNK:, lanes] + intra[:, lanes]
                o_ref = refs[d][5]
                o_ref[c * CHUNK:(c + 1) * CHUNK, hd * LANES:(hd + 1) * LANES] = o.astype(o_ref.dtype)
        for g in gs:
            d = groups[g][0]
            for hd in range(GDN_HEADS):
                dec = col[g][0:1, G_DEC + d * GDN_HEADS + hd:G_DEC + d * GDN_HEADS + hd + 1]
                kv = _dot_tn(k[g][hd], (colv(g, G_EKEND, hd) * vn[(g, hd)]).astype(BF16))
                s_ref[d, hd] = dec * s[(g, hd)] + kv


def _gdn_scan(q2, k2, v2, gcol, grow):
    nb = SEQ // TB
    n_tok = q2.shape[0]
    fwd = lambda b, i: (b * nb + i, 0)
    bwd = lambda b, i: (b * nb + nb - 1 - i, 0)
    fwd3 = lambda b, i: (b * nb + i, 0, 0)
    bwd3 = lambda b, i: (b * nb + nb - 1 - i, 0, 0)

    def specs(m2, m3):
        return [pl.BlockSpec((TB, HW), m2), pl.BlockSpec((TB, HW), m2), pl.BlockSpec((TB, HW), m2),
                pl.BlockSpec((TB, LANES), m2), pl.BlockSpec((NC, ROW_SLAB, CHUNK), m3)]

    return pl.pallas_call(
        _gdn_kernel,
        grid=(BATCH, nb),
        in_specs=specs(fwd, fwd3) + specs(bwd, bwd3),
        out_specs=(pl.BlockSpec((TB, HW), fwd), pl.BlockSpec((TB, HW), bwd)),
        out_shape=(jax.ShapeDtypeStruct((n_tok, HW), BF16), jax.ShapeDtypeStruct((n_tok, HW), BF16)),
        scratch_shapes=[pltpu.VMEM((2, GDN_HEADS, LANES, LANES), F32)],
        compiler_params=pltpu.CompilerParams(dimension_semantics=("parallel", "arbitrary"),
                                             vmem_limit_bytes=VMEM_LIMIT),
        name="gdn_scan",
    )(q2, k2, v2, gcol, grow, q2, k2, v2, gcol, grow)


def _outproj_kernel(o1f_ref, o1b_ref, o2f_ref, o2b_ref, zg_ref, xaq_ref, mkv_ref, x_ref, wout_ref,
                    n1_ref, n2_ref, n3_ref, fn_ref, out_ref, *, final):
    def head_norm(o, nw_ref, width):
        parts = []
        for hd in range(HEADS):
            s = o[:, hd * LANES:(hd + 1) * LANES]
            ms = jnp.sum(s * s, axis=-1, keepdims=True) * (1.0 / width)
            parts.append(s * lax.rsqrt(ms + NORM_EPS))
        return jnp.concatenate(parts, axis=-1) * nw_ref[...]

    both = lambda f_ref, b_ref: f_ref[...].astype(F32) + b_ref[...].astype(F32)
    gate = lambda lo, width: zg_ref[:, lo:lo + width].astype(F32)
    lane_q = lax.broadcasted_iota(jnp.int32, (TM_OUT, LANES), 1)
    lane_m = lax.broadcasted_iota(jnp.int32, (MEM_LEN, LANES), 1)
    first_q = lane_q < XA_DH
    first_m = lane_m < XA_DH
    q_head = (first_q, lane_q >= XA_DH)
    m_head = (first_m, lane_m >= XA_DH)
    heads = [(p, hh) for p in range(XA_HEADS // 2) for hh in range(2)]

    sc = []
    for p, hh in heads:
        qpair = xaq_ref[:, p * LANES:(p + 1) * LANES]
        qm = jnp.where(q_head[hh], qpair, jnp.zeros_like(qpair))
        sc.append(_dot_nt(qm, mkv_ref[0, :, p * LANES:(p + 1) * LANES]))

    o1 = head_norm(both(o1f_ref, o1b_ref), n1_ref, GLA_DV) * gate(0, HW)
    y = _dot(o1.astype(BF16), wout_ref[0:HW, :])

    pv = []
    for (p, hh), s in zip(heads, sc):
        e = jnp.exp(s - jnp.max(s, axis=-1, keepdims=True))
        l = jnp.sum(e, axis=-1, keepdims=True)
        mv = mkv_ref[0, :, XA_W + p * LANES:XA_W + (p + 1) * LANES]
        mvm = jnp.where(m_head[hh], mv, jnp.zeros_like(mv))
        pv.append(_dot(e.astype(BF16), mvm) * (1.0 / l))

    o2 = head_norm(both(o2f_ref, o2b_ref), n2_ref, GDN_DV) * gate(HW, HW)
    y = y + _dot(o2.astype(BF16), wout_ref[HW:2 * HW, :])

    for p in range(XA_HEADS // 2):
        lanes = slice(p * LANES, (p + 1) * LANES)
        acc = pv[2 * p] + pv[2 * p + 1]
        sq = acc * acc
        ss0 = jnp.sum(jnp.where(first_q, sq, 0.0), axis=-1, keepdims=True)
        ss1 = jnp.sum(jnp.where(first_q, 0.0, sq), axis=-1, keepdims=True)
        ms = jnp.where(first_q, ss0, ss1) * (1.0 / XA_DH)
        o3 = acc * lax.rsqrt(ms + NORM_EPS) * n3_ref[:, lanes] * gate(2 * HW + p * LANES, LANES)
        y = y + _dot(o3.astype(BF16), wout_ref[2 * HW + p * LANES:2 * HW + (p + 1) * LANES, :])

    xo = x_ref[...] + y
    if final:
        ms = jnp.mean(xo * xo, axis=-1, keepdims=True)
        xo = xo * lax.rsqrt(ms + NORM_EPS) * fn_ref[...]
    out_ref[...] = xo


def _outproj(o1f, o1b, o2f, o2b, zg, xaq, mkv, x2d, wout, n1, n2, n3, fnw, final):
    n_tok = x2d.shape[0]
    blocks_per_seq = SEQ // TM_OUT
    tok = lambda w: pl.BlockSpec((TM_OUT, w), lambda i: (i, 0))
    const = lambda shape: pl.BlockSpec(shape, lambda i: tuple(0 for _ in shape))
    return pl.pallas_call(
        functools.partial(_outproj_kernel, final=final),
        grid=(n_tok // TM_OUT,),
        in_specs=[
            tok(HW), tok(HW), tok(HW), tok(HW), tok(MIX_PAD_W), tok(XA_W),
            pl.BlockSpec((1, MEM_LEN, 2 * XA_W), lambda i: (i // blocks_per_seq, 0, 0)),
            tok(D_MODEL),
            const((MIX_PAD_W, D_MODEL)),
            const((1, HW)), const((1, HW)), const((1, XA_W)), const((1, D_MODEL)),
        ],
        out_specs=tok(D_MODEL),
        out_shape=jax.ShapeDtypeStruct((n_tok, D_MODEL), F32),
        compiler_params=pltpu.CompilerParams(dimension_semantics=("parallel",),
                                             vmem_limit_bytes=VMEM_LIMIT),
        name="outproj",
    )(o1f, o1b, o2f, o2b, zg, xaq, mkv, x2d, wout, n1, n2, n3, fnw)


def _pad_heads(w, axis):
    shp = w.shape
    w = w.reshape(shp[:axis] + (HEADS, GLA_DV) + shp[axis + 1:])
    pad = [(0, 0)] * w.ndim
    pad[axis + 1] = (0, LANES - GLA_DV)
    w = jnp.pad(w, pad)
    return w.reshape(shp[:axis] + (HW,) + shp[axis + 1:])


def _pack_layer(w_in, gla_w2, gla_b, gla_norm_w, gdn_conv_w, gdn_a_log, gdn_dt_bias, gdn_norm_w,
                xa_norm_w, w_out):
    sizes = (GLA_QK_W, GLA_QK_W, HEADS * GLA_DV, HEADS * GLA_DV, 2 * GLA_RANK, 3 * HEADS * GDN_DK,
             HEADS * GDN_DV, 2 * GDN_HEADS, 2 * GDN_HEADS, XA_W, XA_W)
    cols, start = [], 0
    for s in sizes:
        cols.append(w_in[:, start:start + s])
        start += s
    gq, gk, gv, gz, glr, dqkv, dz, db, da, xq, xz = cols
    hd = HEADS * GDN_DK
    misc = jnp.concatenate(
        [db] + [da] * M_A_COPIES
        + [jnp.zeros((D_MODEL, M_LR - M_A - 8 * M_A_COPIES), F32), glr,
           jnp.zeros((D_MODEL, LANES - M_LR - 2 * GLA_RANK), F32)], axis=1)
    w_all = jnp.concatenate(
        [gq, gk, _pad_heads(gv, 1),
         _pad_heads(dqkv[:, 0:hd], 1), _pad_heads(dqkv[:, hd:2 * hd], 1), _pad_heads(dqkv[:, 2 * hd:], 1),
         _pad_heads(gz, 1), _pad_heads(dz, 1), xz, xq, misc], axis=1).astype(BF16)

    cw = jnp.transpose(gdn_conv_w, (1, 0))
    cw = jnp.concatenate([_pad_heads(cw[:, 0:hd], 1), _pad_heads(cw[:, hd:2 * hd], 1),
                          _pad_heads(cw[:, 2 * hd:], 1)], axis=1)
    cw = jnp.pad(cw, ((0, 8 - GDN_CONV), (0, 0)))

    w2bd = jnp.zeros((LANES, 2 * GLA_QK_W), F32)
    w2bd = w2bd.at[M_LR:M_LR + GLA_RANK, 0:GLA_QK_W].set(gla_w2[0])
    w2bd = w2bd.at[M_LR + GLA_RANK:M_LR + 2 * GLA_RANK, GLA_QK_W:].set(gla_w2[1])
    w2bd = w2bd.astype(BF16)
    glab = gla_b.reshape(1, 2 * GLA_QK_W)

    def a_slab(p):
        flat = p.reshape(2 * GDN_HEADS)
        return jnp.concatenate([jnp.zeros((M_A,), F32)] + [flat] * M_A_COPIES
                               + [jnp.zeros((LANES - M_A - 8 * M_A_COPIES,), F32)])

    gparams = jnp.zeros((8, LANES), F32).at[0].set(a_slab(gdn_a_log)).at[1].set(a_slab(gdn_dt_bias))

    wout = jnp.concatenate([_pad_heads(w_out[0:HEADS * GLA_DV], 0),
                            _pad_heads(w_out[HEADS * GLA_DV:HEADS * (GLA_DV + GDN_DV)], 0),
                            w_out[HEADS * (GLA_DV + GDN_DV):]], axis=0).astype(BF16)
    pad_norm = lambda w: jnp.tile(jnp.pad(w, (0, LANES - w.shape[0])), HEADS).reshape(1, HW)
    n1 = pad_norm(gla_norm_w)
    n2 = pad_norm(gdn_norm_w)
    n3 = jnp.tile(xa_norm_w, XA_HEADS).reshape(1, XA_W)
    return w_all, cw, w2bd, glab, gparams, wout, n1, n2, n3


def kernel(x, mem, norm_w, w_in, gla_w2, gla_b, gla_norm_w, gdn_conv_w, gdn_a_log, gdn_dt_bias,
           gdn_norm_w, mem_norm_w, xa_w_kv, xa_norm_w, w_out, final_norm_w):
    assert x.shape == (BATCH, SEQ, D_MODEL) and mem.shape == (BATCH, MEM_LEN, D_MODEL)
    mkv = _memkv(mem, mem_norm_w, xa_w_kv.astype(BF16))
    h = x.reshape(BATCH * SEQ, D_MODEL)
    fnw = final_norm_w.reshape(1, D_MODEL)
    for l in range(DEPTH):
        w_all, cw, w2bd, glab, gparams, wout, n1, n2, n3 = _pack_layer(
            w_in[l], gla_w2[l], gla_b[l], gla_norm_w[l], gdn_conv_w[l], gdn_a_log[l], gdn_dt_bias[l],
            gdn_norm_w[l], xa_norm_w[l], w_out[l])
        gla, v1, dec, zg, q2, k2, v2, gcol, grow, xaq = _inproj(
            h, norm_w[l].reshape(1, D_MODEL), w_all, cw, w2bd, glab, gparams)
        o1f, o1b = _gla_scan(gla, v1, dec)
        o2f, o2b = _gdn_scan(q2, k2, v2, gcol, grow)
        h = _outproj(o1f, o1b, o2f, o2b, zg, xaq, mkv[l], h, wout, n1, n2, n3, fnw, l == DEPTH - 1)
    return h.reshape(BATCH, SEQ, D_MODEL)
```

```python
import functools

import jax
import jax.numpy as jnp
from jax import lax
from jax.experimental import pallas as pl
from jax.experimental.pallas import tpu as pltpu

F32 = jnp.float32
BF16 = jnp.bfloat16

D_MODEL = 1024
BATCH = 8
SEQ = 4096
DEPTH = 2
MEM_LEN = 256
CHUNK = 64
NORM_EPS = 1e-6
GLA_HEADS = 4
GLA_DK = 64
GLA_DV = 96
GLA_RANK = 16
GLA_GATE_NORMALIZER = 16.0
GDN_HEADS = 4
GDN_DK = 96
GDN_DV = 96
GDN_CONV = 5
XA_HEADS = 4
XA_DH = 64

LANES = 128
HEADS = 4
HW = HEADS * LANES
GLA_QK_W = GLA_HEADS * GLA_DK
XA_W = XA_HEADS * XA_DH
MIX_PAD_W = 2 * HW + XA_W

C_GQ = 0
C_GK = C_GQ + GLA_QK_W
C_GV = C_GK + GLA_QK_W
C_DQKV = C_GV + HW
C_Z = C_DQKV + 3 * HW
C_XQ = C_Z + MIX_PAD_W
C_MISC = C_XQ + XA_W
IN_PAD_W = C_MISC + LANES

M_BETA = 0
M_A = 8
M_A_COPIES = 4
M_LR = 64
G_BETA, G_GC, G_EGC, G_EKEND, G_DEC = 0, 8, 16, 24, 32
ROW_SLAB = 16

TM_IN = 512
TM_OUT = 512
NC = 8
TB = NC * CHUNK
HALO = 8
CONV_PHASES = 4
VMEM_LIMIT = 56 * 1024 * 1024


def _dot(a, b):
    return jnp.dot(a, b, preferred_element_type=F32)


def _dot_nt(a, b):
    return lax.dot_general(a, b, (((1,), (1,)), ((), ())), preferred_element_type=F32)


def _dot_tn(a, b):
    return lax.dot_general(a, b, (((0,), (0,)), ((), ())), preferred_element_type=F32)


def _sigmoid(x):
    return 1.0 / (1.0 + jnp.exp(-x))


def _silu(x):
    return x * _sigmoid(x)


def _softplus(x):
    return jnp.maximum(x, 0.0) + jnp.log(1.0 + jnp.exp(-jnp.abs(x)))


def _log_sigmoid(x):
    return jnp.minimum(x, 0.0) - jnp.log(1.0 + jnp.exp(-jnp.abs(x)))


def _split2(x):
    hi = pltpu.bitcast(pltpu.bitcast(x, jnp.int32) & jnp.int32(-65536), F32)
    return hi.astype(BF16), (x - hi).astype(BF16)


def _tri_sum(tri, x):
    hi, lo = _split2(x)
    return _dot(tri, hi) + _dot(tri, lo)


def _tri_masks():
    ri = lax.broadcasted_iota(jnp.int32, (CHUNK, CHUNK), 0)
    ci = lax.broadcasted_iota(jnp.int32, (CHUNK, CHUNK), 1)
    return ri, ci


def _memkv_kernel(mem_ref, nw_ref, w_ref, out_ref):
    m = mem_ref[0]
    ms = jnp.mean(m * m, axis=-1, keepdims=True)
    mn = (m * lax.rsqrt(ms + NORM_EPS) * nw_ref[0]).astype(BF16)
    out_ref[0, 0] = _dot(mn, w_ref[0]).astype(BF16)


def _memkv(mem, mem_norm_w, xa_w_kv_bf16):
    return pl.pallas_call(
        _memkv_kernel,
        grid=(DEPTH, BATCH),
        in_specs=[
            pl.BlockSpec((1, MEM_LEN, D_MODEL), lambda l, b: (b, 0, 0)),
            pl.BlockSpec((1, 1, D_MODEL), lambda l, b: (l, 0, 0)),
            pl.BlockSpec((1, D_MODEL, 2 * XA_W), lambda l, b: (l, 0, 0)),
        ],
        out_specs=pl.BlockSpec((1, 1, MEM_LEN, 2 * XA_W), lambda l, b: (l, b, 0, 0)),
        out_shape=jax.ShapeDtypeStruct((DEPTH, BATCH, MEM_LEN, 2 * XA_W), BF16),
        compiler_params=pltpu.CompilerParams(dimension_semantics=("parallel", "parallel")),
        name="memkv",
    )(mem, mem_norm_w.reshape(DEPTH, 1, D_MODEL), xa_w_kv_bf16)


def _inproj_kernel(x_ref, xp_ref, xn_ref, nw_ref, w_ref, cw_ref, w2_ref, gb_ref, gp_ref,
                   gla_ref, v1_ref, dec_ref, zg_ref, q2_ref, k2_ref, v2_ref, gcol_ref, grow_ref,
                   xaq_ref, ext_ref, conv_ref):
    nw = nw_ref[...]

    def norm(ref):
        x = ref[...]
        ms = jnp.mean(x * x, axis=-1, keepdims=True)
        return (x * lax.rsqrt(ms + NORM_EPS) * nw).astype(BF16)

    h = norm(x_ref)
    h_prev = norm(xp_ref)
    h_next = norm(xn_ref)

    def proj(hh, lo, width):
        return _dot(hh, w_ref[:, lo:lo + width])

    blocks_per_seq = SEQ // TM_IN
    j = lax.rem(pl.program_id(0), blocks_per_seq)
    cw = cw_ref[...]
    base = HALO - GDN_CONV // 2

    def gdn_mm(grp):
        lo = C_DQKV + grp * HW
        return proj(h, lo, HW), proj(h_prev, lo, HW), proj(h_next, lo, HW)

    def gdn_vpu(grp, res):
        main, prev, nxt = res
        for hd in range(HEADS):
            slab = grp * HEADS + hd
            lanes = slice(hd * LANES, (hd + 1) * LANES)
            ext_ref[slab, 0:HALO, :] = jnp.where(j == 0, 0.0, prev[:, lanes])
            ext_ref[slab, HALO:HALO + TM_IN, :] = main[:, lanes]
            ext_ref[slab, HALO + TM_IN:2 * HALO + TM_IN, :] = jnp.where(j == blocks_per_seq - 1, 0.0,
                                                                     nxt[:, lanes])
            taps = cw[:, grp * HW + hd * LANES:grp * HW + (hd + 1) * LANES]
            for p in range(CONV_PHASES):
                acc = None
                for t in range(GDN_CONV):
                    win = ext_ref[slab, pl.ds(base + p + t, TM_IN // CONV_PHASES, stride=CONV_PHASES), :]
                    acc = win * taps[t:t + 1, :] if acc is None else acc + win * taps[t:t + 1, :]
                conv_ref[slab, pl.ds(p, TM_IN // CONV_PHASES, stride=CONV_PHASES), :] = acc
            y = _silu(conv_ref[slab])
            if grp == 2:
                v2_ref[:, lanes] = y.astype(BF16)
            else:
                ref, scale = ((q2_ref, GDN_DK ** -0.5), (k2_ref, 1.0))[grp]
                ss = jnp.sum(y * y, axis=-1, keepdims=True)
                ref[:, lanes] = (y * lax.rsqrt(ss + NORM_EPS) * scale).astype(BF16)

    def z_vpu(lo, width, res):
        zg_ref[:, lo:lo + width] = _silu(res).astype(BF16)

    def xaq_vpu(res):
        xaq_ref[...] = (res * (XA_DH ** -0.5)).astype(BF16)

    lane = lax.broadcasted_iota(jnp.int32, (1, LANES), 1)
    fwd_lane = lax.rem(lane, 8) < GDN_HEADS
    ri, ci = _tri_masks()
    lower = jnp.where(ri >= ci, 1.0, 0.0).astype(BF16)
    upper = jnp.where(ri <= ci, 1.0, 0.0).astype(BF16)
    gate = {}

    def misc_vpu(m):
        logits = _dot(m.astype(BF16), w2_ref[...]) + gb_ref[...]
        gate["g"] = _log_sigmoid(logits) * (1.0 / GLA_GATE_NORMALIZER)
        is_a = (lane >= M_A) & (lane < M_A + 8 * M_A_COPIES)
        neg_a = jnp.where(is_a, -jnp.exp(gp_ref[0:1, :]), 0.0)
        gg = neg_a * _softplus(m + gp_ref[1:2, :])
        beta = _sigmoid(m)
        for c in range(TM_IN // CHUNK):
            rows = slice(c * CHUNK, (c + 1) * CHUNK)
            ggc = gg[rows]
            pf = _tri_sum(lower, ggc)
            sf = _tri_sum(upper, ggc)
            gc = jnp.where(fwd_lane, pf, sf)
            last = jnp.where(fwd_lane, pf[CHUNK - 1:CHUNK], sf[0:1])
            col = jnp.where(lane < G_GC, beta[rows],
                  jnp.where(lane < G_EGC, gc,
                  jnp.where(lane < G_EKEND, jnp.exp(gc),
                  jnp.where(lane < G_DEC, jnp.exp(last - gc), jnp.exp(last)))))
            gcol_ref[rows, :] = col
            grow_ref[c] = col.T[0:ROW_SLAB, :]

    def gla_v_vpu(res):
        v1_ref[...] = res.astype(BF16)

    def gla_qk_vpu(pg):
        q1 = pg[:, 0:GLA_QK_W] * (GLA_DK ** -0.5)
        k1 = pg[:, GLA_QK_W:2 * GLA_QK_W]
        g = gate["g"]
        for c in range(TM_IN // CHUNK):
            rows = slice(c * CHUNK, (c + 1) * CHUNK)
            gch = g[rows]
            q1c = q1[rows]
            k1c = k1[rows]
            bf = _tri_sum(lower, gch[:, 0:GLA_QK_W])
            br = _tri_sum(upper, gch[:, GLA_QK_W:])
            for d, (b, last) in enumerate(((bf, bf[CHUNK - 1:CHUNK]), (br, br[0:1]))):
                off = d * 3 * GLA_QK_W
                gla_ref[rows, off:off + GLA_QK_W] = (q1c * jnp.exp(b)).astype(BF16)
                gla_ref[rows, off + GLA_QK_W:off + 2 * GLA_QK_W] = (k1c * jnp.exp(-b)).astype(BF16)
                gla_ref[rows, off + 2 * GLA_QK_W:off + 3 * GLA_QK_W] = (k1c * jnp.exp(last - b)).astype(BF16)
                dec_ref[c, :, d * GLA_QK_W:(d + 1) * GLA_QK_W] = jnp.exp(last)

    part = functools.partial
    tasks = [
        (part(proj, h, C_MISC, LANES), misc_vpu),
        (part(gdn_mm, 0), part(gdn_vpu, 0)),
        (part(proj, h, C_Z, HW), part(z_vpu, 0, HW)),
        (part(gdn_mm, 1), part(gdn_vpu, 1)),
        (part(proj, h, C_Z + HW, HW), part(z_vpu, HW, HW)),
        (part(gdn_mm, 2), part(gdn_vpu, 2)),
        (part(proj, h, C_Z + 2 * HW, XA_W), part(z_vpu, 2 * HW, XA_W)),
        (part(proj, h, C_GQ, 2 * GLA_QK_W), gla_qk_vpu),
        (part(proj, h, C_GV, HW), gla_v_vpu),
        (part(proj, h, C_XQ, XA_W), xaq_vpu),
    ]
    res = tasks[0][0]()
    for t, (_, vpu) in enumerate(tasks):
        nxt_res = tasks[t + 1][0]() if t + 1 < len(tasks) else None
        vpu(res)
        res = nxt_res


def _inproj(layer, x2d, norm_w, w_all, conv_w, w2bd, gla_b, gdn_params):
    n_tok = x2d.shape[0]
    nblk = n_tok // TM_IN
    halo_blocks = TM_IN // HALO
    n_halo = n_tok // HALO
    nchunk = TM_IN // CHUNK
    tok = lambda w: pl.BlockSpec((TM_IN, w), lambda i: (i, 0))
    const = lambda shape: pl.BlockSpec((None,) + shape, lambda i: (layer,) + tuple(0 for _ in shape))
    out_shapes = (
        jax.ShapeDtypeStruct((n_tok, 6 * GLA_QK_W), BF16),
        jax.ShapeDtypeStruct((n_tok, HW), BF16),
        jax.ShapeDtypeStruct((n_tok // CHUNK, 1, 2 * GLA_QK_W), F32),
        jax.ShapeDtypeStruct((n_tok, MIX_PAD_W), BF16),
        jax.ShapeDtypeStruct((n_tok, HW), BF16),
        jax.ShapeDtypeStruct((n_tok, HW), BF16),
        jax.ShapeDtypeStruct((n_tok, HW), BF16),
        jax.ShapeDtypeStruct((n_tok, LANES), F32),
        jax.ShapeDtypeStruct((n_tok // CHUNK, ROW_SLAB, CHUNK), F32),
        jax.ShapeDtypeStruct((n_tok, XA_W), BF16),
    )
    out_specs = (
        tok(6 * GLA_QK_W), tok(HW),
        pl.BlockSpec((nchunk, 1, 2 * GLA_QK_W), lambda i: (i, 0, 0)),
        tok(MIX_PAD_W), tok(HW), tok(HW), tok(HW), tok(LANES),
        pl.BlockSpec((nchunk, ROW_SLAB, CHUNK), lambda i: (i, 0, 0)),
        tok(XA_W),
    )
    return pl.pallas_call(
        _inproj_kernel,
        grid=(nblk,),
        in_specs=[
            tok(D_MODEL),
            pl.BlockSpec((HALO, D_MODEL), lambda i: (jnp.maximum(i * halo_blocks - 1, 0), 0)),
            pl.BlockSpec((HALO, D_MODEL), lambda i: (jnp.minimum((i + 1) * halo_blocks, n_halo - 1), 0)),
            const((1, D_MODEL)),
            const((D_MODEL, IN_PAD_W)),
            const((8, 3 * HW)),
            const((LANES, 2 * GLA_QK_W)),
            const((1, 2 * GLA_QK_W)),
            const((8, LANES)),
        ],
        out_specs=out_specs,
        out_shape=out_shapes,
        scratch_shapes=[pltpu.VMEM((3 * HEADS, TM_IN + 2 * HALO, LANES), F32),
                        pltpu.VMEM((3 * HEADS, TM_IN, LANES), F32)],
        compiler_params=pltpu.CompilerParams(dimension_semantics=("parallel",),
                                             vmem_limit_bytes=VMEM_LIMIT),
        name="inproj",
    )(x2d, x2d, x2d, norm_w, w_all, conv_w, w2bd, gla_b, gdn_params)


def _gla_kernel(gf_ref, gb_ref, vf_ref, vb_ref, decf_ref, decb_ref, of_ref, ob_ref, s_ref):
    @pl.when(pl.program_id(1) == 0)
    def _():
        s_ref[...] = jnp.zeros_like(s_ref)

    ri = lax.broadcasted_iota(jnp.int32, (CHUNK, LANES), 0)
    lane_tok = lax.broadcasted_iota(jnp.int32, (CHUNK, LANES), 1)
    ci = lane_tok & (CHUNK - 1)
    incl = (ri >= ci, ri <= ci)
    left = lane_tok < GLA_DK
    lane_sq = lax.broadcasted_iota(jnp.int32, (LANES, LANES), 1) < GLA_DK
    zero_bf = jnp.zeros((CHUNK, LANES), BF16)
    cat = jnp.concatenate

    def split_heads(x):
        return cat([jnp.where(left, x, zero_bf), jnp.where(left, zero_bf, x)], axis=0)

    g_refs, v_refs = (gf_ref, gb_ref), (vf_ref, vb_ref)
    dec_refs, o_refs = (decf_ref, decb_ref), (of_ref, ob_ref)
    chunk_of = lambda d, step: step if d == 0 else NC - 1 - step
    units = [(d, step, p) for step in range(NC) for d in range(2) for p in range(GLA_HEADS // 2)]

    qe, kend, v, a = {}, {}, {}, {}
    for u in units:
        d, step, p = u
        c = chunk_of(d, step)
        rows = slice(c * CHUNK, (c + 1) * CHUNK)
        qe[u] = g_refs[d][rows, p * LANES:(p + 1) * LANES]
        ke = g_refs[d][rows, GLA_QK_W + p * LANES:GLA_QK_W + (p + 1) * LANES]
        kend[u] = g_refs[d][rows, 2 * GLA_QK_W + p * LANES:2 * GLA_QK_W + (p + 1) * LANES]
        v[u] = (v_refs[d][rows, 2 * p * LANES:(2 * p + 1) * LANES],
                v_refs[d][rows, (2 * p + 1) * LANES:(2 * p + 2) * LANES])
        a[u] = jnp.where(incl[d], _dot_nt(qe[u], split_heads(ke)), 0.0).astype(BF16)
    intra = {u: _dot(a[u], cat([cat([v[u][0], zero_bf], axis=1), cat([zero_bf, v[u][1]], axis=1)], axis=0))
             for u in units}
    kv = {u: _dot_tn(cat([v[u][0], v[u][1]], axis=1), kend[u]) for u in units}

    for d in range(2):
        for p in range(GLA_HEADS // 2):
            st = s_ref[d, p]
            for step in range(NC):
                u = (d, step, p)
                c = chunk_of(d, step)
                inter = _dot_nt(split_heads(qe[u]), st.astype(BF16))
                for hh in range(2):
                    hd = 2 * p + hh
                    o = intra[u][:, hh * LANES:(hh + 1) * LANES] + inter[hh * CHUNK:(hh + 1) * CHUNK]
                    o_refs[d][c * CHUNK:(c + 1) * CHUNK, hd * LANES:(hd + 1) * LANES] = o.astype(o_refs[d].dtype)
                dec = dec_refs[d][c, :, p * LANES:(p + 1) * LANES]
                st = dec * st + jnp.where(lane_sq, kv[u][0:LANES], kv[u][LANES:])
            s_ref[d, p] = st


def _gla_scan(gla, v1, dec):
    nb = SEQ // TB
    fwd = lambda b, i: (b * nb + i, 0)
    bwd = lambda b, i: (b * nb + nb - 1 - i, 0)
    n_tok = gla.shape[0]
    return pl.pallas_call(
        _gla_kernel,
        grid=(BATCH, nb),
        in_specs=[
            pl.BlockSpec((TB, 3 * GLA_QK_W), fwd),
            pl.BlockSpec((TB, 3 * GLA_QK_W), lambda b, i: (b * nb + nb - 1 - i, 1)),
            pl.BlockSpec((TB, HW), fwd),
            pl.BlockSpec((TB, HW), bwd),
            pl.BlockSpec((NC, 1, GLA_QK_W), lambda b, i: (b * nb + i, 0, 0)),
            pl.BlockSpec((NC, 1, GLA_QK_W), lambda b, i: (b * nb + nb - 1 - i, 0, 1)),
        ],
        out_specs=(pl.BlockSpec((TB, HW), fwd), pl.BlockSpec((TB, HW), bwd)),
        out_shape=(jax.ShapeDtypeStruct((n_tok, HW), BF16), jax.ShapeDtypeStruct((n_tok, HW), BF16)),
        scratch_shapes=[pltpu.VMEM((2, GLA_HEADS // 2, LANES, LANES), F32)],
        compiler_params=pltpu.CompilerParams(dimension_semantics=("parallel", "arbitrary"),
                                             vmem_limit_bytes=VMEM_LIMIT),
        name="gla_scan",
    )(gla, gla, v1, v1, dec, dec)


def _gdn_kernel(qf_ref, kf_ref, vf_ref, colf_ref, rowf_ref, qb_ref, kb_ref, vb_ref, colb_ref, rowb_ref,
                of_ref, ob_ref, s_ref):
    @pl.when(pl.program_id(1) == 0)
    def _():
        s_ref[...] = jnp.zeros_like(s_ref)

    pk = GDN_HEADS * CHUNK
    ri = lax.broadcasted_iota(jnp.int32, (CHUNK, pk), 0)
    ci = lax.broadcasted_iota(jnp.int32, (CHUNK, pk), 1) & (CHUNK - 1)
    incl = (ri >= ci, ri <= ci)
    strict = (ri > ci, ri < ci)
    eye = jnp.where(ri == ci, 1.0, 0.0)
    blk = lambda axis: lax.shift_right_logical(lax.broadcasted_iota(jnp.int32, (pk, pk), axis), 6)
    same_blk = blk(0) == blk(1)
    bd_mask = jnp.where(same_blk, 1.0, 0.0).astype(BF16)
    left = lax.broadcasted_iota(jnp.int32, (CHUNK, LANES), 1) < CHUNK
    zero_tok = jnp.zeros((CHUNK, LANES), BF16)
    zero_sq = jnp.zeros((LANES, LANES), BF16)

    refs = ((qf_ref, kf_ref, vf_ref, colf_ref, rowf_ref, of_ref),
            (qb_ref, kb_ref, vb_ref, colb_ref, rowb_ref, ob_ref))
    groups = [(d, step) for step in range(NC) for d in range(2)]
    pairs = range(GDN_HEADS // 2)
    each = lambda fn, *lists: [fn(*args) for args in zip(*lists)]
    top = lambda m: m[0:CHUNK]
    bot = lambda m: m[CHUNK:2 * CHUNK]
    cat = jnp.concatenate
    chunk_of = lambda d, step: step if d == 0 else NC - 1 - step

    def block_diag(x):
        return cat([x, x, x, x], axis=0) * bd_mask

    def diag2(a, b, zero):
        return cat([cat([a, zero], axis=1), cat([zero, b], axis=1)], axis=0)

    q, k, v, col, gc_r = [], [], [], [], []
    for d, step in groups:
        q_ref, k_ref, v_ref, col_ref, row_ref, _ = refs[d]
        c = chunk_of(d, step)
        rows = slice(c * CHUNK, (c + 1) * CHUNK)
        heads = [slice(hd * LANES, (hd + 1) * LANES) for hd in range(GDN_HEADS)]
        q.append([q_ref[rows, h] for h in heads])
        k.append([k_ref[rows, h] for h in heads])
        v.append([v_ref[rows, h] for h in heads])
        col.append(col_ref[rows, :])
        rowt = row_ref[c]
        gc_r.append(cat([rowt[8 + d * GDN_HEADS + hd:9 + d * GDN_HEADS + hd, :] for hd in range(GDN_HEADS)],
                        axis=1))

    def colv(g, base, hd):
        lane = base + groups[g][0] * GDN_HEADS + hd
        return col[g][:, lane:lane + 1]

    def col_packed(g, base):
        tiles = [jnp.where(left, jnp.broadcast_to(colv(g, base, 2 * j), (CHUNK, LANES)),
                           jnp.broadcast_to(colv(g, base, 2 * j + 1), (CHUNK, LANES))) for j in pairs]
        return cat(tiles, axis=1)

    ng = len(groups)
    kq = [[_dot_nt(cat([cat([k[g][2 * j], k[g][2 * j + 1]], axis=1),
                        cat([q[g][2 * j], q[g][2 * j + 1]], axis=1)], axis=0),
                   diag2(k[g][2 * j], k[g][2 * j + 1], zero_tok)) for j in pairs] for g in range(ng)]
    kk = [cat([top(kq[g][j]) for j in pairs], axis=1) for g in range(ng)]
    qkr = [cat([bot(kq[g][j]) for j in pairs], axis=1) for g in range(ng)]
    dirs = [d for d, _ in groups]
    decay = [jnp.where(incl[d], jnp.exp(jnp.where(incl[d], col_packed(g, G_GC) - gc_r[g], 0.0)), 0.0)
             for g, d in enumerate(dirs)]
    n = [jnp.where(strict[d], -(kk[g] * decay[g] * col_packed(g, G_BETA)), 0.0) for g, d in enumerate(dirs)]
    qk = [jnp.where(incl[d], qkr[g] * decay[g], 0.0).astype(BF16) for g, d in enumerate(dirs)]
    pm = each(lambda a: eye + a, n)
    cur = each(lambda a: _dot(a.astype(BF16), block_diag(a.astype(BF16))), n)
    for _ in range(4):
        st = each(lambda p, cc: _dot(cat([p.astype(BF16), cc.astype(BF16)], axis=0),
                                     block_diag(cc.astype(BF16))), pm, cur)
        pm = each(lambda p, s_: p + top(s_), pm, st)
        cur = each(bot, st)
    tinv = each(lambda p, cc: p + _dot(p.astype(BF16), block_diag(cc.astype(BF16))), pm, cur)

    def rhs_of(g, hd):
        beta = colv(g, G_BETA, hd)
        return cat([(v[g][hd].astype(F32) * beta).astype(BF16),
                    (k[g][hd].astype(F32) * (beta * colv(g, G_EGC, hd))).astype(BF16)], axis=1)

    def uw_of(g, j):
        t = tinv[g][:, j * LANES:(j + 1) * LANES]
        lhs = cat([jnp.where(left, t, 0.0), jnp.where(left, 0.0, t)], axis=0).astype(BF16)
        return _dot(lhs, cat([rhs_of(g, 2 * j), rhs_of(g, 2 * j + 1)], axis=0))

    uw = [[uw_of(g, j) for j in pairs] for g in range(ng)]

    for step in range(NC):
        gs = [g for g, (_, st_) in enumerate(groups) if st_ == step]
        units = [(g, j) for g in gs for j in pairs]
        s = {(g, hd): s_ref[groups[g][0], hd] for g in gs for hd in range(GDN_HEADS)}
        r = {}
        for g, j in units:
            a, b = 2 * j, 2 * j + 1
            x = uw[g][j]
            lhs = cat([cat([x[0:CHUNK, LANES:].astype(BF16), x[CHUNK:, LANES:].astype(BF16)], axis=1),
                       cat([q[g][a], q[g][b]], axis=1)], axis=0)
            r[(g, j)] = _dot(lhs, diag2(s[(g, a)].astype(BF16), s[(g, b)].astype(BF16), zero_sq))
        vn = {}
        for g, j in units:
            x, rr = uw[g][j], r[(g, j)]
            vn[(g, 2 * j)] = x[0:CHUNK, 0:LANES] - rr[0:CHUNK, 0:LANES]
            vn[(g, 2 * j + 1)] = x[CHUNK:, 0:LANES] - rr[0:CHUNK, LANES:]
        for g, j in units:
            d = groups[g][0]
            a, b = 2 * j, 2 * j + 1
            intra = _dot(qk[g][:, j * LANES:(j + 1) * LANES],
                         diag2(vn[(g, a)].astype(BF16), vn[(g, b)].astype(BF16), zero_tok))
            c = chunk_of(d, step)
            for hd, lanes in ((a, slice(0, LANES)), (b, slice(LANES, 2 * LANES))):
                o = colv(g, G_EGC, hd) * r[(g, j)][CHUNK:, lanes] + intra[:, lanes]
                o_ref = refs[d][5]
                o_ref[c * CHUNK:(c + 1) * CHUNK, hd * LANES:(hd + 1) * LANES] = o.astype(o_ref.dtype)
        for g in gs:
            d = groups[g][0]
            for hd in range(GDN_HEADS):
                dec = col[g][0:1, G_DEC + d * GDN_HEADS + hd:G_DEC + d * GDN_HEADS + hd + 1]
                kv = _dot_tn(k[g][hd], (colv(g, G_EKEND, hd) * vn[(g, hd)]).astype(BF16))
                s_ref[d, hd] = dec * s[(g, hd)] + kv


def _gdn_scan(q2, k2, v2, gcol, grow):
    nb = SEQ // TB
    n_tok = q2.shape[0]
    fwd = lambda b, i: (b * nb + i, 0)
    bwd = lambda b, i: (b * nb + nb - 1 - i, 0)
    fwd3 = lambda b, i: (b * nb + i, 0, 0)
    bwd3 = lambda b, i: (b * nb + nb - 1 - i, 0, 0)

    def specs(m2, m3):
        return [pl.BlockSpec((TB, HW), m2), pl.BlockSpec((TB, HW), m2), pl.BlockSpec((TB, HW), m2),
                pl.BlockSpec((TB, LANES), m2), pl.BlockSpec((NC, ROW_SLAB, CHUNK), m3)]

    return pl.pallas_call(
        _gdn_kernel,
        grid=(BATCH, nb),
        in_specs=specs(fwd, fwd3) + specs(bwd, bwd3),
        out_specs=(pl.BlockSpec((TB, HW), fwd), pl.BlockSpec((TB, HW), bwd)),
        out_shape=(jax.ShapeDtypeStruct((n_tok, HW), BF16), jax.ShapeDtypeStruct((n_tok, HW), BF16)),
        scratch_shapes=[pltpu.VMEM((2, GDN_HEADS, LANES, LANES), F32)],
        compiler_params=pltpu.CompilerParams(dimension_semantics=("parallel", "arbitrary"),
                                             vmem_limit_bytes=VMEM_LIMIT),
        name="gdn_scan",
    )(q2, k2, v2, gcol, grow, q2, k2, v2, gcol, grow)


def _outproj_kernel(o1f_ref, o1b_ref, o2f_ref, o2b_ref, zg_ref, xaq_ref, mkv_ref, x_ref, wout_ref,
                    n1_ref, n2_ref, n3_ref, fn_ref, out_ref, *, final):
    def head_norm(o, nw_ref, width):
        parts = []
        for hd in range(HEADS):
            s = o[:, hd * LANES:(hd + 1) * LANES]
            ms = jnp.sum(s * s, axis=-1, keepdims=True) * (1.0 / width)
            parts.append(s * lax.rsqrt(ms + NORM_EPS))
        return jnp.concatenate(parts, axis=-1) * nw_ref[...]

    both = lambda f_ref, b_ref: f_ref[...].astype(F32) + b_ref[...].astype(F32)
    gate = lambda lo, width: zg_ref[:, lo:lo + width].astype(F32)
    lane_q = lax.broadcasted_iota(jnp.int32, (TM_OUT, LANES), 1)
    lane_m = lax.broadcasted_iota(jnp.int32, (MEM_LEN, LANES), 1)
    first_q = lane_q < XA_DH
    first_m = lane_m < XA_DH
    q_head = (first_q, lane_q >= XA_DH)
    m_head = (first_m, lane_m >= XA_DH)
    heads = [(p, hh) for p in range(XA_HEADS // 2) for hh in range(2)]

    sc = []
    for p, hh in heads:
        qpair = xaq_ref[:, p * LANES:(p + 1) * LANES]
        qm = jnp.where(q_head[hh], qpair, jnp.zeros_like(qpair))
        sc.append(_dot_nt(qm, mkv_ref[0, :, p * LANES:(p + 1) * LANES]))

    o1 = head_norm(both(o1f_ref, o1b_ref), n1_ref, GLA_DV) * gate(0, HW)
    y = _dot(o1.astype(BF16), wout_ref[0:HW, :])

    pv = []
    for (p, hh), s in zip(heads, sc):
        e = jnp.exp(s - jnp.max(s, axis=-1, keepdims=True))
        l = jnp.sum(e, axis=-1, keepdims=True)
        mv = mkv_ref[0, :, XA_W + p * LANES:XA_W + (p + 1) * LANES]
        mvm = jnp.where(m_head[hh], mv, jnp.zeros_like(mv))
        pv.append(_dot(e.astype(BF16), mvm) * (1.0 / l))

    o2 = head_norm(both(o2f_ref, o2b_ref), n2_ref, GDN_DV) * gate(HW, HW)
    y = y + _dot(o2.astype(BF16), wout_ref[HW:2 * HW, :])

    for p in range(XA_HEADS // 2):
        lanes = slice(p * LANES, (p + 1) * LANES)
        acc = pv[2 * p] + pv[2 * p + 1]
        sq = acc * acc
        ss0 = jnp.sum(jnp.where(first_q, sq, 0.0), axis=-1, keepdims=True)
        ss1 = jnp.sum(jnp.where(first_q, 0.0, sq), axis=-1, keepdims=True)
        ms = jnp.where(first_q, ss0, ss1) * (1.0 / XA_DH)
        o3 = acc * lax.rsqrt(ms + NORM_EPS) * n3_ref[:, lanes] * gate(2 * HW + p * LANES, LANES)
        y = y + _dot(o3.astype(BF16), wout_ref[2 * HW + p * LANES:2 * HW + (p + 1) * LANES, :])

    xo = x_ref[...] + y
    if final:
        ms = jnp.mean(xo * xo, axis=-1, keepdims=True)
        xo = xo * lax.rsqrt(ms + NORM_EPS) * fn_ref[...]
    out_ref[...] = xo


def _outproj(layer, o1f, o1b, o2f, o2b, zg, xaq, mkv, x2d, wout, n1, n2, n3, fnw):
    n_tok = x2d.shape[0]
    blocks_per_seq = SEQ // TM_OUT
    tok = lambda w: pl.BlockSpec((TM_OUT, w), lambda i: (i, 0))
    const = lambda shape: pl.BlockSpec((None,) + shape, lambda i: (layer,) + tuple(0 for _ in shape))
    return pl.pallas_call(
        functools.partial(_outproj_kernel, final=layer == DEPTH - 1),
        grid=(n_tok // TM_OUT,),
        in_specs=[
            tok(HW), tok(HW), tok(HW), tok(HW), tok(MIX_PAD_W), tok(XA_W),
            pl.BlockSpec((None, 1, MEM_LEN, 2 * XA_W), lambda i: (layer, i // blocks_per_seq, 0, 0)),
            tok(D_MODEL),
            const((MIX_PAD_W, D_MODEL)),
            const((1, HW)), const((1, HW)), const((1, XA_W)),
            pl.BlockSpec((1, D_MODEL), lambda i: (0, 0)),
        ],
        out_specs=tok(D_MODEL),
        out_shape=jax.ShapeDtypeStruct((n_tok, D_MODEL), F32),
        compiler_params=pltpu.CompilerParams(dimension_semantics=("parallel",),
                                             vmem_limit_bytes=VMEM_LIMIT),
        name="outproj",
    )(o1f, o1b, o2f, o2b, zg, xaq, mkv, x2d, wout, n1, n2, n3, fnw)


def _pad_heads(w, axis):
    shp = w.shape
    w = w.reshape(shp[:axis] + (HEADS, GLA_DV) + shp[axis + 1:])
    pad = [(0, 0)] * w.ndim
    pad[axis + 1] = (0, LANES - GLA_DV)
    w = jnp.pad(w, pad)
    return w.reshape(shp[:axis] + (HW,) + shp[axis + 1:])


def _pack_params(norm_w, w_in, gla_w2, gla_b, gla_norm_w, gdn_conv_w, gdn_a_log, gdn_dt_bias, gdn_norm_w,
                 xa_norm_w, w_out):
    sizes = (GLA_QK_W, GLA_QK_W, HEADS * GLA_DV, HEADS * GLA_DV, 2 * GLA_RANK, 3 * HEADS * GDN_DK,
             HEADS * GDN_DV, 2 * GDN_HEADS, 2 * GDN_HEADS, XA_W, XA_W)
    cols, start = [], 0
    for s in sizes:
        cols.append(w_in[:, :, start:start + s])
        start += s
    gq, gk, gv, gz, glr, dqkv, dz, db, da, xq, xz = cols
    hd = HEADS * GDN_DK
    zeros = lambda *shape: jnp.zeros((DEPTH,) + shape, F32)
    misc = jnp.concatenate(
        [db] + [da] * M_A_COPIES
        + [zeros(D_MODEL, M_LR - M_A - 8 * M_A_COPIES), glr, zeros(D_MODEL, LANES - M_LR - 2 * GLA_RANK)],
        axis=2)
    w_all = jnp.concatenate(
        [gq, gk, _pad_heads(gv, 2),
         _pad_heads(dqkv[:, :, 0:hd], 2), _pad_heads(dqkv[:, :, hd:2 * hd], 2), _pad_heads(dqkv[:, :, 2 * hd:], 2),
         _pad_heads(gz, 2), _pad_heads(dz, 2), xz, xq, misc], axis=2).astype(BF16)

    cw = jnp.transpose(gdn_conv_w, (0, 2, 1))
    cw = jnp.concatenate([_pad_heads(cw[:, :, 0:hd], 2), _pad_heads(cw[:, :, hd:2 * hd], 2),
                          _pad_heads(cw[:, :, 2 * hd:], 2)], axis=2)
    cw = jnp.pad(cw, ((0, 0), (0, 8 - GDN_CONV), (0, 0)))

    w2bd = jnp.concatenate(
        [zeros(M_LR, 2 * GLA_QK_W),
         jnp.concatenate([gla_w2[:, 0], zeros(GLA_RANK, GLA_QK_W)], axis=2),
         jnp.concatenate([zeros(GLA_RANK, GLA_QK_W), gla_w2[:, 1]], axis=2),
         zeros(LANES - M_LR - 2 * GLA_RANK, 2 * GLA_QK_W)], axis=1).astype(BF16)
    glab = gla_b.reshape(DEPTH, 1, 2 * GLA_QK_W)

    def a_slab(p):
        flat = p.reshape(DEPTH, 1, 2 * GDN_HEADS)
        return jnp.concatenate([zeros(1, M_A)] + [flat] * M_A_COPIES + [zeros(1, LANES - M_A - 8 * M_A_COPIES)],
                               axis=2)

    gparams = jnp.concatenate([a_slab(gdn_a_log), a_slab(gdn_dt_bias), zeros(6, LANES)], axis=1)

    wout = jnp.concatenate([_pad_heads(w_out[:, 0:HEADS * GLA_DV], 1),
                            _pad_heads(w_out[:, HEADS * GLA_DV:HEADS * (GLA_DV + GDN_DV)], 1),
                            w_out[:, HEADS * (GLA_DV + GDN_DV):]], axis=1).astype(BF16)
    pad_norm = lambda w: jnp.tile(jnp.pad(w, ((0, 0), (0, LANES - w.shape[1]))), (1, HEADS)).reshape(DEPTH, 1, HW)
    n1 = pad_norm(gla_norm_w)
    n2 = pad_norm(gdn_norm_w)
    n3 = jnp.tile(xa_norm_w, (1, XA_HEADS)).reshape(DEPTH, 1, XA_W)
    return norm_w.reshape(DEPTH, 1, D_MODEL), w_all, cw, w2bd, glab, gparams, wout, n1, n2, n3


def kernel(x, mem, norm_w, w_in, gla_w2, gla_b, gla_norm_w, gdn_conv_w, gdn_a_log, gdn_dt_bias,
           gdn_norm_w, mem_norm_w, xa_w_kv, xa_norm_w, w_out, final_norm_w):
    assert x.shape == (BATCH, SEQ, D_MODEL) and mem.shape == (BATCH, MEM_LEN, D_MODEL)
    mkv = _memkv(mem, mem_norm_w, xa_w_kv.astype(BF16))
    nw, w_all, cw, w2bd, glab, gparams, wout, n1, n2, n3 = _pack_params(
        norm_w, w_in, gla_w2, gla_b, gla_norm_w, gdn_conv_w, gdn_a_log, gdn_dt_bias, gdn_norm_w,
        xa_norm_w, w_out)
    h = x.reshape(BATCH * SEQ, D_MODEL)
    fnw = final_norm_w.reshape(1, D_MODEL)
    for l in range(DEPTH):
        gla, v1, dec, zg, q2, k2, v2, gcol, grow, xaq = _inproj(l, h, nw, w_all, cw, w2bd, glab, gparams)
        o1f, o1b = _gla_scan(gla, v1, dec)
        o2f, o2b = _gdn_scan(q2, k2, v2, gcol, grow)
        h = _outproj(l, o1f, o1b, o2f, o2b, zg, xaq, mkv, h, wout, n1, n2, n3, fnw)
    return h.reshape(BATCH, SEQ, D_MODEL)
```

```python
import functools

import jax
import jax.numpy as jnp
from jax import lax
from jax.experimental import pallas as pl
from jax.experimental.pallas import tpu as pltpu

F32 = jnp.float32
BF16 = jnp.bfloat16

D_MODEL = 1024
BATCH = 8
SEQ = 4096
DEPTH = 2
MEM_LEN = 256
CHUNK = 64
NORM_EPS = 1e-6
GLA_HEADS = 4
GLA_DK = 64
GLA_DV = 96
GLA_RANK = 16
GLA_GATE_NORMALIZER = 16.0
GDN_HEADS = 4
GDN_DK = 96
GDN_DV = 96
GDN_CONV = 5
XA_HEADS = 4
XA_DH = 64

LANES = 128
HEADS = 4
HW = HEADS * LANES
GLA_QK_W = GLA_HEADS * GLA_DK
XA_W = XA_HEADS * XA_DH
MIX_PAD_W = 2 * HW + XA_W

C_GQ = 0
C_GK = C_GQ + GLA_QK_W
C_GV = C_GK + GLA_QK_W
C_DQKV = C_GV + HW
C_Z = C_DQKV + 3 * HW
C_XQ = C_Z + MIX_PAD_W
C_MISC = C_XQ + XA_W
IN_PAD_W = C_MISC + LANES

M_BETA = 0
M_A = 8
M_A_COPIES = 4
M_LR = 64
G_BETA, G_GC, G_EGC, G_EKEND, G_DEC = 0, 8, 16, 24, 32
ROW_SLAB = 16

TM_IN = 512
TM_OUT = 512
NC = 8
GDN_PARTS = 2
TB = NC * CHUNK
HALO = 8
CONV_PHASES = 4
VMEM_LIMIT = 56 * 1024 * 1024


def _dot(a, b):
    return jnp.dot(a, b, preferred_element_type=F32)


def _dot_nt(a, b):
    return lax.dot_general(a, b, (((1,), (1,)), ((), ())), preferred_element_type=F32)


def _dot_tn(a, b):
    return lax.dot_general(a, b, (((0,), (0,)), ((), ())), preferred_element_type=F32)


def _sigmoid(x):
    return 1.0 / (1.0 + jnp.exp(-x))


def _silu(x):
    return x * _sigmoid(x)


def _softplus(x):
    return jnp.maximum(x, 0.0) + jnp.log(1.0 + jnp.exp(-jnp.abs(x)))


def _log_sigmoid(x):
    return jnp.minimum(x, 0.0) - jnp.log(1.0 + jnp.exp(-jnp.abs(x)))


def _split2(x):
    hi = pltpu.bitcast(pltpu.bitcast(x, jnp.int32) & jnp.int32(-65536), F32)
    return hi.astype(BF16), (x - hi).astype(BF16)


def _tri_sum(tri, x):
    hi, lo = _split2(x)
    return _dot(tri, hi) + _dot(tri, lo)


def _tri_masks():
    ri = lax.broadcasted_iota(jnp.int32, (CHUNK, CHUNK), 0)
    ci = lax.broadcasted_iota(jnp.int32, (CHUNK, CHUNK), 1)
    return ri, ci


def _memkv_kernel(mem_ref, nw_ref, w_ref, out_ref):
    m = mem_ref[0]
    ms = jnp.mean(m * m, axis=-1, keepdims=True)
    mn = (m * lax.rsqrt(ms + NORM_EPS) * nw_ref[0]).astype(BF16)
    out_ref[0, 0] = _dot(mn, w_ref[0]).astype(BF16)


def _memkv(mem, mem_norm_w, xa_w_kv_bf16):
    return pl.pallas_call(
        _memkv_kernel,
        grid=(DEPTH, BATCH),
        in_specs=[
            pl.BlockSpec((1, MEM_LEN, D_MODEL), lambda l, b: (b, 0, 0)),
            pl.BlockSpec((1, 1, D_MODEL), lambda l, b: (l, 0, 0)),
            pl.BlockSpec((1, D_MODEL, 2 * XA_W), lambda l, b: (l, 0, 0)),
        ],
        out_specs=pl.BlockSpec((1, 1, MEM_LEN, 2 * XA_W), lambda l, b: (l, b, 0, 0)),
        out_shape=jax.ShapeDtypeStruct((DEPTH, BATCH, MEM_LEN, 2 * XA_W), BF16),
        compiler_params=pltpu.CompilerParams(dimension_semantics=("parallel", "parallel")),
        name="memkv",
    )(mem, mem_norm_w.reshape(DEPTH, 1, D_MODEL), xa_w_kv_bf16)


def _inproj_kernel(x_ref, xp_ref, xn_ref, nw_ref, w_ref, cw_ref, w2_ref, gb_ref, gp_ref,
                   gla_ref, v1_ref, dec_ref, zg_ref, q2_ref, k2_ref, v2_ref, gcol_ref, grow_ref,
                   xaq_ref, ext_ref, conv_ref):
    nw = nw_ref[...]

    def norm(x):
        ms = jnp.mean(x * x, axis=-1, keepdims=True)
        return (x * lax.rsqrt(ms + NORM_EPS) * nw).astype(BF16)

    h = norm(x_ref[...])
    h_halo = norm(jnp.concatenate([xp_ref[...], xn_ref[...]], axis=0))

    def proj(hh, lo, width):
        return _dot(hh, w_ref[:, lo:lo + width])

    blocks_per_seq = SEQ // TM_IN
    j = lax.rem(pl.program_id(0), blocks_per_seq)
    cw = cw_ref[...]
    base = HALO - GDN_CONV // 2

    def gdn_mm(grp):
        lo = C_DQKV + grp * HW
        halo = proj(h_halo, lo, HW)
        return proj(h, lo, HW), halo[0:HALO], halo[HALO:]

    def gdn_vpu(grp, res):
        main, prev, nxt = res
        for hd in range(HEADS):
            slab = grp * HEADS + hd
            lanes = slice(hd * LANES, (hd + 1) * LANES)
            ext_ref[slab, 0:HALO, :] = jnp.where(j == 0, 0.0, prev[:, lanes])
            ext_ref[slab, HALO:HALO + TM_IN, :] = main[:, lanes]
            ext_ref[slab, HALO + TM_IN:2 * HALO + TM_IN, :] = jnp.where(j == blocks_per_seq - 1, 0.0,
                                                                     nxt[:, lanes])
            taps = cw[:, grp * HW + hd * LANES:grp * HW + (hd + 1) * LANES]
            for p in range(CONV_PHASES):
                acc = None
                for t in range(GDN_CONV):
                    win = ext_ref[slab, pl.ds(base + p + t, TM_IN // CONV_PHASES, stride=CONV_PHASES), :]
                    acc = win * taps[t:t + 1, :] if acc is None else acc + win * taps[t:t + 1, :]
                conv_ref[slab, pl.ds(p, TM_IN // CONV_PHASES, stride=CONV_PHASES), :] = acc
            y = _silu(conv_ref[slab])
            if grp == 2:
                v2_ref[:, lanes] = y.astype(BF16)
            else:
                ref, scale = ((q2_ref, GDN_DK ** -0.5), (k2_ref, 1.0))[grp]
                ss = jnp.sum(y * y, axis=-1, keepdims=True)
                ref[:, lanes] = (y * lax.rsqrt(ss + NORM_EPS) * scale).astype(BF16)

    def z_vpu(lo, width, res):
        zg_ref[:, lo:lo + width] = _silu(res).astype(BF16)

    def xaq_vpu(res):
        xaq_ref[...] = (res * (XA_DH ** -0.5)).astype(BF16)

    lane = lax.broadcasted_iota(jnp.int32, (1, LANES), 1)
    fwd_lane = lax.rem(lane, 8) < GDN_HEADS
    ri, ci = _tri_masks()
    lower = jnp.where(ri >= ci, 1.0, 0.0).astype(BF16)
    upper = jnp.where(ri <= ci, 1.0, 0.0).astype(BF16)
    lower_upper = jnp.concatenate([lower, upper], axis=0)
    gate = {}

    def misc_vpu(m):
        logits = _dot(m.astype(BF16), w2_ref[...]) + gb_ref[...]
        gate["g"] = _log_sigmoid(logits) * (1.0 / GLA_GATE_NORMALIZER)
        is_a = (lane >= M_A) & (lane < M_A + 8 * M_A_COPIES)
        neg_a = jnp.where(is_a, -jnp.exp(gp_ref[0:1, :]), 0.0)
        gg = neg_a * _softplus(m + gp_ref[1:2, :])
        beta = _sigmoid(m)
        for c in range(TM_IN // CHUNK):
            rows = slice(c * CHUNK, (c + 1) * CHUNK)
            ggc = gg[rows]
            hi, lo = _split2(ggc)
            both = _dot(lower_upper, jnp.concatenate([hi, lo], axis=1))
            pf = both[0:CHUNK, 0:LANES] + both[0:CHUNK, LANES:]
            sf = both[CHUNK:, 0:LANES] + both[CHUNK:, LANES:]
            gc = jnp.where(fwd_lane, pf, sf)
            last = jnp.where(fwd_lane, pf[CHUNK - 1:CHUNK], sf[0:1])
            col = jnp.where(lane < G_GC, beta[rows],
                  jnp.where(lane < G_EGC, gc,
                  jnp.where(lane < G_EKEND, jnp.exp(gc),
                  jnp.where(lane < G_DEC, jnp.exp(last - gc), jnp.exp(last)))))
            gcol_ref[rows, :] = col
            grow_ref[c] = col.T[0:ROW_SLAB, :]

    def gla_v_vpu(res):
        v1_ref[...] = res.astype(BF16)

    def gla_qk_vpu(pg):
        q1 = pg[:, 0:GLA_QK_W] * (GLA_DK ** -0.5)
        k1 = pg[:, GLA_QK_W:2 * GLA_QK_W]
        g = gate["g"]
        for c in range(TM_IN // CHUNK):
            rows = slice(c * CHUNK, (c + 1) * CHUNK)
            gch = g[rows]
            q1c = q1[rows]
            k1c = k1[rows]
            bf = _tri_sum(lower, gch[:, 0:GLA_QK_W])
            br = _tri_sum(upper, gch[:, GLA_QK_W:])
            for d, (b, last) in enumerate(((bf, bf[CHUNK - 1:CHUNK]), (br, br[0:1]))):
                off = d * 3 * GLA_QK_W
                gla_ref[rows, off:off + GLA_QK_W] = (q1c * jnp.exp(b)).astype(BF16)
                gla_ref[rows, off + GLA_QK_W:off + 2 * GLA_QK_W] = (k1c * jnp.exp(-b)).astype(BF16)
                gla_ref[rows, off + 2 * GLA_QK_W:off + 3 * GLA_QK_W] = (k1c * jnp.exp(last - b)).astype(BF16)
                dec_ref[c, :, d * GLA_QK_W:(d + 1) * GLA_QK_W] = jnp.exp(last)

    part = functools.partial
    tasks = [
        (part(proj, h, C_MISC, LANES), misc_vpu),
        (part(gdn_mm, 0), part(gdn_vpu, 0)),
        (part(proj, h, C_Z, HW), part(z_vpu, 0, HW)),
        (part(gdn_mm, 1), part(gdn_vpu, 1)),
        (part(proj, h, C_Z + HW, HW), part(z_vpu, HW, HW)),
        (part(gdn_mm, 2), part(gdn_vpu, 2)),
        (part(proj, h, C_Z + 2 * HW, XA_W), part(z_vpu, 2 * HW, XA_W)),
        (part(proj, h, C_GQ, 2 * GLA_QK_W), gla_qk_vpu),
        (part(proj, h, C_GV, HW), gla_v_vpu),
        (part(proj, h, C_XQ, XA_W), xaq_vpu),
    ]
    res = tasks[0][0]()
    for t, (_, vpu) in enumerate(tasks):
        nxt_res = tasks[t + 1][0]() if t + 1 < len(tasks) else None
        vpu(res)
        res = nxt_res


def _inproj(layer, x2d, norm_w, w_all, conv_w, w2bd, gla_b, gdn_params):
    n_tok = x2d.shape[0]
    nblk = n_tok // TM_IN
    halo_blocks = TM_IN // HALO
    n_halo = n_tok // HALO
    nchunk = TM_IN // CHUNK
    tok = lambda w: pl.BlockSpec((TM_IN, w), lambda i: (i, 0))
    const = lambda shape: pl.BlockSpec((None,) + shape, lambda i: (layer,) + tuple(0 for _ in shape))
    out_shapes = (
        jax.ShapeDtypeStruct((n_tok, 6 * GLA_QK_W), BF16),
        jax.ShapeDtypeStruct((n_tok, HW), BF16),
        jax.ShapeDtypeStruct((n_tok // CHUNK, 1, 2 * GLA_QK_W), F32),
        jax.ShapeDtypeStruct((n_tok, MIX_PAD_W), BF16),
        jax.ShapeDtypeStruct((n_tok, HW), BF16),
        jax.ShapeDtypeStruct((n_tok, HW), BF16),
        jax.ShapeDtypeStruct((n_tok, HW), BF16),
        jax.ShapeDtypeStruct((n_tok, LANES), F32),
        jax.ShapeDtypeStruct((n_tok // CHUNK, ROW_SLAB, CHUNK), F32),
        jax.ShapeDtypeStruct((n_tok, XA_W), BF16),
    )
    out_specs = (
        tok(6 * GLA_QK_W), tok(HW),
        pl.BlockSpec((nchunk, 1, 2 * GLA_QK_W), lambda i: (i, 0, 0)),
        tok(MIX_PAD_W), tok(HW), tok(HW), tok(HW), tok(LANES),
        pl.BlockSpec((nchunk, ROW_SLAB, CHUNK), lambda i: (i, 0, 0)),
        tok(XA_W),
    )
    return pl.pallas_call(
        _inproj_kernel,
        grid=(nblk,),
        in_specs=[
            tok(D_MODEL),
            pl.BlockSpec((HALO, D_MODEL), lambda i: (jnp.maximum(i * halo_blocks - 1, 0), 0)),
            pl.BlockSpec((HALO, D_MODEL), lambda i: (jnp.minimum((i + 1) * halo_blocks, n_halo - 1), 0)),
            const((1, D_MODEL)),
            const((D_MODEL, IN_PAD_W)),
            const((8, 3 * HW)),
            const((LANES, 2 * GLA_QK_W)),
            const((1, 2 * GLA_QK_W)),
            const((8, LANES)),
        ],
        out_specs=out_specs,
        out_shape=out_shapes,
        scratch_shapes=[pltpu.VMEM((3 * HEADS, TM_IN + 2 * HALO, LANES), F32),
                        pltpu.VMEM((3 * HEADS, TM_IN, LANES), F32)],
        compiler_params=pltpu.CompilerParams(dimension_semantics=("parallel",),
                                             vmem_limit_bytes=VMEM_LIMIT),
        name="inproj",
    )(x2d, x2d, x2d, norm_w, w_all, conv_w, w2bd, gla_b, gdn_params)


def _gla_kernel(gf_ref, gb_ref, vf_ref, vb_ref, decf_ref, decb_ref, of_ref, ob_ref, s_ref):
    @pl.when(pl.program_id(1) == 0)
    def _():
        s_ref[...] = jnp.zeros_like(s_ref)

    ri = lax.broadcasted_iota(jnp.int32, (CHUNK, LANES), 0)
    lane_tok = lax.broadcasted_iota(jnp.int32, (CHUNK, LANES), 1)
    ci = lane_tok & (CHUNK - 1)
    incl = (ri >= ci, ri <= ci)
    left = lane_tok < GLA_DK
    lane_sq = lax.broadcasted_iota(jnp.int32, (LANES, LANES), 1) < GLA_DK
    zero_bf = jnp.zeros((CHUNK, LANES), BF16)
    cat = jnp.concatenate

    def split_heads(x):
        return cat([jnp.where(left, x, zero_bf), jnp.where(left, zero_bf, x)], axis=0)

    g_refs, v_refs = (gf_ref, gb_ref), (vf_ref, vb_ref)
    dec_refs, o_refs = (decf_ref, decb_ref), (of_ref, ob_ref)
    chunk_of = lambda d, step: step if d == 0 else NC - 1 - step
    units = [(d, step, p) for step in range(NC) for d in range(2) for p in range(GLA_HEADS // 2)]

    qe, kend, v, a = {}, {}, {}, {}
    for u in units:
        d, step, p = u
        c = chunk_of(d, step)
        rows = slice(c * CHUNK, (c + 1) * CHUNK)
        qe[u] = g_refs[d][rows, p * LANES:(p + 1) * LANES]
        ke = g_refs[d][rows, GLA_QK_W + p * LANES:GLA_QK_W + (p + 1) * LANES]
        kend[u] = g_refs[d][rows, 2 * GLA_QK_W + p * LANES:2 * GLA_QK_W + (p + 1) * LANES]
        v[u] = (v_refs[d][rows, 2 * p * LANES:(2 * p + 1) * LANES],
                v_refs[d][rows, (2 * p + 1) * LANES:(2 * p + 2) * LANES])
        a[u] = jnp.where(incl[d], _dot_nt(qe[u], split_heads(ke)), 0.0).astype(BF16)
    intra = {u: _dot(a[u], cat([cat([v[u][0], zero_bf], axis=1), cat([zero_bf, v[u][1]], axis=1)], axis=0))
             for u in units}
    kv = {u: _dot_tn(cat([v[u][0], v[u][1]], axis=1), kend[u]) for u in units}

    for d in range(2):
        for p in range(GLA_HEADS // 2):
            st = s_ref[d, p]
            for step in range(NC):
                u = (d, step, p)
                c = chunk_of(d, step)
                inter = _dot_nt(split_heads(qe[u]), st.astype(BF16))
                for hh in range(2):
                    hd = 2 * p + hh
                    o = intra[u][:, hh * LANES:(hh + 1) * LANES] + inter[hh * CHUNK:(hh + 1) * CHUNK]
                    o_refs[d][c * CHUNK:(c + 1) * CHUNK, hd * LANES:(hd + 1) * LANES] = o.astype(o_refs[d].dtype)
                dec = dec_refs[d][c, :, p * LANES:(p + 1) * LANES]
                st = dec * st + jnp.where(lane_sq, kv[u][0:LANES], kv[u][LANES:])
            s_ref[d, p] = st


def _gla_scan(gla, v1, dec):
    nb = SEQ // TB
    fwd = lambda b, i: (b * nb + i, 0)
    bwd = lambda b, i: (b * nb + nb - 1 - i, 0)
    n_tok = gla.shape[0]
    return pl.pallas_call(
        _gla_kernel,
        grid=(BATCH, nb),
        in_specs=[
            pl.BlockSpec((TB, 3 * GLA_QK_W), fwd),
            pl.BlockSpec((TB, 3 * GLA_QK_W), lambda b, i: (b * nb + nb - 1 - i, 1)),
            pl.BlockSpec((TB, HW), fwd),
            pl.BlockSpec((TB, HW), bwd),
            pl.BlockSpec((NC, 1, GLA_QK_W), lambda b, i: (b * nb + i, 0, 0)),
            pl.BlockSpec((NC, 1, GLA_QK_W), lambda b, i: (b * nb + nb - 1 - i, 0, 1)),
        ],
        out_specs=(pl.BlockSpec((TB, HW), fwd), pl.BlockSpec((TB, HW), bwd)),
        out_shape=(jax.ShapeDtypeStruct((n_tok, HW), BF16), jax.ShapeDtypeStruct((n_tok, HW), BF16)),
        scratch_shapes=[pltpu.VMEM((2, GLA_HEADS // 2, LANES, LANES), F32)],
        compiler_params=pltpu.CompilerParams(dimension_semantics=("parallel", "arbitrary"),
                                             vmem_limit_bytes=VMEM_LIMIT),
        name="gla_scan",
    )(gla, gla, v1, v1, dec, dec)


def _gdn_kernel(qf_ref, kf_ref, vf_ref, colf_ref, rowf_ref, qb_ref, kb_ref, vb_ref, colb_ref, rowb_ref,
                of_ref, ob_ref, s_ref):
    @pl.when(pl.program_id(1) == 0)
    def _():
        s_ref[...] = jnp.zeros_like(s_ref)

    pk = GDN_HEADS * CHUNK
    ri = lax.broadcasted_iota(jnp.int32, (CHUNK, pk), 0)
    ci = lax.broadcasted_iota(jnp.int32, (CHUNK, pk), 1) & (CHUNK - 1)
    incl = (ri >= ci, ri <= ci)
    strict = (ri > ci, ri < ci)
    eye = jnp.where(ri == ci, 1.0, 0.0)
    blk = lambda axis: lax.shift_right_logical(lax.broadcasted_iota(jnp.int32, (pk, pk), axis), 6)
    same_blk = blk(0) == blk(1)
    bd_mask = jnp.where(same_blk, 1.0, 0.0).astype(BF16)
    left = lax.broadcasted_iota(jnp.int32, (CHUNK, LANES), 1) < CHUNK
    zero_tok = jnp.zeros((CHUNK, LANES), BF16)
    zero_sq = jnp.zeros((LANES, LANES), BF16)

    refs = ((qf_ref, kf_ref, vf_ref, colf_ref, rowf_ref, of_ref),
            (qb_ref, kb_ref, vb_ref, colb_ref, rowb_ref, ob_ref))
    groups = [(d, step) for step in range(NC) for d in range(2)]
    pairs = range(GDN_HEADS // 2)
    each = lambda fn, *lists: [fn(*args) for args in zip(*lists)]
    top = lambda m: m[0:CHUNK]
    bot = lambda m: m[CHUNK:2 * CHUNK]
    cat = jnp.concatenate
    chunk_of = lambda d, step: step if d == 0 else NC - 1 - step

    def block_diag(x):
        return cat([x, x, x, x], axis=0) * bd_mask

    def diag2(a, b, zero):
        return cat([cat([a, zero], axis=1), cat([zero, b], axis=1)], axis=0)

    q, k, v, col, gc_r = [], [], [], [], []
    for d, step in groups:
        q_ref, k_ref, v_ref, col_ref, row_ref, _ = refs[d]
        c = chunk_of(d, step)
        rows = slice(c * CHUNK, (c + 1) * CHUNK)
        heads = [slice(hd * LANES, (hd + 1) * LANES) for hd in range(GDN_HEADS)]
        q.append([q_ref[rows, h] for h in heads])
        k.append([k_ref[rows, h] for h in heads])
        v.append([v_ref[rows, h] for h in heads])
        col.append(col_ref[rows, :])
        rowt = row_ref[c]
        gc_r.append(cat([rowt[8 + d * GDN_HEADS + hd:9 + d * GDN_HEADS + hd, :] for hd in range(GDN_HEADS)],
                        axis=1))

    def colv(g, base, hd):
        lane = base + groups[g][0] * GDN_HEADS + hd
        return col[g][:, lane:lane + 1]

    def col_packed(g, base):
        tiles = [jnp.where(left, jnp.broadcast_to(colv(g, base, 2 * j), (CHUNK, LANES)),
                           jnp.broadcast_to(colv(g, base, 2 * j + 1), (CHUNK, LANES))) for j in pairs]
        return cat(tiles, axis=1)

    qk, uw = {}, {}

    def rhs_of(g, hd):
        beta = colv(g, G_BETA, hd)
        return cat([(v[g][hd].astype(F32) * beta).astype(BF16),
                    (k[g][hd].astype(F32) * (beta * colv(g, G_EGC, hd))).astype(BF16)], axis=1)

    def pre(gl):
        kq = {g: [_dot_nt(cat([cat([k[g][2 * j], k[g][2 * j + 1]], axis=1),
                               cat([q[g][2 * j], q[g][2 * j + 1]], axis=1)], axis=0),
                          diag2(k[g][2 * j], k[g][2 * j + 1], zero_tok)) for j in pairs] for g in gl}
        yield
        pm, cur = {}, {}
        for g in gl:
            d = groups[g][0]
            kk = cat([top(kq[g][j]) for j in pairs], axis=1)
            qkr = cat([bot(kq[g][j]) for j in pairs], axis=1)
            decay = jnp.where(incl[d], jnp.exp(jnp.where(incl[d], col_packed(g, G_GC) - gc_r[g], 0.0)), 0.0)
            n = jnp.where(strict[d], -(kk * decay * col_packed(g, G_BETA)), 0.0)
            qk[g] = jnp.where(incl[d], qkr * decay, 0.0).astype(BF16)
            pm[g] = eye + n
            cur[g] = _dot(n.astype(BF16), block_diag(n.astype(BF16)))
        yield
        for _ in range(4):
            st = {g: _dot(cat([pm[g].astype(BF16), cur[g].astype(BF16)], axis=0),
                          block_diag(cur[g].astype(BF16))) for g in gl}
            for g in gl:
                pm[g] = pm[g] + top(st[g])
                cur[g] = bot(st[g])
            yield
        tinv = {g: pm[g] + _dot(pm[g].astype(BF16), block_diag(cur[g].astype(BF16))) for g in gl}
        yield
        for g in gl:
            uw[g] = []
            for j in pairs:
                t = tinv[g][:, j * LANES:(j + 1) * LANES]
                lhs = cat([jnp.where(left, t, 0.0), jnp.where(left, 0.0, t)], axis=0).astype(BF16)
                uw[g].append(_dot(lhs, cat([rhs_of(g, 2 * j), rhs_of(g, 2 * j + 1)], axis=0)))
        yield

    def scan(step):
        gs = [g for g, (_, st_) in enumerate(groups) if st_ == step]
        units = [(g, j) for g in gs for j in pairs]
        s = {(g, hd): s_ref[groups[g][0], hd] for g in gs for hd in range(GDN_HEADS)}
        r = {}
        for g, j in units:
            a, b = 2 * j, 2 * j + 1
            x = uw[g][j]
            lhs = cat([cat([x[0:CHUNK, LANES:].astype(BF16), x[CHUNK:, LANES:].astype(BF16)], axis=1),
                       cat([q[g][a], q[g][b]], axis=1)], axis=0)
            r[(g, j)] = _dot(lhs, diag2(s[(g, a)].astype(BF16), s[(g, b)].astype(BF16), zero_sq))
        yield
        vn = {}
        for g, j in units:
            x, rr = uw[g][j], r[(g, j)]
            vn[(g, 2 * j)] = x[0:CHUNK, 0:LANES] - rr[0:CHUNK, 0:LANES]
            vn[(g, 2 * j + 1)] = x[CHUNK:, 0:LANES] - rr[0:CHUNK, LANES:]
        for g, j in units:
            d = groups[g][0]
            a, b = 2 * j, 2 * j + 1
            intra = _dot(qk[g][:, j * LANES:(j + 1) * LANES],
                         diag2(vn[(g, a)].astype(BF16), vn[(g, b)].astype(BF16), zero_tok))
            c = chunk_of(d, step)
            for hd, lanes in ((a, slice(0, LANES)), (b, slice(LANES, 2 * LANES))):
                o = colv(g, G_EGC, hd) * r[(g, j)][CHUNK:, lanes] + intra[:, lanes]
                o_ref = refs[d][5]
                o_ref[c * CHUNK:(c + 1) * CHUNK, hd * LANES:(hd + 1) * LANES] = o.astype(o_ref.dtype)
        for g in gs:
            d = groups[g][0]
            for hd in range(GDN_HEADS):
                dec = col[g][0:1, G_DEC + d * GDN_HEADS + hd:G_DEC + d * GDN_HEADS + hd + 1]
                kv = _dot_tn(k[g][hd], (colv(g, G_EKEND, hd) * vn[(g, hd)]).astype(BF16))
                s_ref[d, hd] = dec * s[(g, hd)] + kv
        yield

    part = NC // GDN_PARTS
    groups_of = lambda p: [g for g, (_, st_) in enumerate(groups) if p * part <= st_ < (p + 1) * part]
    pre_stages = 8
    for _ in pre(groups_of(0)):
        pass
    for p in range(GDN_PARTS):
        later = pre(groups_of(p + 1)) if p + 1 < GDN_PARTS else iter(())
        scan_stages = [stage for step in range(p * part, (p + 1) * part) for stage in (scan(step),)]
        per_scan_stage = pre_stages // (2 * part)
        for sc in scan_stages:
            for _ in sc:
                for _ in range(per_scan_stage):
                    next(later, None)
        for _ in later:
            pass


def _gdn_scan(q2, k2, v2, gcol, grow):
    nb = SEQ // TB
    n_tok = q2.shape[0]
    fwd = lambda b, i: (b * nb + i, 0)
    bwd = lambda b, i: (b * nb + nb - 1 - i, 0)
    fwd3 = lambda b, i: (b * nb + i, 0, 0)
    bwd3 = lambda b, i: (b * nb + nb - 1 - i, 0, 0)

    def specs(m2, m3):
        return [pl.BlockSpec((TB, HW), m2), pl.BlockSpec((TB, HW), m2), pl.BlockSpec((TB, HW), m2),
                pl.BlockSpec((TB, LANES), m2), pl.BlockSpec((NC, ROW_SLAB, CHUNK), m3)]

    return pl.pallas_call(
        _gdn_kernel,
        grid=(BATCH, nb),
        in_specs=specs(fwd, fwd3) + specs(bwd, bwd3),
        out_specs=(pl.BlockSpec((TB, HW), fwd), pl.BlockSpec((TB, HW), bwd)),
        out_shape=(jax.ShapeDtypeStruct((n_tok, HW), BF16), jax.ShapeDtypeStruct((n_tok, HW), BF16)),
        scratch_shapes=[pltpu.VMEM((2, GDN_HEADS, LANES, LANES), F32)],
        compiler_params=pltpu.CompilerParams(dimension_semantics=("parallel", "arbitrary"),
                                             vmem_limit_bytes=VMEM_LIMIT),
        name="gdn_scan",
    )(q2, k2, v2, gcol, grow, q2, k2, v2, gcol, grow)


def _outproj_kernel(o1f_ref, o1b_ref, o2f_ref, o2b_ref, zg_ref, xaq_ref, mkv_ref, x_ref, wout_ref,
                    n1_ref, n2_ref, n3_ref, fn_ref, out_ref, *, final):
    def head_norm(o, nw_ref, width):
        parts = []
        for hd in range(HEADS):
            s = o[:, hd * LANES:(hd + 1) * LANES]
            ms = jnp.sum(s * s, axis=-1, keepdims=True) * (1.0 / width)
            parts.append(s * lax.rsqrt(ms + NORM_EPS))
        return jnp.concatenate(parts, axis=-1) * nw_ref[...]

    both = lambda f_ref, b_ref: f_ref[...].astype(F32) + b_ref[...].astype(F32)
    gate = lambda lo, width: zg_ref[:, lo:lo + width].astype(F32)
    lane_q = lax.broadcasted_iota(jnp.int32, (TM_OUT, LANES), 1)
    lane_m = lax.broadcasted_iota(jnp.int32, (MEM_LEN, LANES), 1)
    first_q = lane_q < XA_DH
    first_m = lane_m < XA_DH
    q_head = (first_q, lane_q >= XA_DH)
    m_head = (first_m, lane_m >= XA_DH)
    heads = [(p, hh) for p in range(XA_HEADS // 2) for hh in range(2)]

    sc = []
    for p, hh in heads:
        qpair = xaq_ref[:, p * LANES:(p + 1) * LANES]
        qm = jnp.where(q_head[hh], qpair, jnp.zeros_like(qpair))
        sc.append(_dot_nt(qm, mkv_ref[0, :, p * LANES:(p + 1) * LANES]))

    o1 = head_norm(both(o1f_ref, o1b_ref), n1_ref, GLA_DV) * gate(0, HW)
    y = _dot(o1.astype(BF16), wout_ref[0:HW, :])

    pv = []
    for (p, hh), s in zip(heads, sc):
        e = jnp.exp(s - jnp.max(s, axis=-1, keepdims=True))
        l = jnp.sum(e, axis=-1, keepdims=True)
        mv = mkv_ref[0, :, XA_W + p * LANES:XA_W + (p + 1) * LANES]
        mvm = jnp.where(m_head[hh], mv, jnp.zeros_like(mv))
        pv.append(_dot(e.astype(BF16), mvm) * (1.0 / l))

    o2 = head_norm(both(o2f_ref, o2b_ref), n2_ref, GDN_DV) * gate(HW, HW)
    y = y + _dot(o2.astype(BF16), wout_ref[HW:2 * HW, :])

    for p in range(XA_HEADS // 2):
        lanes = slice(p * LANES, (p + 1) * LANES)
        acc = pv[2 * p] + pv[2 * p + 1]
        sq = acc * acc
        ss0 = jnp.sum(jnp.where(first_q, sq, 0.0), axis=-1, keepdims=True)
        ss1 = jnp.sum(jnp.where(first_q, 0.0, sq), axis=-1, keepdims=True)
        ms = jnp.where(first_q, ss0, ss1) * (1.0 / XA_DH)
        o3 = acc * lax.rsqrt(ms + NORM_EPS) * n3_ref[:, lanes] * gate(2 * HW + p * LANES, LANES)
        y = y + _dot(o3.astype(BF16), wout_ref[2 * HW + p * LANES:2 * HW + (p + 1) * LANES, :])

    xo = x_ref[...] + y
    if final:
        ms = jnp.mean(xo * xo, axis=-1, keepdims=True)
        xo = xo * lax.rsqrt(ms + NORM_EPS) * fn_ref[...]
    out_ref[...] = xo


def _outproj(layer, o1f, o1b, o2f, o2b, zg, xaq, mkv, x2d, wout, n1, n2, n3, fnw):
    n_tok = x2d.shape[0]
    blocks_per_seq = SEQ // TM_OUT
    tok = lambda w: pl.BlockSpec((TM_OUT, w), lambda i: (i, 0))
    const = lambda shape: pl.BlockSpec((None,) + shape, lambda i: (layer,) + tuple(0 for _ in shape))
    return pl.pallas_call(
        functools.partial(_outproj_kernel, final=layer == DEPTH - 1),
        grid=(n_tok // TM_OUT,),
        in_specs=[
            tok(HW), tok(HW), tok(HW), tok(HW), tok(MIX_PAD_W), tok(XA_W),
            pl.BlockSpec((None, 1, MEM_LEN, 2 * XA_W), lambda i: (layer, i // blocks_per_seq, 0, 0)),
            tok(D_MODEL),
            const((MIX_PAD_W, D_MODEL)),
            const((1, HW)), const((1, HW)), const((1, XA_W)),
            pl.BlockSpec((1, D_MODEL), lambda i: (0, 0)),
        ],
        out_specs=tok(D_MODEL),
        out_shape=jax.ShapeDtypeStruct((n_tok, D_MODEL), F32),
        compiler_params=pltpu.CompilerParams(dimension_semantics=("parallel",),
                                             vmem_limit_bytes=VMEM_LIMIT),
        name="outproj",
    )(o1f, o1b, o2f, o2b, zg, xaq, mkv, x2d, wout, n1, n2, n3, fnw)


def _pad_heads(w, axis):
    shp = w.shape
    w = w.reshape(shp[:axis] + (HEADS, GLA_DV) + shp[axis + 1:])
    pad = [(0, 0)] * w.ndim
    pad[axis + 1] = (0, LANES - GLA_DV)
    w = jnp.pad(w, pad)
    return w.reshape(shp[:axis] + (HW,) + shp[axis + 1:])


def _pack_params(norm_w, w_in, gla_w2, gla_b, gla_norm_w, gdn_conv_w, gdn_a_log, gdn_dt_bias, gdn_norm_w,
                 xa_norm_w, w_out):
    sizes = (GLA_QK_W, GLA_QK_W, HEADS * GLA_DV, HEADS * GLA_DV, 2 * GLA_RANK, 3 * HEADS * GDN_DK,
             HEADS * GDN_DV, 2 * GDN_HEADS, 2 * GDN_HEADS, XA_W, XA_W)
    cols, start = [], 0
    for s in sizes:
        cols.append(w_in[:, :, start:start + s])
        start += s
    gq, gk, gv, gz, glr, dqkv, dz, db, da, xq, xz = cols
    hd = HEADS * GDN_DK
    zeros = lambda *shape: jnp.zeros((DEPTH,) + shape, F32)
    misc = jnp.concatenate(
        [db] + [da] * M_A_COPIES
        + [zeros(D_MODEL, M_LR - M_A - 8 * M_A_COPIES), glr, zeros(D_MODEL, LANES - M_LR - 2 * GLA_RANK)],
        axis=2)
    w_all = jnp.concatenate(
        [gq, gk, _pad_heads(gv, 2),
         _pad_heads(dqkv[:, :, 0:hd], 2), _pad_heads(dqkv[:, :, hd:2 * hd], 2), _pad_heads(dqkv[:, :, 2 * hd:], 2),
         _pad_heads(gz, 2), _pad_heads(dz, 2), xz, xq, misc], axis=2).astype(BF16)

    cw = jnp.transpose(gdn_conv_w, (0, 2, 1))
    cw = jnp.concatenate([_pad_heads(cw[:, :, 0:hd], 2), _pad_heads(cw[:, :, hd:2 * hd], 2),
                          _pad_heads(cw[:, :, 2 * hd:], 2)], axis=2)
    cw = jnp.pad(cw, ((0, 0), (0, 8 - GDN_CONV), (0, 0)))

    w2bd = jnp.concatenate(
        [zeros(M_LR, 2 * GLA_QK_W),
         jnp.concatenate([gla_w2[:, 0], zeros(GLA_RANK, GLA_QK_W)], axis=2),
         jnp.concatenate([zeros(GLA_RANK, GLA_QK_W), gla_w2[:, 1]], axis=2),
         zeros(LANES - M_LR - 2 * GLA_RANK, 2 * GLA_QK_W)], axis=1).astype(BF16)
    glab = gla_b.reshape(DEPTH, 1, 2 * GLA_QK_W)

    def a_slab(p):
        flat = p.reshape(DEPTH, 1, 2 * GDN_HEADS)
        return jnp.concatenate([zeros(1, M_A)] + [flat] * M_A_COPIES + [zeros(1, LANES - M_A - 8 * M_A_COPIES)],
                               axis=2)

    gparams = jnp.concatenate([a_slab(gdn_a_log), a_slab(gdn_dt_bias), zeros(6, LANES)], axis=1)

    wout = jnp.concatenate([_pad_heads(w_out[:, 0:HEADS * GLA_DV], 1),
                            _pad_heads(w_out[:, HEADS * GLA_DV:HEADS * (GLA_DV + GDN_DV)], 1),
                            w_out[:, HEADS * (GLA_DV + GDN_DV):]], axis=1).astype(BF16)
    pad_norm = lambda w: jnp.tile(jnp.pad(w, ((0, 0), (0, LANES - w.shape[1]))), (1, HEADS)).reshape(DEPTH, 1, HW)
    n1 = pad_norm(gla_norm_w)
    n2 = pad_norm(gdn_norm_w)
    n3 = jnp.tile(xa_norm_w, (1, XA_HEADS)).reshape(DEPTH, 1, XA_W)
    return norm_w.reshape(DEPTH, 1, D_MODEL), w_all, cw, w2bd, glab, gparams, wout, n1, n2, n3


def kernel(x, mem, norm_w, w_in, gla_w2, gla_b, gla_norm_w, gdn_conv_w, gdn_a_log, gdn_dt_bias,
           gdn_norm_w, mem_norm_w, xa_w_kv, xa_norm_w, w_out, final_norm_w):
    assert x.shape == (BATCH, SEQ, D_MODEL) and mem.shape == (BATCH, MEM_LEN, D_MODEL)
    mkv = _memkv(mem, mem_norm_w, xa_w_kv.astype(BF16))
    nw, w_all, cw, w2bd, glab, gparams, wout, n1, n2, n3 = _pack_params(
        norm_w, w_in, gla_w2, gla_b, gla_norm_w, gdn_conv_w, gdn_a_log, gdn_dt_bias, gdn_norm_w,
        xa_norm_w, w_out)
    h = x.reshape(BATCH * SEQ, D_MODEL)
    fnw = final_norm_w.reshape(1, D_MODEL)
    for l in range(DEPTH):
        gla, v1, dec, zg, q2, k2, v2, gcol, grow, xaq = _inproj(l, h, nw, w_all, cw, w2bd, glab, gparams)
        o1f, o1b = _gla_scan(gla, v1, dec)
        o2f, o2b = _gdn_scan(q2, k2, v2, gcol, grow)
        h = _outproj(l, o1f, o1b, o2f, o2b, zg, xaq, mkv, h, wout, n1, n2, n3, fnw)
    return h.reshape(BATCH, SEQ, D_MODEL)
```

```python
import functools

import jax
import jax.numpy as jnp
from jax import lax
from jax.experimental import pallas as pl
from jax.experimental.pallas import tpu as pltpu

F32 = jnp.float32
BF16 = jnp.bfloat16

D_MODEL = 1024
BATCH = 8
SEQ = 4096
DEPTH = 2
MEM_LEN = 256
CHUNK = 64
NORM_EPS = 1e-6
GLA_HEADS = 4
GLA_DK = 64
GLA_DV = 96
GLA_RANK = 16
GLA_GATE_NORMALIZER = 16.0
GDN_HEADS = 4
GDN_DK = 96
GDN_DV = 96
GDN_CONV = 5
XA_HEADS = 4
XA_DH = 64

LANES = 128
HEADS = 4
HW = HEADS * LANES
GLA_QK_W = GLA_HEADS * GLA_DK
XA_W = XA_HEADS * XA_DH
MIX_PAD_W = 2 * HW + XA_W

C_GQ = 0
C_GK = C_GQ + GLA_QK_W
C_GV = C_GK + GLA_QK_W
C_DQKV = C_GV + HW
C_Z = C_DQKV + 3 * HW
C_XQ = C_Z + MIX_PAD_W
C_MISC = C_XQ + XA_W
IN_PAD_W = C_MISC + LANES

M_BETA = 0
M_A = 8
M_A_COPIES = 4
M_LR = 64
G_BETA, G_GC, G_EGC, G_EKEND, G_DEC = 0, 8, 16, 24, 32
ROW_SLAB = 16

TM_IN = 512
TM_OUT = 512
NC = 8
TB = NC * CHUNK
HALO = 8
CONV_PHASES = 4
VMEM_LIMIT = 56 * 1024 * 1024


def _dot(a, b):
    return jnp.dot(a, b, preferred_element_type=F32)


def _dot_nt(a, b):
    return lax.dot_general(a, b, (((1,), (1,)), ((), ())), preferred_element_type=F32)


def _dot_tn(a, b):
    return lax.dot_general(a, b, (((0,), (0,)), ((), ())), preferred_element_type=F32)


def _sigmoid(x):
    return 1.0 / (1.0 + jnp.exp(-x))


def _silu(x):
    return x * _sigmoid(x)


def _softplus(x):
    return jnp.maximum(x, 0.0) + jnp.log(1.0 + jnp.exp(-jnp.abs(x)))


def _log_sigmoid(x):
    return jnp.minimum(x, 0.0) - jnp.log(1.0 + jnp.exp(-jnp.abs(x)))


def _split2(x):
    hi = pltpu.bitcast(pltpu.bitcast(x, jnp.int32) & jnp.int32(-65536), F32)
    return hi.astype(BF16), (x - hi).astype(BF16)


def _tri_sum(tri, x):
    hi, lo = _split2(x)
    return _dot(tri, hi) + _dot(tri, lo)


def _tri_masks():
    ri = lax.broadcasted_iota(jnp.int32, (CHUNK, CHUNK), 0)
    ci = lax.broadcasted_iota(jnp.int32, (CHUNK, CHUNK), 1)
    return ri, ci


def _memkv_kernel(mem_ref, nw_ref, w_ref, out_ref):
    m = mem_ref[0]
    ms = jnp.mean(m * m, axis=-1, keepdims=True)
    mn = (m * lax.rsqrt(ms + NORM_EPS) * nw_ref[0]).astype(BF16)
    out_ref[0, 0] = _dot(mn, w_ref[0]).astype(BF16)


def _memkv(mem, mem_norm_w, xa_w_kv_bf16):
    return pl.pallas_call(
        _memkv_kernel,
        grid=(DEPTH, BATCH),
        in_specs=[
            pl.BlockSpec((1, MEM_LEN, D_MODEL), lambda l, b: (b, 0, 0)),
            pl.BlockSpec((1, 1, D_MODEL), lambda l, b: (l, 0, 0)),
            pl.BlockSpec((1, D_MODEL, 2 * XA_W), lambda l, b: (l, 0, 0)),
        ],
        out_specs=pl.BlockSpec((1, 1, MEM_LEN, 2 * XA_W), lambda l, b: (l, b, 0, 0)),
        out_shape=jax.ShapeDtypeStruct((DEPTH, BATCH, MEM_LEN, 2 * XA_W), BF16),
        compiler_params=pltpu.CompilerParams(dimension_semantics=("parallel", "parallel")),
        name="memkv",
    )(mem, mem_norm_w.reshape(DEPTH, 1, D_MODEL), xa_w_kv_bf16)


def _inproj_kernel(x_ref, xp_ref, xn_ref, nw_ref, w_ref, cw_ref, w2_ref, gb_ref, gp_ref,
                   gla_ref, v1_ref, dec_ref, zg_ref, q2_ref, k2_ref, v2_ref, gcol_ref, grow_ref,
                   xaq_ref, ext_ref, conv_ref):
    nw = nw_ref[...]

    def norm(x):
        ms = jnp.mean(x * x, axis=-1, keepdims=True)
        return (x * lax.rsqrt(ms + NORM_EPS) * nw).astype(BF16)

    h = norm(x_ref[...])
    h_halo = norm(jnp.concatenate([xp_ref[...], xn_ref[...]], axis=0))

    def proj(hh, lo, width):
        return _dot(hh, w_ref[:, lo:lo + width])

    blocks_per_seq = SEQ // TM_IN
    j = lax.rem(pl.program_id(0), blocks_per_seq)
    cw = cw_ref[...]
    base = HALO - GDN_CONV // 2

    def gdn_mm(grp):
        lo = C_DQKV + grp * HW
        halo = proj(h_halo, lo, HW)
        return proj(h, lo, HW), halo[0:HALO], halo[HALO:]

    def gdn_vpu(grp, res):
        main, prev, nxt = res
        for hd in range(HEADS):
            slab = grp * HEADS + hd
            lanes = slice(hd * LANES, (hd + 1) * LANES)
            ext_ref[slab, 0:HALO, :] = jnp.where(j == 0, 0.0, prev[:, lanes])
            ext_ref[slab, HALO:HALO + TM_IN, :] = main[:, lanes]
            ext_ref[slab, HALO + TM_IN:2 * HALO + TM_IN, :] = jnp.where(j == blocks_per_seq - 1, 0.0,
                                                                     nxt[:, lanes])
            taps = cw[:, grp * HW + hd * LANES:grp * HW + (hd + 1) * LANES]
            for p in range(CONV_PHASES):
                acc = None
                for t in range(GDN_CONV):
                    win = ext_ref[slab, pl.ds(base + p + t, TM_IN // CONV_PHASES, stride=CONV_PHASES), :]
                    acc = win * taps[t:t + 1, :] if acc is None else acc + win * taps[t:t + 1, :]
                conv_ref[slab, pl.ds(p, TM_IN // CONV_PHASES, stride=CONV_PHASES), :] = acc
            y = _silu(conv_ref[slab])
            if grp == 2:
                v2_ref[:, lanes] = y.astype(BF16)
            else:
                ref, scale = ((q2_ref, GDN_DK ** -0.5), (k2_ref, 1.0))[grp]
                ss = jnp.sum(y * y, axis=-1, keepdims=True)
                ref[:, lanes] = (y * lax.rsqrt(ss + NORM_EPS) * scale).astype(BF16)

    def z_vpu(lo, width, res):
        zg_ref[:, lo:lo + width] = _silu(res).astype(BF16)

    def xaq_vpu(res):
        xaq_ref[...] = (res * (XA_DH ** -0.5)).astype(BF16)

    lane = lax.broadcasted_iota(jnp.int32, (1, LANES), 1)
    fwd_lane = lax.rem(lane, 8) < GDN_HEADS
    ri, ci = _tri_masks()
    lower = jnp.where(ri >= ci, 1.0, 0.0).astype(BF16)
    upper = jnp.where(ri <= ci, 1.0, 0.0).astype(BF16)
    lower_upper = jnp.concatenate([lower, upper], axis=0)
    gate = {}

    def misc_vpu(m):
        logits = _dot(m.astype(BF16), w2_ref[...]) + gb_ref[...]
        gate["g"] = _log_sigmoid(logits) * (1.0 / GLA_GATE_NORMALIZER)
        is_a = (lane >= M_A) & (lane < M_A + 8 * M_A_COPIES)
        neg_a = jnp.where(is_a, -jnp.exp(gp_ref[0:1, :]), 0.0)
        gg = neg_a * _softplus(m + gp_ref[1:2, :])
        beta = _sigmoid(m)
        for c in range(TM_IN // CHUNK):
            rows = slice(c * CHUNK, (c + 1) * CHUNK)
            ggc = gg[rows]
            hi, lo = _split2(ggc)
            both = _dot(lower_upper, jnp.concatenate([hi, lo], axis=1))
            pf = both[0:CHUNK, 0:LANES] + both[0:CHUNK, LANES:]
            sf = both[CHUNK:, 0:LANES] + both[CHUNK:, LANES:]
            gc = jnp.where(fwd_lane, pf, sf)
            last = jnp.where(fwd_lane, pf[CHUNK - 1:CHUNK], sf[0:1])
            col = jnp.where(lane < G_GC, beta[rows],
                  jnp.where(lane < G_EGC, gc,
                  jnp.where(lane < G_EKEND, jnp.exp(gc),
                  jnp.where(lane < G_DEC, jnp.exp(last - gc), jnp.exp(last)))))
            gcol_ref[rows, :] = col
            grow_ref[c] = col.T[0:ROW_SLAB, :]

    def gla_v_vpu(res):
        v1_ref[...] = res.astype(BF16)

    def gla_qk_vpu(pg):
        q1 = pg[:, 0:GLA_QK_W] * (GLA_DK ** -0.5)
        k1 = pg[:, GLA_QK_W:2 * GLA_QK_W]
        g = gate["g"]
        for c in range(TM_IN // CHUNK):
            rows = slice(c * CHUNK, (c + 1) * CHUNK)
            gch = g[rows]
            q1c = q1[rows]
            k1c = k1[rows]
            bf = _tri_sum(lower, gch[:, 0:GLA_QK_W])
            br = _tri_sum(upper, gch[:, GLA_QK_W:])
            for d, (b, last) in enumerate(((bf, bf[CHUNK - 1:CHUNK]), (br, br[0:1]))):
                off = d * 3 * GLA_QK_W
                gla_ref[rows, off:off + GLA_QK_W] = (q1c * jnp.exp(b)).astype(BF16)
                gla_ref[rows, off + GLA_QK_W:off + 2 * GLA_QK_W] = (k1c * jnp.exp(-b)).astype(BF16)
                gla_ref[rows, off + 2 * GLA_QK_W:off + 3 * GLA_QK_W] = (k1c * jnp.exp(last - b)).astype(BF16)
                dec_ref[c, :, d * GLA_QK_W:(d + 1) * GLA_QK_W] = jnp.exp(last)

    part = functools.partial
    tasks = [
        (part(proj, h, C_MISC, LANES), misc_vpu),
        (part(gdn_mm, 0), part(gdn_vpu, 0)),
        (part(proj, h, C_Z, HW), part(z_vpu, 0, HW)),
        (part(gdn_mm, 1), part(gdn_vpu, 1)),
        (part(proj, h, C_Z + HW, HW), part(z_vpu, HW, HW)),
        (part(gdn_mm, 2), part(gdn_vpu, 2)),
        (part(proj, h, C_Z + 2 * HW, XA_W), part(z_vpu, 2 * HW, XA_W)),
        (part(proj, h, C_GQ, 2 * GLA_QK_W), gla_qk_vpu),
        (part(proj, h, C_GV, HW), gla_v_vpu),
        (part(proj, h, C_XQ, XA_W), xaq_vpu),
    ]
    res = tasks[0][0]()
    for t, (_, vpu) in enumerate(tasks):
        nxt_res = tasks[t + 1][0]() if t + 1 < len(tasks) else None
        vpu(res)
        res = nxt_res


def _inproj(layer, x2d, norm_w, w_all, conv_w, w2bd, gla_b, gdn_params):
    n_tok = x2d.shape[0]
    nblk = n_tok // TM_IN
    halo_blocks = TM_IN // HALO
    n_halo = n_tok // HALO
    nchunk = TM_IN // CHUNK
    tok = lambda w: pl.BlockSpec((TM_IN, w), lambda i: (i, 0))
    const = lambda shape: pl.BlockSpec((None,) + shape, lambda i: (layer,) + tuple(0 for _ in shape))
    out_shapes = (
        jax.ShapeDtypeStruct((n_tok, 6 * GLA_QK_W), BF16),
        jax.ShapeDtypeStruct((n_tok, HW), BF16),
        jax.ShapeDtypeStruct((n_tok // CHUNK, 1, 2 * GLA_QK_W), F32),
        jax.ShapeDtypeStruct((n_tok, MIX_PAD_W), BF16),
        jax.ShapeDtypeStruct((n_tok, HW), BF16),
        jax.ShapeDtypeStruct((n_tok, HW), BF16),
        jax.ShapeDtypeStruct((n_tok, HW), BF16),
        jax.ShapeDtypeStruct((n_tok, LANES), F32),
        jax.ShapeDtypeStruct((n_tok // CHUNK, ROW_SLAB, CHUNK), F32),
        jax.ShapeDtypeStruct((n_tok, XA_W), BF16),
    )
    out_specs = (
        tok(6 * GLA_QK_W), tok(HW),
        pl.BlockSpec((nchunk, 1, 2 * GLA_QK_W), lambda i: (i, 0, 0)),
        tok(MIX_PAD_W), tok(HW), tok(HW), tok(HW), tok(LANES),
        pl.BlockSpec((nchunk, ROW_SLAB, CHUNK), lambda i: (i, 0, 0)),
        tok(XA_W),
    )
    return pl.pallas_call(
        _inproj_kernel,
        grid=(nblk,),
        in_specs=[
            tok(D_MODEL),
            pl.BlockSpec((HALO, D_MODEL), lambda i: (jnp.maximum(i * halo_blocks - 1, 0), 0)),
            pl.BlockSpec((HALO, D_MODEL), lambda i: (jnp.minimum((i + 1) * halo_blocks, n_halo - 1), 0)),
            const((1, D_MODEL)),
            const((D_MODEL, IN_PAD_W)),
            const((8, 3 * HW)),
            const((LANES, 2 * GLA_QK_W)),
            const((1, 2 * GLA_QK_W)),
            const((8, LANES)),
        ],
        out_specs=out_specs,
        out_shape=out_shapes,
        scratch_shapes=[pltpu.VMEM((3 * HEADS, TM_IN + 2 * HALO, LANES), F32),
                        pltpu.VMEM((3 * HEADS, TM_IN, LANES), F32)],
        compiler_params=pltpu.CompilerParams(dimension_semantics=("parallel",),
                                             vmem_limit_bytes=VMEM_LIMIT),
        name="inproj",
    )(x2d, x2d, x2d, norm_w, w_all, conv_w, w2bd, gla_b, gdn_params)


def _gla_stages(gf_ref, gb_ref, vf_ref, vb_ref, decf_ref, decb_ref, of_ref, ob_ref, s_ref):
    ri = lax.broadcasted_iota(jnp.int32, (CHUNK, LANES), 0)
    lane_tok = lax.broadcasted_iota(jnp.int32, (CHUNK, LANES), 1)
    ci = lane_tok & (CHUNK - 1)
    incl = (ri >= ci, ri <= ci)
    left = lane_tok < GLA_DK
    lane_sq = lax.broadcasted_iota(jnp.int32, (LANES, LANES), 1) < GLA_DK
    zero_bf = jnp.zeros((CHUNK, LANES), BF16)
    cat = jnp.concatenate

    def split_heads(x):
        return cat([jnp.where(left, x, zero_bf), jnp.where(left, zero_bf, x)], axis=0)

    g_refs, v_refs = (gf_ref, gb_ref), (vf_ref, vb_ref)
    dec_refs, o_refs = (decf_ref, decb_ref), (of_ref, ob_ref)
    chunk_of = lambda d, step: step if d == 0 else NC - 1 - step
    units_of = lambda step: [(d, step, p) for d in range(2) for p in range(GLA_HEADS // 2)]
    qe, kend, v, a, intra, kv = {}, {}, {}, {}, {}, {}
    st = {(d, p): s_ref[d, p] for d in range(2) for p in range(GLA_HEADS // 2)}

    def scores(step):
        for u in units_of(step):
            d, _, p = u
            c = chunk_of(d, step)
            rows = slice(c * CHUNK, (c + 1) * CHUNK)
            qe[u] = g_refs[d][rows, p * LANES:(p + 1) * LANES]
            ke = g_refs[d][rows, GLA_QK_W + p * LANES:GLA_QK_W + (p + 1) * LANES]
            kend[u] = g_refs[d][rows, 2 * GLA_QK_W + p * LANES:2 * GLA_QK_W + (p + 1) * LANES]
            v[u] = (v_refs[d][rows, 2 * p * LANES:(2 * p + 1) * LANES],
                    v_refs[d][rows, (2 * p + 1) * LANES:(2 * p + 2) * LANES])
            a[u] = jnp.where(incl[d], _dot_nt(qe[u], split_heads(ke)), 0.0).astype(BF16)

    def values(step):
        for u in units_of(step):
            intra[u] = _dot(a[u], cat([cat([v[u][0], zero_bf], axis=1), cat([zero_bf, v[u][1]], axis=1)],
                                      axis=0))
            kv[u] = _dot_tn(cat([v[u][0], v[u][1]], axis=1), kend[u])

    def outputs(step):
        for u in units_of(step):
            d, _, p = u
            c = chunk_of(d, step)
            inter = _dot_nt(split_heads(qe[u]), st[(d, p)].astype(BF16))
            for hh in range(2):
                hd = 2 * p + hh
                o = intra[u][:, hh * LANES:(hh + 1) * LANES] + inter[hh * CHUNK:(hh + 1) * CHUNK]
                o_refs[d][c * CHUNK:(c + 1) * CHUNK, hd * LANES:(hd + 1) * LANES] = o.astype(o_refs[d].dtype)
            dec = dec_refs[d][c, :, p * LANES:(p + 1) * LANES]
            st[(d, p)] = dec * st[(d, p)] + jnp.where(lane_sq, kv[u][0:LANES], kv[u][LANES:])

    def finish():
        for (d, p), s in st.items():
            s_ref[d, p] = s

    return scores, values, outputs, finish


def _gdn_stages(qf_ref, kf_ref, vf_ref, colf_ref, rowf_ref, qb_ref, kb_ref, vb_ref, colb_ref, rowb_ref,
                of_ref, ob_ref, s_ref):
    pk = GDN_HEADS * CHUNK
    ri = lax.broadcasted_iota(jnp.int32, (CHUNK, pk), 0)
    ci = lax.broadcasted_iota(jnp.int32, (CHUNK, pk), 1) & (CHUNK - 1)
    incl = (ri >= ci, ri <= ci)
    strict = (ri > ci, ri < ci)
    eye = jnp.where(ri == ci, 1.0, 0.0)
    blk = lambda axis: lax.shift_right_logical(lax.broadcasted_iota(jnp.int32, (pk, pk), axis), 6)
    same_blk = blk(0) == blk(1)
    bd_mask = jnp.where(same_blk, 1.0, 0.0).astype(BF16)
    left = lax.broadcasted_iota(jnp.int32, (CHUNK, LANES), 1) < CHUNK
    zero_tok = jnp.zeros((CHUNK, LANES), BF16)
    zero_sq = jnp.zeros((LANES, LANES), BF16)

    refs = ((qf_ref, kf_ref, vf_ref, colf_ref, rowf_ref, of_ref),
            (qb_ref, kb_ref, vb_ref, colb_ref, rowb_ref, ob_ref))
    groups = [(d, step) for step in range(NC) for d in range(2)]
    pairs = range(GDN_HEADS // 2)
    each = lambda fn, *lists: [fn(*args) for args in zip(*lists)]
    top = lambda m: m[0:CHUNK]
    bot = lambda m: m[CHUNK:2 * CHUNK]
    cat = jnp.concatenate
    chunk_of = lambda d, step: step if d == 0 else NC - 1 - step

    def block_diag(x):
        return cat([x, x, x, x], axis=0) * bd_mask

    def diag2(a, b, zero):
        return cat([cat([a, zero], axis=1), cat([zero, b], axis=1)], axis=0)

    q, k, v, col, gc_r = [], [], [], [], []
    for d, step in groups:
        q_ref, k_ref, v_ref, col_ref, row_ref, _ = refs[d]
        c = chunk_of(d, step)
        rows = slice(c * CHUNK, (c + 1) * CHUNK)
        heads = [slice(hd * LANES, (hd + 1) * LANES) for hd in range(GDN_HEADS)]
        q.append([q_ref[rows, h] for h in heads])
        k.append([k_ref[rows, h] for h in heads])
        v.append([v_ref[rows, h] for h in heads])
        col.append(col_ref[rows, :])
        rowt = row_ref[c]
        gc_r.append(cat([rowt[8 + d * GDN_HEADS + hd:9 + d * GDN_HEADS + hd, :] for hd in range(GDN_HEADS)],
                        axis=1))

    def colv(g, base, hd):
        lane = base + groups[g][0] * GDN_HEADS + hd
        return col[g][:, lane:lane + 1]

    def col_packed(g, base):
        tiles = [jnp.where(left, jnp.broadcast_to(colv(g, base, 2 * j), (CHUNK, LANES)),
                           jnp.broadcast_to(colv(g, base, 2 * j + 1), (CHUNK, LANES))) for j in pairs]
        return cat(tiles, axis=1)

    qk, uw = {}, {}

    def rhs_of(g, hd):
        beta = colv(g, G_BETA, hd)
        return cat([(v[g][hd].astype(F32) * beta).astype(BF16),
                    (k[g][hd].astype(F32) * (beta * colv(g, G_EGC, hd))).astype(BF16)], axis=1)

    def pre(gl):
        kq = {g: [_dot_nt(cat([cat([k[g][2 * j], k[g][2 * j + 1]], axis=1),
                               cat([q[g][2 * j], q[g][2 * j + 1]], axis=1)], axis=0),
                          diag2(k[g][2 * j], k[g][2 * j + 1], zero_tok)) for j in pairs] for g in gl}
        yield
        pm, cur = {}, {}
        for g in gl:
            d = groups[g][0]
            kk = cat([top(kq[g][j]) for j in pairs], axis=1)
            qkr = cat([bot(kq[g][j]) for j in pairs], axis=1)
            decay = jnp.where(incl[d], jnp.exp(jnp.where(incl[d], col_packed(g, G_GC) - gc_r[g], 0.0)), 0.0)
            n = jnp.where(strict[d], -(kk * decay * col_packed(g, G_BETA)), 0.0)
            qk[g] = jnp.where(incl[d], qkr * decay, 0.0).astype(BF16)
            pm[g] = eye + n
            cur[g] = _dot(n.astype(BF16), block_diag(n.astype(BF16)))
        yield
        for _ in range(4):
            st = {g: _dot(cat([pm[g].astype(BF16), cur[g].astype(BF16)], axis=0),
                          block_diag(cur[g].astype(BF16))) for g in gl}
            for g in gl:
                pm[g] = pm[g] + top(st[g])
                cur[g] = bot(st[g])
            yield
        tinv = {g: pm[g] + _dot(pm[g].astype(BF16), block_diag(cur[g].astype(BF16))) for g in gl}
        yield
        for g in gl:
            uw[g] = []
            for j in pairs:
                t = tinv[g][:, j * LANES:(j + 1) * LANES]
                lhs = cat([jnp.where(left, t, 0.0), jnp.where(left, 0.0, t)], axis=0).astype(BF16)
                uw[g].append(_dot(lhs, cat([rhs_of(g, 2 * j), rhs_of(g, 2 * j + 1)], axis=0)))
        yield

    def scan(step):
        gs = [g for g, (_, st_) in enumerate(groups) if st_ == step]
        units = [(g, j) for g in gs for j in pairs]
        s = {(g, hd): s_ref[groups[g][0], hd] for g in gs for hd in range(GDN_HEADS)}
        r = {}
        for g, j in units:
            a, b = 2 * j, 2 * j + 1
            x = uw[g][j]
            lhs = cat([cat([x[0:CHUNK, LANES:].astype(BF16), x[CHUNK:, LANES:].astype(BF16)], axis=1),
                       cat([q[g][a], q[g][b]], axis=1)], axis=0)
            r[(g, j)] = _dot(lhs, diag2(s[(g, a)].astype(BF16), s[(g, b)].astype(BF16), zero_sq))
        yield
        vn = {}
        for g, j in units:
            x, rr = uw[g][j], r[(g, j)]
            vn[(g, 2 * j)] = x[0:CHUNK, 0:LANES] - rr[0:CHUNK, 0:LANES]
            vn[(g, 2 * j + 1)] = x[CHUNK:, 0:LANES] - rr[0:CHUNK, LANES:]
        for g, j in units:
            d = groups[g][0]
            a, b = 2 * j, 2 * j + 1
            intra = _dot(qk[g][:, j * LANES:(j + 1) * LANES],
                         diag2(vn[(g, a)].astype(BF16), vn[(g, b)].astype(BF16), zero_tok))
            c = chunk_of(d, step)
            for hd, lanes in ((a, slice(0, LANES)), (b, slice(LANES, 2 * LANES))):
                o = colv(g, G_EGC, hd) * r[(g, j)][CHUNK:, lanes] + intra[:, lanes]
                o_ref = refs[d][5]
                o_ref[c * CHUNK:(c + 1) * CHUNK, hd * LANES:(hd + 1) * LANES] = o.astype(o_ref.dtype)
        for g in gs:
            d = groups[g][0]
            for hd in range(GDN_HEADS):
                dec = col[g][0:1, G_DEC + d * GDN_HEADS + hd:G_DEC + d * GDN_HEADS + hd + 1]
                kv = _dot_tn(k[g][hd], (colv(g, G_EKEND, hd) * vn[(g, hd)]).astype(BF16))
                s_ref[d, hd] = dec * s[(g, hd)] + kv
        yield

    groups_of = lambda lo, hi: [g for g, (_, st_) in enumerate(groups) if lo <= st_ < hi]
    return pre, scan, groups_of


def _scan_kernel(gf_ref, gb_ref, v1f_ref, v1b_ref, decf_ref, decb_ref,
                 qf_ref, kf_ref, vf_ref, colf_ref, rowf_ref, qb_ref, kb_ref, vb_ref, colb_ref, rowb_ref,
                 o1f_ref, o1b_ref, o2f_ref, o2b_ref, s1_ref, s2_ref):
    @pl.when(pl.program_id(1) == 0)
    def _():
        s1_ref[...] = jnp.zeros_like(s1_ref)
        s2_ref[...] = jnp.zeros_like(s2_ref)

    gla_scores, gla_values, gla_outputs, gla_finish = _gla_stages(
        gf_ref, gb_ref, v1f_ref, v1b_ref, decf_ref, decb_ref, o1f_ref, o1b_ref, s1_ref)
    pre, scan, groups_of = _gdn_stages(qf_ref, kf_ref, vf_ref, colf_ref, rowf_ref,
                                       qb_ref, kb_ref, vb_ref, colb_ref, rowb_ref, o2f_ref, o2b_ref, s2_ref)

    half = NC // 2
    for _ in pre(groups_of(0, half)):
        pass
    later = pre(groups_of(half, NC))
    for step in range(half):
        for _ in scan(step):
            next(later, None)
    for _ in later:
        pass
    gla_scores(0)
    gla_step = 0
    for step in range(half, NC):
        for _ in scan(step):
            if gla_step + 1 < NC:
                gla_scores(gla_step + 1)
            gla_values(gla_step)
            gla_outputs(gla_step)
            gla_step += 1
    assert gla_step == NC
    gla_finish()


def _mix_scan(gla, v1, dec, q2, k2, v2, gcol, grow):
    nb = SEQ // TB
    n_tok = q2.shape[0]
    fwd = lambda b, i: (b * nb + i, 0)
    bwd = lambda b, i: (b * nb + nb - 1 - i, 0)
    fwd3 = lambda b, i: (b * nb + i, 0, 0)
    bwd3 = lambda b, i: (b * nb + nb - 1 - i, 0, 0)

    def gdn_specs(m2, m3):
        return [pl.BlockSpec((TB, HW), m2), pl.BlockSpec((TB, HW), m2), pl.BlockSpec((TB, HW), m2),
                pl.BlockSpec((TB, LANES), m2), pl.BlockSpec((NC, ROW_SLAB, CHUNK), m3)]

    gla_specs = [
        pl.BlockSpec((TB, 3 * GLA_QK_W), fwd),
        pl.BlockSpec((TB, 3 * GLA_QK_W), lambda b, i: (b * nb + nb - 1 - i, 1)),
        pl.BlockSpec((TB, HW), fwd),
        pl.BlockSpec((TB, HW), bwd),
        pl.BlockSpec((NC, 1, GLA_QK_W), lambda b, i: (b * nb + i, 0, 0)),
        pl.BlockSpec((NC, 1, GLA_QK_W), lambda b, i: (b * nb + nb - 1 - i, 0, 1)),
    ]
    out = jax.ShapeDtypeStruct((n_tok, HW), BF16)
    return pl.pallas_call(
        _scan_kernel,
        grid=(BATCH, nb),
        in_specs=gla_specs + gdn_specs(fwd, fwd3) + gdn_specs(bwd, bwd3),
        out_specs=(pl.BlockSpec((TB, HW), fwd), pl.BlockSpec((TB, HW), bwd),
                   pl.BlockSpec((TB, HW), fwd), pl.BlockSpec((TB, HW), bwd)),
        out_shape=(out, out, out, out),
        scratch_shapes=[pltpu.VMEM((2, GLA_HEADS // 2, LANES, LANES), F32),
                        pltpu.VMEM((2, GDN_HEADS, LANES, LANES), F32)],
        compiler_params=pltpu.CompilerParams(dimension_semantics=("parallel", "arbitrary"),
                                             vmem_limit_bytes=VMEM_LIMIT),
        name="mix_scan",
    )(gla, gla, v1, v1, dec, dec, q2, k2, v2, gcol, grow, q2, k2, v2, gcol, grow)


def _outproj_kernel(o1f_ref, o1b_ref, o2f_ref, o2b_ref, zg_ref, xaq_ref, mkv_ref, x_ref, wout_ref,
                    n1_ref, n2_ref, n3_ref, fn_ref, out_ref, *, final):
    def head_norm(o, nw_ref, width):
        parts = []
        for hd in range(HEADS):
            s = o[:, hd * LANES:(hd + 1) * LANES]
            ms = jnp.sum(s * s, axis=-1, keepdims=True) * (1.0 / width)
            parts.append(s * lax.rsqrt(ms + NORM_EPS))
        return jnp.concatenate(parts, axis=-1) * nw_ref[...]

    both = lambda f_ref, b_ref: f_ref[...].astype(F32) + b_ref[...].astype(F32)
    gate = lambda lo, width: zg_ref[:, lo:lo + width].astype(F32)
    lane_q = lax.broadcasted_iota(jnp.int32, (TM_OUT, LANES), 1)
    lane_m = lax.broadcasted_iota(jnp.int32, (MEM_LEN, LANES), 1)
    first_q = lane_q < XA_DH
    first_m = lane_m < XA_DH
    q_head = (first_q, lane_q >= XA_DH)
    m_head = (first_m, lane_m >= XA_DH)
    heads = [(p, hh) for p in range(XA_HEADS // 2) for hh in range(2)]

    sc = []
    for p, hh in heads:
        qpair = xaq_ref[:, p * LANES:(p + 1) * LANES]
        qm = jnp.where(q_head[hh], qpair, jnp.zeros_like(qpair))
        sc.append(_dot_nt(qm, mkv_ref[0, :, p * LANES:(p + 1) * LANES]))

    o1 = head_norm(both(o1f_ref, o1b_ref), n1_ref, GLA_DV) * gate(0, HW)
    y = _dot(o1.astype(BF16), wout_ref[0:HW, :])

    pv = []
    for (p, hh), s in zip(heads, sc):
        e = jnp.exp(s - jnp.max(s, axis=-1, keepdims=True))
        l = jnp.sum(e, axis=-1, keepdims=True)
        mv = mkv_ref[0, :, XA_W + p * LANES:XA_W + (p + 1) * LANES]
        mvm = jnp.where(m_head[hh], mv, jnp.zeros_like(mv))
        pv.append(_dot(e.astype(BF16), mvm) * (1.0 / l))

    o2 = head_norm(both(o2f_ref, o2b_ref), n2_ref, GDN_DV) * gate(HW, HW)
    y = y + _dot(o2.astype(BF16), wout_ref[HW:2 * HW, :])

    for p in range(XA_HEADS // 2):
        lanes = slice(p * LANES, (p + 1) * LANES)
        acc = pv[2 * p] + pv[2 * p + 1]
        sq = acc * acc
        ss0 = jnp.sum(jnp.where(first_q, sq, 0.0), axis=-1, keepdims=True)
        ss1 = jnp.sum(jnp.where(first_q, 0.0, sq), axis=-1, keepdims=True)
        ms = jnp.where(first_q, ss0, ss1) * (1.0 / XA_DH)
        o3 = acc * lax.rsqrt(ms + NORM_EPS) * n3_ref[:, lanes] * gate(2 * HW + p * LANES, LANES)
        y = y + _dot(o3.astype(BF16), wout_ref[2 * HW + p * LANES:2 * HW + (p + 1) * LANES, :])

    xo = x_ref[...] + y
    if final:
        ms = jnp.mean(xo * xo, axis=-1, keepdims=True)
        xo = xo * lax.rsqrt(ms + NORM_EPS) * fn_ref[...]
    out_ref[...] = xo


def _outproj(layer, o1f, o1b, o2f, o2b, zg, xaq, mkv, x2d, wout, n1, n2, n3, fnw):
    n_tok = x2d.shape[0]
    blocks_per_seq = SEQ // TM_OUT
    tok = lambda w: pl.BlockSpec((TM_OUT, w), lambda i: (i, 0))
    const = lambda shape: pl.BlockSpec((None,) + shape, lambda i: (layer,) + tuple(0 for _ in shape))
    return pl.pallas_call(
        functools.partial(_outproj_kernel, final=layer == DEPTH - 1),
        grid=(n_tok // TM_OUT,),
        in_specs=[
            tok(HW), tok(HW), tok(HW), tok(HW), tok(MIX_PAD_W), tok(XA_W),
            pl.BlockSpec((None, 1, MEM_LEN, 2 * XA_W), lambda i: (layer, i // blocks_per_seq, 0, 0)),
            tok(D_MODEL),
            const((MIX_PAD_W, D_MODEL)),
            const((1, HW)), const((1, HW)), const((1, XA_W)),
            pl.BlockSpec((1, D_MODEL), lambda i: (0, 0)),
        ],
        out_specs=tok(D_MODEL),
        out_shape=jax.ShapeDtypeStruct((n_tok, D_MODEL), F32),
        compiler_params=pltpu.CompilerParams(dimension_semantics=("parallel",),
                                             vmem_limit_bytes=VMEM_LIMIT),
        name="outproj",
    )(o1f, o1b, o2f, o2b, zg, xaq, mkv, x2d, wout, n1, n2, n3, fnw)


def _pad_heads(w, axis):
    shp = w.shape
    w = w.reshape(shp[:axis] + (HEADS, GLA_DV) + shp[axis + 1:])
    pad = [(0, 0)] * w.ndim
    pad[axis + 1] = (0, LANES - GLA_DV)
    w = jnp.pad(w, pad)
    return w.reshape(shp[:axis] + (HW,) + shp[axis + 1:])


def _pack_params(norm_w, w_in, gla_w2, gla_b, gla_norm_w, gdn_conv_w, gdn_a_log, gdn_dt_bias, gdn_norm_w,
                 xa_norm_w, w_out):
    sizes = (GLA_QK_W, GLA_QK_W, HEADS * GLA_DV, HEADS * GLA_DV, 2 * GLA_RANK, 3 * HEADS * GDN_DK,
             HEADS * GDN_DV, 2 * GDN_HEADS, 2 * GDN_HEADS, XA_W, XA_W)
    cols, start = [], 0
    for s in sizes:
        cols.append(w_in[:, :, start:start + s])
        start += s
    gq, gk, gv, gz, glr, dqkv, dz, db, da, xq, xz = cols
    hd = HEADS * GDN_DK
    zeros = lambda *shape: jnp.zeros((DEPTH,) + shape, F32)
    misc = jnp.concatenate(
        [db] + [da] * M_A_COPIES
        + [zeros(D_MODEL, M_LR - M_A - 8 * M_A_COPIES), glr, zeros(D_MODEL, LANES - M_LR - 2 * GLA_RANK)],
        axis=2)
    w_all = jnp.concatenate(
        [gq, gk, _pad_heads(gv, 2),
         _pad_heads(dqkv[:, :, 0:hd], 2), _pad_heads(dqkv[:, :, hd:2 * hd], 2), _pad_heads(dqkv[:, :, 2 * hd:], 2),
         _pad_heads(gz, 2), _pad_heads(dz, 2), xz, xq, misc], axis=2).astype(BF16)

    cw = jnp.transpose(gdn_conv_w, (0, 2, 1))
    cw = jnp.concatenate([_pad_heads(cw[:, :, 0:hd], 2), _pad_heads(cw[:, :, hd:2 * hd], 2),
                          _pad_heads(cw[:, :, 2 * hd:], 2)], axis=2)
    cw = jnp.pad(cw, ((0, 0), (0, 8 - GDN_CONV), (0, 0)))

    w2bd = jnp.concatenate(
        [zeros(M_LR, 2 * GLA_QK_W),
         jnp.concatenate([gla_w2[:, 0], zeros(GLA_RANK, GLA_QK_W)], axis=2),
         jnp.concatenate([zeros(GLA_RANK, GLA_QK_W), gla_w2[:, 1]], axis=2),
         zeros(LANES - M_LR - 2 * GLA_RANK, 2 * GLA_QK_W)], axis=1).astype(BF16)
    glab = gla_b.reshape(DEPTH, 1, 2 * GLA_QK_W)

    def a_slab(p):
        flat = p.reshape(DEPTH, 1, 2 * GDN_HEADS)
        return jnp.concatenate([zeros(1, M_A)] + [flat] * M_A_COPIES + [zeros(1, LANES - M_A - 8 * M_A_COPIES)],
                               axis=2)

    gparams = jnp.concatenate([a_slab(gdn_a_log), a_slab(gdn_dt_bias), zeros(6, LANES)], axis=1)

    wout = jnp.concatenate([_pad_heads(w_out[:, 0:HEADS * GLA_DV], 1),
                            _pad_heads(w_out[:, HEADS * GLA_DV:HEADS * (GLA_DV + GDN_DV)], 1),
                            w_out[:, HEADS * (GLA_DV + GDN_DV):]], axis=1).astype(BF16)
    pad_norm = lambda w: jnp.tile(jnp.pad(w, ((0, 0), (0, LANES - w.shape[1]))), (1, HEADS)).reshape(DEPTH, 1, HW)
    n1 = pad_norm(gla_norm_w)
    n2 = pad_norm(gdn_norm_w)
    n3 = jnp.tile(xa_norm_w, (1, XA_HEADS)).reshape(DEPTH, 1, XA_W)
    return norm_w.reshape(DEPTH, 1, D_MODEL), w_all, cw, w2bd, glab, gparams, wout, n1, n2, n3


def kernel(x, mem, norm_w, w_in, gla_w2, gla_b, gla_norm_w, gdn_conv_w, gdn_a_log, gdn_dt_bias,
           gdn_norm_w, mem_norm_w, xa_w_kv, xa_norm_w, w_out, final_norm_w):
    assert x.shape == (BATCH, SEQ, D_MODEL) and mem.shape == (BATCH, MEM_LEN, D_MODEL)
    mkv = _memkv(mem, mem_norm_w, xa_w_kv.astype(BF16))
    nw, w_all, cw, w2bd, glab, gparams, wout, n1, n2, n3 = _pack_params(
        norm_w, w_in, gla_w2, gla_b, gla_norm_w, gdn_conv_w, gdn_a_log, gdn_dt_bias, gdn_norm_w,
        xa_norm_w, w_out)
    h = x.reshape(BATCH * SEQ, D_MODEL)
    fnw = final_norm_w.reshape(1, D_MODEL)
    for l in range(DEPTH):
        gla, v1, dec, zg, q2, k2, v2, gcol, grow, xaq = _inproj(l, h, nw, w_all, cw, w2bd, glab, gparams)
        o1f, o1b, o2f, o2b = _mix_scan(gla, v1, dec, q2, k2, v2, gcol, grow)
        h = _outproj(l, o1f, o1b, o2f, o2b, zg, xaq, mkv, h, wout, n1, n2, n3, fnw)
    return h.reshape(BATCH, SEQ, D_MODEL)
```

```python
import functools

import jax
import jax.numpy as jnp
from jax import lax
from jax.experimental import pallas as pl
from jax.experimental.pallas import tpu as pltpu

F32 = jnp.float32
BF16 = jnp.bfloat16

D_MODEL = 1024
BATCH = 8
SEQ = 4096
DEPTH = 2
MEM_LEN = 256
CHUNK = 64
NORM_EPS = 1e-6
GLA_HEADS = 4
GLA_DK = 64
GLA_DV = 96
GLA_RANK = 16
GLA_GATE_NORMALIZER = 16.0
GDN_HEADS = 4
GDN_DK = 96
GDN_DV = 96
GDN_CONV = 5
XA_HEADS = 4
XA_DH = 64

LANES = 128
HEADS = 4
HW = HEADS * LANES
GLA_QK_W = GLA_HEADS * GLA_DK
XA_W = XA_HEADS * XA_DH
MIX_PAD_W = 2 * HW + XA_W

C_GQ = 0
C_GK = C_GQ + GLA_QK_W
C_GV = C_GK + GLA_QK_W
C_DQKV = C_GV + HW
C_Z = C_DQKV + 3 * HW
C_XQ = C_Z + MIX_PAD_W
C_MISC = C_XQ + XA_W
IN_PAD_W = C_MISC + LANES

M_BETA = 0
M_A = 8
M_A_COPIES = 4
M_LR = 64
G_BETA, G_GC, G_EGC, G_EKEND, G_DEC = 0, 8, 16, 24, 32
ROW_SLAB = 16

TM_IN = 512
TM_OUT = 512
NC = 8
TB = NC * CHUNK
HALO = 8
CONV_PHASES = 4
VMEM_LIMIT = 56 * 1024 * 1024


def _dot(a, b):
    return jnp.dot(a, b, preferred_element_type=F32)


def _dot_nt(a, b):
    return lax.dot_general(a, b, (((1,), (1,)), ((), ())), preferred_element_type=F32)


def _dot_tn(a, b):
    return lax.dot_general(a, b, (((0,), (0,)), ((), ())), preferred_element_type=F32)


def _sigmoid(x):
    return 1.0 / (1.0 + jnp.exp(-x))


def _silu(x):
    return x * _sigmoid(x)


def _softplus(x):
    return jnp.maximum(x, 0.0) + jnp.log(1.0 + jnp.exp(-jnp.abs(x)))


def _log_sigmoid(x):
    return jnp.minimum(x, 0.0) - jnp.log(1.0 + jnp.exp(-jnp.abs(x)))


def _split2(x):
    hi = pltpu.bitcast(pltpu.bitcast(x, jnp.int32) & jnp.int32(-65536), F32)
    return hi.astype(BF16), (x - hi).astype(BF16)


def _tri_sum(tri, x):
    hi, lo = _split2(x)
    return _dot(tri, hi) + _dot(tri, lo)


def _tri_masks():
    ri = lax.broadcasted_iota(jnp.int32, (CHUNK, CHUNK), 0)
    ci = lax.broadcasted_iota(jnp.int32, (CHUNK, CHUNK), 1)
    return ri, ci


def _memkv_kernel(mem_ref, nw_ref, w_ref, out_ref):
    m = mem_ref[0]
    ms = jnp.mean(m * m, axis=-1, keepdims=True)
    mn = (m * lax.rsqrt(ms + NORM_EPS) * nw_ref[0]).astype(BF16)
    out_ref[0, 0] = _dot(mn, w_ref[0]).astype(BF16)


def _memkv(mem, mem_norm_w, xa_w_kv_bf16):
    return pl.pallas_call(
        _memkv_kernel,
        grid=(DEPTH, BATCH),
        in_specs=[
            pl.BlockSpec((1, MEM_LEN, D_MODEL), lambda l, b: (b, 0, 0)),
            pl.BlockSpec((1, 1, D_MODEL), lambda l, b: (l, 0, 0)),
            pl.BlockSpec((1, D_MODEL, 2 * XA_W), lambda l, b: (l, 0, 0)),
        ],
        out_specs=pl.BlockSpec((1, 1, MEM_LEN, 2 * XA_W), lambda l, b: (l, b, 0, 0)),
        out_shape=jax.ShapeDtypeStruct((DEPTH, BATCH, MEM_LEN, 2 * XA_W), BF16),
        compiler_params=pltpu.CompilerParams(dimension_semantics=("parallel", "parallel")),
        name="memkv",
    )(mem, mem_norm_w.reshape(DEPTH, 1, D_MODEL), xa_w_kv_bf16)


def _inproj_kernel(x_ref, xp_ref, xn_ref, nw_ref, w_ref, cw_ref, w2_ref, gb_ref, gp_ref,
                   gla_ref, v1_ref, dec_ref, zg_ref, q2_ref, k2_ref, v2_ref, gcol_ref, grow_ref,
                   xaq_ref, ext_ref, conv_ref):
    nw = nw_ref[...]

    def norm(x):
        ms = jnp.mean(x * x, axis=-1, keepdims=True)
        return (x * lax.rsqrt(ms + NORM_EPS) * nw).astype(BF16)

    h = norm(x_ref[...])
    h_halo = norm(jnp.concatenate([xp_ref[...], xn_ref[...]], axis=0))

    def proj(hh, lo, width):
        return _dot(hh, w_ref[:, lo:lo + width])

    blocks_per_seq = SEQ // TM_IN
    j = lax.rem(pl.program_id(0), blocks_per_seq)
    cw = cw_ref[...]
    base = HALO - GDN_CONV // 2

    def gdn_mm(grp):
        lo = C_DQKV + grp * HW
        halo = proj(h_halo, lo, HW)
        return proj(h, lo, HW), halo[0:HALO], halo[HALO:]

    def gdn_vpu(grp, res):
        main, prev, nxt = res
        for hd in range(HEADS):
            slab = grp * HEADS + hd
            lanes = slice(hd * LANES, (hd + 1) * LANES)
            ext_ref[slab, 0:HALO, :] = jnp.where(j == 0, 0.0, prev[:, lanes])
            ext_ref[slab, HALO:HALO + TM_IN, :] = main[:, lanes]
            ext_ref[slab, HALO + TM_IN:2 * HALO + TM_IN, :] = jnp.where(j == blocks_per_seq - 1, 0.0,
                                                                     nxt[:, lanes])
            taps = cw[:, grp * HW + hd * LANES:grp * HW + (hd + 1) * LANES]
            for p in range(CONV_PHASES):
                acc = None
                for t in range(GDN_CONV):
                    win = ext_ref[slab, pl.ds(base + p + t, TM_IN // CONV_PHASES, stride=CONV_PHASES), :]
                    acc = win * taps[t:t + 1, :] if acc is None else acc + win * taps[t:t + 1, :]
                conv_ref[slab, pl.ds(p, TM_IN // CONV_PHASES, stride=CONV_PHASES), :] = acc
            y = _silu(conv_ref[slab])
            if grp == 2:
                v2_ref[:, lanes] = y.astype(BF16)
            else:
                ref, scale = ((q2_ref, GDN_DK ** -0.5), (k2_ref, 1.0))[grp]
                ss = jnp.sum(y * y, axis=-1, keepdims=True)
                ref[:, lanes] = (y * lax.rsqrt(ss + NORM_EPS) * scale).astype(BF16)

    def z_vpu(lo, width, res):
        zg_ref[:, lo:lo + width] = _silu(res).astype(BF16)

    def xaq_vpu(res):
        xaq_ref[...] = (res * (XA_DH ** -0.5)).astype(BF16)

    lane = lax.broadcasted_iota(jnp.int32, (1, LANES), 1)
    fwd_lane = lax.rem(lane, 8) < GDN_HEADS
    ri, ci = _tri_masks()
    lower = jnp.where(ri >= ci, 1.0, 0.0).astype(BF16)
    upper = jnp.where(ri <= ci, 1.0, 0.0).astype(BF16)
    lower_upper = jnp.concatenate([lower, upper], axis=0)
    gate = {}

    def misc_vpu(m):
        logits = _dot(m.astype(BF16), w2_ref[...]) + gb_ref[...]
        gate["g"] = _log_sigmoid(logits) * (1.0 / GLA_GATE_NORMALIZER)
        is_a = (lane >= M_A) & (lane < M_A + 8 * M_A_COPIES)
        neg_a = jnp.where(is_a, -jnp.exp(gp_ref[0:1, :]), 0.0)
        gg = neg_a * _softplus(m + gp_ref[1:2, :])
        beta = _sigmoid(m)
        for c in range(TM_IN // CHUNK):
            rows = slice(c * CHUNK, (c + 1) * CHUNK)
            ggc = gg[rows]
            hi, lo = _split2(ggc)
            both = _dot(lower_upper, jnp.concatenate([hi, lo], axis=1))
            pf = both[0:CHUNK, 0:LANES] + both[0:CHUNK, LANES:]
            sf = both[CHUNK:, 0:LANES] + both[CHUNK:, LANES:]
            gc = jnp.where(fwd_lane, pf, sf)
            last = jnp.where(fwd_lane, pf[CHUNK - 1:CHUNK], sf[0:1])
            col = jnp.where(lane < G_GC, beta[rows],
                  jnp.where(lane < G_EGC, gc,
                  jnp.where(lane < G_EKEND, jnp.exp(gc),
                  jnp.where(lane < G_DEC, jnp.exp(last - gc), jnp.exp(last)))))
            gcol_ref[rows, :] = col
            grow_ref[c] = col.T[0:ROW_SLAB, :]

    def gla_v_vpu(res):
        v1_ref[...] = res.astype(BF16)

    def gla_qk_vpu(pg):
        q1 = pg[:, 0:GLA_QK_W] * (GLA_DK ** -0.5)
        k1 = pg[:, GLA_QK_W:2 * GLA_QK_W]
        g = gate["g"]
        for c in range(TM_IN // CHUNK):
            rows = slice(c * CHUNK, (c + 1) * CHUNK)
            gch = g[rows]
            q1c = q1[rows]
            k1c = k1[rows]
            bf = _tri_sum(lower, gch[:, 0:GLA_QK_W])
            br = _tri_sum(upper, gch[:, GLA_QK_W:])
            for d, (b, last) in enumerate(((bf, bf[CHUNK - 1:CHUNK]), (br, br[0:1]))):
                off = d * 3 * GLA_QK_W
                gla_ref[rows, off:off + GLA_QK_W] = (q1c * jnp.exp(b)).astype(BF16)
                gla_ref[rows, off + GLA_QK_W:off + 2 * GLA_QK_W] = (k1c * jnp.exp(-b)).astype(BF16)
                gla_ref[rows, off + 2 * GLA_QK_W:off + 3 * GLA_QK_W] = (k1c * jnp.exp(last - b)).astype(BF16)
                dec_ref[c, :, d * GLA_QK_W:(d + 1) * GLA_QK_W] = jnp.exp(last)

    part = functools.partial
    tasks = [
        (part(proj, h, C_MISC, LANES), misc_vpu),
        (part(gdn_mm, 0), part(gdn_vpu, 0)),
        (part(proj, h, C_Z, HW), part(z_vpu, 0, HW)),
        (part(gdn_mm, 1), part(gdn_vpu, 1)),
        (part(proj, h, C_Z + HW, HW), part(z_vpu, HW, HW)),
        (part(gdn_mm, 2), part(gdn_vpu, 2)),
        (part(proj, h, C_Z + 2 * HW, XA_W), part(z_vpu, 2 * HW, XA_W)),
        (part(proj, h, C_GQ, 2 * GLA_QK_W), gla_qk_vpu),
        (part(proj, h, C_GV, HW), gla_v_vpu),
        (part(proj, h, C_XQ, XA_W), xaq_vpu),
    ]
    res = tasks[0][0]()
    for t, (_, vpu) in enumerate(tasks):
        nxt_res = tasks[t + 1][0]() if t + 1 < len(tasks) else None
        vpu(res)
        res = nxt_res


def _inproj(layer, x2d, norm_w, w_all, conv_w, w2bd, gla_b, gdn_params):
    n_tok = x2d.shape[0]
    nblk = n_tok // TM_IN
    halo_blocks = TM_IN // HALO
    n_halo = n_tok // HALO
    nchunk = TM_IN // CHUNK
    tok = lambda w: pl.BlockSpec((TM_IN, w), lambda i: (i, 0))
    const = lambda shape: pl.BlockSpec((None,) + shape, lambda i: (layer,) + tuple(0 for _ in shape))
    out_shapes = (
        jax.ShapeDtypeStruct((n_tok, 6 * GLA_QK_W), BF16),
        jax.ShapeDtypeStruct((n_tok, HW), BF16),
        jax.ShapeDtypeStruct((n_tok // CHUNK, 1, 2 * GLA_QK_W), F32),
        jax.ShapeDtypeStruct((n_tok, MIX_PAD_W), BF16),
        jax.ShapeDtypeStruct((n_tok, HW), BF16),
        jax.ShapeDtypeStruct((n_tok, HW), BF16),
        jax.ShapeDtypeStruct((n_tok, HW), BF16),
        jax.ShapeDtypeStruct((n_tok, LANES), F32),
        jax.ShapeDtypeStruct((n_tok // CHUNK, ROW_SLAB, CHUNK), F32),
        jax.ShapeDtypeStruct((n_tok, XA_W), BF16),
    )
    out_specs = (
        tok(6 * GLA_QK_W), tok(HW),
        pl.BlockSpec((nchunk, 1, 2 * GLA_QK_W), lambda i: (i, 0, 0)),
        tok(MIX_PAD_W), tok(HW), tok(HW), tok(HW), tok(LANES),
        pl.BlockSpec((nchunk, ROW_SLAB, CHUNK), lambda i: (i, 0, 0)),
        tok(XA_W),
    )
    return pl.pallas_call(
        _inproj_kernel,
        grid=(nblk,),
        in_specs=[
            tok(D_MODEL),
            pl.BlockSpec((HALO, D_MODEL), lambda i: (jnp.maximum(i * halo_blocks - 1, 0), 0)),
            pl.BlockSpec((HALO, D_MODEL), lambda i: (jnp.minimum((i + 1) * halo_blocks, n_halo - 1), 0)),
            const((1, D_MODEL)),
            const((D_MODEL, IN_PAD_W)),
            const((8, 3 * HW)),
            const((LANES, 2 * GLA_QK_W)),
            const((1, 2 * GLA_QK_W)),
            const((8, LANES)),
        ],
        out_specs=out_specs,
        out_shape=out_shapes,
        scratch_shapes=[pltpu.VMEM((3 * HEADS, TM_IN + 2 * HALO, LANES), F32),
                        pltpu.VMEM((3 * HEADS, TM_IN, LANES), F32)],
        compiler_params=pltpu.CompilerParams(dimension_semantics=("parallel",),
                                             vmem_limit_bytes=VMEM_LIMIT),
        name="inproj",
    )(x2d, x2d, x2d, norm_w, w_all, conv_w, w2bd, gla_b, gdn_params)


def _gla_stages(gf_ref, gb_ref, vf_ref, vb_ref, decf_ref, decb_ref, of_ref, ob_ref, s_ref):
    ri = lax.broadcasted_iota(jnp.int32, (CHUNK, LANES), 0)
    lane_tok = lax.broadcasted_iota(jnp.int32, (CHUNK, LANES), 1)
    ci = lane_tok & (CHUNK - 1)
    incl = (ri >= ci, ri <= ci)
    left = lane_tok < GLA_DK
    lane_sq = lax.broadcasted_iota(jnp.int32, (LANES, LANES), 1) < GLA_DK
    zero_bf = jnp.zeros((CHUNK, LANES), BF16)
    cat = jnp.concatenate

    def split_heads(x):
        return cat([jnp.where(left, x, zero_bf), jnp.where(left, zero_bf, x)], axis=0)

    g_refs, v_refs = (gf_ref, gb_ref), (vf_ref, vb_ref)
    dec_refs, o_refs = (decf_ref, decb_ref), (of_ref, ob_ref)
    chunk_of = lambda d, step: step if d == 0 else NC - 1 - step
    units_of = lambda step: [(d, step, p) for d in range(2) for p in range(GLA_HEADS // 2)]
    qe, kend, v, a, intra, kv = {}, {}, {}, {}, {}, {}
    st = {(d, p): s_ref[d, p] for d in range(2) for p in range(GLA_HEADS // 2)}

    def scores(step):
        for u in units_of(step):
            d, _, p = u
            c = chunk_of(d, step)
            rows = slice(c * CHUNK, (c + 1) * CHUNK)
            qe[u] = g_refs[d][rows, p * LANES:(p + 1) * LANES]
            ke = g_refs[d][rows, GLA_QK_W + p * LANES:GLA_QK_W + (p + 1) * LANES]
            kend[u] = g_refs[d][rows, 2 * GLA_QK_W + p * LANES:2 * GLA_QK_W + (p + 1) * LANES]
            v[u] = (v_refs[d][rows, 2 * p * LANES:(2 * p + 1) * LANES],
                    v_refs[d][rows, (2 * p + 1) * LANES:(2 * p + 2) * LANES])
            a[u] = jnp.where(incl[d], _dot_nt(qe[u], split_heads(ke)), 0.0).astype(BF16)

    def values(step):
        for u in units_of(step):
            intra[u] = _dot(a[u], cat([cat([v[u][0], zero_bf], axis=1), cat([zero_bf, v[u][1]], axis=1)],
                                      axis=0))
            kv[u] = _dot_tn(cat([v[u][0], v[u][1]], axis=1), kend[u])

    def outputs(step):
        for u in units_of(step):
            d, _, p = u
            c = chunk_of(d, step)
            inter = _dot_nt(split_heads(qe[u]), st[(d, p)].astype(BF16))
            for hh in range(2):
                hd = 2 * p + hh
                o = intra[u][:, hh * LANES:(hh + 1) * LANES] + inter[hh * CHUNK:(hh + 1) * CHUNK]
                o_refs[d][c * CHUNK:(c + 1) * CHUNK, hd * LANES:(hd + 1) * LANES] = o.astype(o_refs[d].dtype)
            dec = dec_refs[d][c, :, p * LANES:(p + 1) * LANES]
            st[(d, p)] = dec * st[(d, p)] + jnp.where(lane_sq, kv[u][0:LANES], kv[u][LANES:])

    def finish():
        for (d, p), s in st.items():
            s_ref[d, p] = s

    return scores, values, outputs, finish


def _gdn_stages(qf_ref, kf_ref, vf_ref, colf_ref, rowf_ref, qb_ref, kb_ref, vb_ref, colb_ref, rowb_ref,
                of_ref, ob_ref, s_ref):
    pk = GDN_HEADS * CHUNK
    ri = lax.broadcasted_iota(jnp.int32, (CHUNK, pk), 0)
    ci = lax.broadcasted_iota(jnp.int32, (CHUNK, pk), 1) & (CHUNK - 1)
    incl = (ri >= ci, ri <= ci)
    strict = (ri > ci, ri < ci)
    eye = jnp.where(ri == ci, 1.0, 0.0)
    blk = lambda axis: lax.shift_right_logical(lax.broadcasted_iota(jnp.int32, (pk, pk), axis), 6)
    same_blk = blk(0) == blk(1)
    bd_mask = jnp.where(same_blk, 1.0, 0.0).astype(BF16)
    left = lax.broadcasted_iota(jnp.int32, (CHUNK, LANES), 1) < CHUNK
    zero_tok = jnp.zeros((CHUNK, LANES), BF16)
    zero_sq = jnp.zeros((LANES, LANES), BF16)

    refs = ((qf_ref, kf_ref, vf_ref, colf_ref, rowf_ref, of_ref),
            (qb_ref, kb_ref, vb_ref, colb_ref, rowb_ref, ob_ref))
    groups = [(d, step) for step in range(NC) for d in range(2)]
    pairs = range(GDN_HEADS // 2)
    each = lambda fn, *lists: [fn(*args) for args in zip(*lists)]
    top = lambda m: m[0:CHUNK]
    bot = lambda m: m[CHUNK:2 * CHUNK]
    cat = jnp.concatenate
    chunk_of = lambda d, step: step if d == 0 else NC - 1 - step

    def block_diag(x):
        return cat([x, x, x, x], axis=0) * bd_mask

    def diag2(a, b, zero):
        return cat([cat([a, zero], axis=1), cat([zero, b], axis=1)], axis=0)

    q, k, v, col, gc_r = [], [], [], [], []
    for d, step in groups:
        q_ref, k_ref, v_ref, col_ref, row_ref, _ = refs[d]
        c = chunk_of(d, step)
        rows = slice(c * CHUNK, (c + 1) * CHUNK)
        heads = [slice(hd * LANES, (hd + 1) * LANES) for hd in range(GDN_HEADS)]
        q.append([q_ref[rows, h] for h in heads])
        k.append([k_ref[rows, h] for h in heads])
        v.append([v_ref[rows, h] for h in heads])
        col.append(col_ref[rows, :])
        rowt = row_ref[c]
        gc_r.append(cat([rowt[8 + d * GDN_HEADS + hd:9 + d * GDN_HEADS + hd, :] for hd in range(GDN_HEADS)],
                        axis=1))

    def colv(g, base, hd):
        lane = base + groups[g][0] * GDN_HEADS + hd
        return col[g][:, lane:lane + 1]

    def col_packed(g, base):
        tiles = [jnp.where(left, jnp.broadcast_to(colv(g, base, 2 * j), (CHUNK, LANES)),
                           jnp.broadcast_to(colv(g, base, 2 * j + 1), (CHUNK, LANES))) for j in pairs]
        return cat(tiles, axis=1)

    qk, uw = {}, {}

    def rhs_of(g, hd):
        beta = colv(g, G_BETA, hd)
        return cat([(v[g][hd].astype(F32) * beta).astype(BF16),
                    (k[g][hd].astype(F32) * (beta * colv(g, G_EGC, hd))).astype(BF16)], axis=1)

    def pre(gl):
        kq = {g: [_dot_nt(cat([cat([k[g][2 * j], k[g][2 * j + 1]], axis=1),
                               cat([q[g][2 * j], q[g][2 * j + 1]], axis=1)], axis=0),
                          diag2(k[g][2 * j], k[g][2 * j + 1], zero_tok)) for j in pairs] for g in gl}
        yield
        pm, cur = {}, {}
        for g in gl:
            d = groups[g][0]
            kk = cat([top(kq[g][j]) for j in pairs], axis=1)
            qkr = cat([bot(kq[g][j]) for j in pairs], axis=1)
            decay = jnp.where(incl[d], jnp.exp(jnp.where(incl[d], col_packed(g, G_GC) - gc_r[g], 0.0)), 0.0)
            n = jnp.where(strict[d], -(kk * decay * col_packed(g, G_BETA)), 0.0)
            qk[g] = jnp.where(incl[d], qkr * decay, 0.0).astype(BF16)
            pm[g] = eye + n
            cur[g] = _dot(n.astype(BF16), block_diag(n.astype(BF16)))
        yield
        for _ in range(4):
            st = {g: _dot(cat([pm[g].astype(BF16), cur[g].astype(BF16)], axis=0),
                          block_diag(cur[g].astype(BF16))) for g in gl}
            for g in gl:
                pm[g] = pm[g] + top(st[g])
                cur[g] = bot(st[g])
            yield
        tinv = {g: pm[g] + _dot(pm[g].astype(BF16), block_diag(cur[g].astype(BF16))) for g in gl}
        yield
        for g in gl:
            uw[g] = []
            for j in pairs:
                t = tinv[g][:, j * LANES:(j + 1) * LANES]
                lhs = cat([jnp.where(left, t, 0.0), jnp.where(left, 0.0, t)], axis=0).astype(BF16)
                uw[g].append(_dot(lhs, cat([rhs_of(g, 2 * j), rhs_of(g, 2 * j + 1)], axis=0)))
        yield

    def scan(step):
        gs = [g for g, (_, st_) in enumerate(groups) if st_ == step]
        units = [(g, j) for g in gs for j in pairs]
        s = {(g, hd): s_ref[groups[g][0], hd] for g in gs for hd in range(GDN_HEADS)}
        r = {}
        for g, j in units:
            a, b = 2 * j, 2 * j + 1
            x = uw[g][j]
            lhs = cat([cat([x[0:CHUNK, LANES:].astype(BF16), x[CHUNK:, LANES:].astype(BF16)], axis=1),
                       cat([q[g][a], q[g][b]], axis=1)], axis=0)
            r[(g, j)] = _dot(lhs, diag2(s[(g, a)].astype(BF16), s[(g, b)].astype(BF16), zero_sq))
        yield
        vn = {}
        for g, j in units:
            x, rr = uw[g][j], r[(g, j)]
            vn[(g, 2 * j)] = x[0:CHUNK, 0:LANES] - rr[0:CHUNK, 0:LANES]
            vn[(g, 2 * j + 1)] = x[CHUNK:, 0:LANES] - rr[0:CHUNK, LANES:]
        for g, j in units:
            d = groups[g][0]
            a, b = 2 * j, 2 * j + 1
            intra = _dot(qk[g][:, j * LANES:(j + 1) * LANES],
                         diag2(vn[(g, a)].astype(BF16), vn[(g, b)].astype(BF16), zero_tok))
            c = chunk_of(d, step)
            for hd, lanes in ((a, slice(0, LANES)), (b, slice(LANES, 2 * LANES))):
                o = colv(g, G_EGC, hd) * r[(g, j)][CHUNK:, lanes] + intra[:, lanes]
                o_ref = refs[d][5]
                o_ref[c * CHUNK:(c + 1) * CHUNK, hd * LANES:(hd + 1) * LANES] = o.astype(o_ref.dtype)
        for g in gs:
            d = groups[g][0]
            for hd in range(GDN_HEADS):
                dec = col[g][0:1, G_DEC + d * GDN_HEADS + hd:G_DEC + d * GDN_HEADS + hd + 1]
                kv = _dot_tn(k[g][hd], (colv(g, G_EKEND, hd) * vn[(g, hd)]).astype(BF16))
                s_ref[d, hd] = dec * s[(g, hd)] + kv
        yield

    groups_of = lambda lo, hi: [g for g, (_, st_) in enumerate(groups) if lo <= st_ < hi]
    return pre, scan, groups_of


def _scan_kernel(gf_ref, gb_ref, v1f_ref, v1b_ref, decf_ref, decb_ref,
                 qf_ref, kf_ref, vf_ref, colf_ref, rowf_ref, qb_ref, kb_ref, vb_ref, colb_ref, rowb_ref,
                 o1f_ref, o1b_ref, o2f_ref, o2b_ref, s1_ref, s2_ref):
    @pl.when(pl.program_id(1) == 0)
    def _():
        s1_ref[...] = jnp.zeros_like(s1_ref)
        s2_ref[...] = jnp.zeros_like(s2_ref)

    gla_scores, gla_values, gla_outputs, gla_finish = _gla_stages(
        gf_ref, gb_ref, v1f_ref, v1b_ref, decf_ref, decb_ref, o1f_ref, o1b_ref, s1_ref)
    pre, scan, groups_of = _gdn_stages(qf_ref, kf_ref, vf_ref, colf_ref, rowf_ref,
                                       qb_ref, kb_ref, vb_ref, colb_ref, rowb_ref, o2f_ref, o2b_ref, s2_ref)

    half = NC // 2
    for _ in pre(groups_of(0, half)):
        pass
    later = pre(groups_of(half, NC))
    for step in range(half):
        for _ in scan(step):
            next(later, None)
    for _ in later:
        pass
    gla_scores(0)
    gla_step = 0
    for step in range(half, NC):
        for _ in scan(step):
            if gla_step + 1 < NC:
                gla_scores(gla_step + 1)
            gla_values(gla_step)
            gla_outputs(gla_step)
            gla_step += 1
    assert gla_step == NC
    gla_finish()


def _mix_scan(gla, v1, dec, q2, k2, v2, gcol, grow):
    nb = SEQ // TB
    n_tok = q2.shape[0]
    fwd = lambda b, i: (b * nb + i, 0)
    bwd = lambda b, i: (b * nb + nb - 1 - i, 0)
    fwd3 = lambda b, i: (b * nb + i, 0, 0)
    bwd3 = lambda b, i: (b * nb + nb - 1 - i, 0, 0)

    def gdn_specs(m2, m3):
        return [pl.BlockSpec((TB, HW), m2), pl.BlockSpec((TB, HW), m2), pl.BlockSpec((TB, HW), m2),
                pl.BlockSpec((TB, LANES), m2), pl.BlockSpec((NC, ROW_SLAB, CHUNK), m3)]

    gla_specs = [
        pl.BlockSpec((TB, 3 * GLA_QK_W), fwd),
        pl.BlockSpec((TB, 3 * GLA_QK_W), lambda b, i: (b * nb + nb - 1 - i, 1)),
        pl.BlockSpec((TB, HW), fwd),
        pl.BlockSpec((TB, HW), bwd),
        pl.BlockSpec((NC, 1, GLA_QK_W), lambda b, i: (b * nb + i, 0, 0)),
        pl.BlockSpec((NC, 1, GLA_QK_W), lambda b, i: (b * nb + nb - 1 - i, 0, 1)),
    ]
    out = jax.ShapeDtypeStruct((n_tok, HW), BF16)
    return pl.pallas_call(
        _scan_kernel,
        grid=(BATCH, nb),
        in_specs=gla_specs + gdn_specs(fwd, fwd3) + gdn_specs(bwd, bwd3),
        out_specs=(pl.BlockSpec((TB, HW), fwd), pl.BlockSpec((TB, HW), bwd),
                   pl.BlockSpec((TB, HW), fwd), pl.BlockSpec((TB, HW), bwd)),
        out_shape=(out, out, out, out),
        scratch_shapes=[pltpu.VMEM((2, GLA_HEADS // 2, LANES, LANES), F32),
                        pltpu.VMEM((2, GDN_HEADS, LANES, LANES), F32)],
        compiler_params=pltpu.CompilerParams(dimension_semantics=("parallel", "arbitrary"),
                                             vmem_limit_bytes=VMEM_LIMIT),
        name="mix_scan",
    )(gla, gla, v1, v1, dec, dec, q2, k2, v2, gcol, grow, q2, k2, v2, gcol, grow)


def _outproj_kernel(o1f_ref, o1b_ref, o2f_ref, o2b_ref, zg_ref, xaq_ref, mkv_ref, x_ref, wout_ref,
                    n1_ref, n2_ref, n3_ref, fn_ref, out_ref, *, final):
    def head_norm(o, nw_ref, width):
        parts = []
        for hd in range(HEADS):
            s = o[:, hd * LANES:(hd + 1) * LANES]
            ms = jnp.sum(s * s, axis=-1, keepdims=True) * (1.0 / width)
            parts.append(s * lax.rsqrt(ms + NORM_EPS))
        return jnp.concatenate(parts, axis=-1) * nw_ref[...]

    both = lambda f_ref, b_ref: f_ref[...].astype(F32) + b_ref[...].astype(F32)
    gate = lambda lo, width: zg_ref[:, lo:lo + width].astype(F32)
    lane_q = lax.broadcasted_iota(jnp.int32, (TM_OUT, LANES), 1)
    lane_m = lax.broadcasted_iota(jnp.int32, (MEM_LEN, LANES), 1)
    first_q = lane_q < XA_DH
    first_m = lane_m < XA_DH
    q_head = (first_q, lane_q >= XA_DH)
    m_head = (first_m, lane_m >= XA_DH)
    heads = [(p, hh) for p in range(XA_HEADS // 2) for hh in range(2)]

    sc = []
    for p, hh in heads:
        qpair = xaq_ref[:, p * LANES:(p + 1) * LANES]
        qm = jnp.where(q_head[hh], qpair, jnp.zeros_like(qpair))
        sc.append(_dot_nt(qm, mkv_ref[0, :, p * LANES:(p + 1) * LANES]))

    o1 = head_norm(both(o1f_ref, o1b_ref), n1_ref, GLA_DV) * gate(0, HW)
    y = _dot(o1.astype(BF16), wout_ref[0:HW, :])

    pv = []
    for (p, hh), s in zip(heads, sc):
        e = jnp.exp(s - jnp.max(s, axis=-1, keepdims=True))
        l = jnp.sum(e, axis=-1, keepdims=True)
        mv = mkv_ref[0, :, XA_W + p * LANES:XA_W + (p + 1) * LANES]
        mvm = jnp.where(m_head[hh], mv, jnp.zeros_like(mv))
        pv.append(_dot(e.astype(BF16), mvm) * (1.0 / l))

    o2 = head_norm(both(o2f_ref, o2b_ref), n2_ref, GDN_DV) * gate(HW, HW)
    y = y + _dot(o2.astype(BF16), wout_ref[HW:2 * HW, :])

    for p in range(XA_HEADS // 2):
        lanes = slice(p * LANES, (p + 1) * LANES)
        acc = pv[2 * p] + pv[2 * p + 1]
        sq = acc * acc
        ss0 = jnp.sum(jnp.where(first_q, sq, 0.0), axis=-1, keepdims=True)
        ss1 = jnp.sum(jnp.where(first_q, 0.0, sq), axis=-1, keepdims=True)
        ms = jnp.where(first_q, ss0, ss1) * (1.0 / XA_DH)
        o3 = acc * lax.rsqrt(ms + NORM_EPS) * n3_ref[:, lanes] * gate(2 * HW + p * LANES, LANES)
        y = y + _dot(o3.astype(BF16), wout_ref[2 * HW + p * LANES:2 * HW + (p + 1) * LANES, :])

    xo = x_ref[...] + y
    if final:
        ms = jnp.mean(xo * xo, axis=-1, keepdims=True)
        xo = xo * lax.rsqrt(ms + NORM_EPS) * fn_ref[...]
    out_ref[...] = xo


def _outproj(layer, o1f, o1b, o2f, o2b, zg, xaq, mkv, x2d, wout, n1, n2, n3, fnw):
    n_tok = x2d.shape[0]
    blocks_per_seq = SEQ // TM_OUT
    tok = lambda w: pl.BlockSpec((TM_OUT, w), lambda i: (i, 0))
    const = lambda shape: pl.BlockSpec((None,) + shape, lambda i: (layer,) + tuple(0 for _ in shape))
    return pl.pallas_call(
        functools.partial(_outproj_kernel, final=layer == DEPTH - 1),
        grid=(n_tok // TM_OUT,),
        in_specs=[
            tok(HW), tok(HW), tok(HW), tok(HW), tok(MIX_PAD_W), tok(XA_W),
            pl.BlockSpec((None, 1, MEM_LEN, 2 * XA_W), lambda i: (layer, i // blocks_per_seq, 0, 0)),
            tok(D_MODEL),
            const((MIX_PAD_W, D_MODEL)),
            const((1, HW)), const((1, HW)), const((1, XA_W)),
            pl.BlockSpec((1, D_MODEL), lambda i: (0, 0)),
        ],
        out_specs=tok(D_MODEL),
        out_shape=jax.ShapeDtypeStruct((n_tok, D_MODEL), F32),
        compiler_params=pltpu.CompilerParams(dimension_semantics=("parallel",),
                                             vmem_limit_bytes=VMEM_LIMIT),
        name="outproj",
    )(o1f, o1b, o2f, o2b, zg, xaq, mkv, x2d, wout, n1, n2, n3, fnw)


def _pad_heads(w, axis):
    shp = w.shape
    w = w.reshape(shp[:axis] + (HEADS, GLA_DV) + shp[axis + 1:])
    pad = [(0, 0)] * w.ndim
    pad[axis + 1] = (0, LANES - GLA_DV)
    w = jnp.pad(w, pad)
    return w.reshape(shp[:axis] + (HW,) + shp[axis + 1:])


_SRC_SIZES = (GLA_QK_W, GLA_QK_W, HEADS * GLA_DV, HEADS * GLA_DV, 2 * GLA_RANK, 3 * HEADS * GDN_DK,
              HEADS * GDN_DV, 2 * GDN_HEADS, 2 * GDN_HEADS, XA_W, XA_W)
(S_GQ, S_GK, S_GV, S_GZ, S_LR, S_DQKV, S_DZ, S_DB, S_DA, S_XQ, S_XZ) = (
    sum(_SRC_SIZES[:n]) for n in range(len(_SRC_SIZES)))
IN_W = sum(_SRC_SIZES)
REPACK_ROWS = 256


def _repack_kernel(w_ref, out_ref):
    def src(lo, width):
        return w_ref[:, lo:lo + width].astype(BF16)

    rows = w_ref.shape[0]
    zeros = lambda width: jnp.zeros((rows, width), BF16)

    def put(dst, piece):
        out_ref[:, dst:dst + piece.shape[1]] = piece

    def put_heads(dst, lo):
        for hd in range(HEADS):
            put(dst + hd * LANES, jnp.concatenate([src(lo + hd * GLA_DV, GLA_DV), zeros(LANES - GLA_DV)], axis=1))

    put(C_GQ, src(S_GQ, GLA_QK_W))
    put(C_GK, src(S_GK, GLA_QK_W))
    put_heads(C_GV, S_GV)
    for part in range(3):
        put_heads(C_DQKV + part * HW, S_DQKV + part * HEADS * GDN_DK)
    put_heads(C_Z, S_GZ)
    put_heads(C_Z + HW, S_DZ)
    put(C_Z + 2 * HW, src(S_XZ, XA_W))
    put(C_XQ, src(S_XQ, XA_W))
    da = src(S_DA, 2 * GDN_HEADS)
    put(C_MISC, jnp.concatenate(
        [src(S_DB, 2 * GDN_HEADS)] + [da] * M_A_COPIES
        + [zeros(M_LR - M_A - 8 * M_A_COPIES), src(S_LR, 2 * GLA_RANK), zeros(LANES - M_LR - 2 * GLA_RANK)],
        axis=1))


def _repack_w_in(w_in):
    return pl.pallas_call(
        _repack_kernel,
        grid=(DEPTH, D_MODEL // REPACK_ROWS),
        in_specs=[pl.BlockSpec((None, REPACK_ROWS, IN_W), lambda l, r: (l, r, 0))],
        out_specs=pl.BlockSpec((None, REPACK_ROWS, IN_PAD_W), lambda l, r: (l, r, 0)),
        out_shape=jax.ShapeDtypeStruct((DEPTH, D_MODEL, IN_PAD_W), BF16),
        compiler_params=pltpu.CompilerParams(dimension_semantics=("parallel", "parallel")),
        name="repack_w_in",
    )(w_in)


def _pack_params(norm_w, w_in, gla_w2, gla_b, gla_norm_w, gdn_conv_w, gdn_a_log, gdn_dt_bias, gdn_norm_w,
                 xa_norm_w, w_out):
    hd = HEADS * GDN_DK
    zeros = lambda *shape: jnp.zeros((DEPTH,) + shape, F32)
    w_all = _repack_w_in(w_in)

    cw = jnp.transpose(gdn_conv_w, (0, 2, 1))
    cw = jnp.concatenate([_pad_heads(cw[:, :, 0:hd], 2), _pad_heads(cw[:, :, hd:2 * hd], 2),
                          _pad_heads(cw[:, :, 2 * hd:], 2)], axis=2)
    cw = jnp.pad(cw, ((0, 0), (0, 8 - GDN_CONV), (0, 0)))

    w2bd = jnp.concatenate(
        [zeros(M_LR, 2 * GLA_QK_W),
         jnp.concatenate([gla_w2[:, 0], zeros(GLA_RANK, GLA_QK_W)], axis=2),
         jnp.concatenate([zeros(GLA_RANK, GLA_QK_W), gla_w2[:, 1]], axis=2),
         zeros(LANES - M_LR - 2 * GLA_RANK, 2 * GLA_QK_W)], axis=1).astype(BF16)
    glab = gla_b.reshape(DEPTH, 1, 2 * GLA_QK_W)

    def a_slab(p):
        flat = p.reshape(DEPTH, 1, 2 * GDN_HEADS)
        return jnp.concatenate([zeros(1, M_A)] + [flat] * M_A_COPIES + [zeros(1, LANES - M_A - 8 * M_A_COPIES)],
                               axis=2)

    gparams = jnp.concatenate([a_slab(gdn_a_log), a_slab(gdn_dt_bias), zeros(6, LANES)], axis=1)

    wout = jnp.concatenate([_pad_heads(w_out[:, 0:HEADS * GLA_DV], 1),
                            _pad_heads(w_out[:, HEADS * GLA_DV:HEADS * (GLA_DV + GDN_DV)], 1),
                            w_out[:, HEADS * (GLA_DV + GDN_DV):]], axis=1).astype(BF16)
    pad_norm = lambda w: jnp.tile(jnp.pad(w, ((0, 0), (0, LANES - w.shape[1]))), (1, HEADS)).reshape(DEPTH, 1, HW)
    n1 = pad_norm(gla_norm_w)
    n2 = pad_norm(gdn_norm_w)
    n3 = jnp.tile(xa_norm_w, (1, XA_HEADS)).reshape(DEPTH, 1, XA_W)
    return norm_w.reshape(DEPTH, 1, D_MODEL), w_all, cw, w2bd, glab, gparams, wout, n1, n2, n3


def kernel(x, mem, norm_w, w_in, gla_w2, gla_b, gla_norm_w, gdn_conv_w, gdn_a_log, gdn_dt_bias,
           gdn_norm_w, mem_norm_w, xa_w_kv, xa_norm_w, w_out, final_norm_w):
    assert x.shape == (BATCH, SEQ, D_MODEL) and mem.shape == (BATCH, MEM_LEN, D_MODEL)
    mkv = _memkv(mem, mem_norm_w, xa_w_kv.astype(BF16))
    nw, w_all, cw, w2bd, glab, gparams, wout, n1, n2, n3 = _pack_params(
        norm_w, w_in, gla_w2, gla_b, gla_norm_w, gdn_conv_w, gdn_a_log, gdn_dt_bias, gdn_norm_w,
        xa_norm_w, w_out)
    h = x.reshape(BATCH * SEQ, D_MODEL)
    fnw = final_norm_w.reshape(1, D_MODEL)
    for l in range(DEPTH):
        gla, v1, dec, zg, q2, k2, v2, gcol, grow, xaq = _inproj(l, h, nw, w_all, cw, w2bd, glab, gparams)
        o1f, o1b, o2f, o2b = _mix_scan(gla, v1, dec, q2, k2, v2, gcol, grow)
        h = _outproj(l, o1f, o1b, o2f, o2b, zg, xaq, mkv, h, wout, n1, n2, n3, fnw)
    return h.reshape(BATCH, SEQ, D_MODEL)
```

```python
import functools

import jax
import jax.numpy as jnp
from jax import lax
from jax.experimental import pallas as pl
from jax.experimental.pallas import tpu as pltpu

F32 = jnp.float32
BF16 = jnp.bfloat16

D_MODEL = 1024
BATCH = 8
SEQ = 4096
DEPTH = 2
MEM_LEN = 256
CHUNK = 64
NORM_EPS = 1e-6
GLA_HEADS = 4
GLA_DK = 64
GLA_DV = 96
GLA_RANK = 16
GLA_GATE_NORMALIZER = 16.0
GDN_HEADS = 4
GDN_DK = 96
GDN_DV = 96
GDN_CONV = 5
XA_HEADS = 4
XA_DH = 64

LANES = 128
HEADS = 4
HW = HEADS * LANES
GLA_QK_W = GLA_HEADS * GLA_DK
XA_W = XA_HEADS * XA_DH
MIX_PAD_W = 2 * HW + XA_W

C_GQ = 0
C_GK = C_GQ + GLA_QK_W
C_GV = C_GK + GLA_QK_W
C_DQKV = C_GV + HW
C_Z = C_DQKV + 3 * HW
C_XQ = C_Z + MIX_PAD_W
IN_PAD_W = C_XQ + XA_W
MISC_PER_TILE = LANES - GLA_DV

M_BETA = 0
M_A = 8
M_A_COPIES = 4
M_LR = 64
G_BETA, G_GC, G_EGC, G_EKEND, G_DEC = 0, 8, 16, 24, 32
ROW_SLAB = 16

TM_IN = 512
TM_OUT = 512
NC = 8
TB = NC * CHUNK
HALO = 8
CONV_PHASES = 4
VMEM_LIMIT = 56 * 1024 * 1024


def _dot(a, b):
    return jnp.dot(a, b, preferred_element_type=F32)


def _dot_nt(a, b):
    return lax.dot_general(a, b, (((1,), (1,)), ((), ())), preferred_element_type=F32)


def _dot_tn(a, b):
    return lax.dot_general(a, b, (((0,), (0,)), ((), ())), preferred_element_type=F32)


def _sigmoid(x):
    return 1.0 / (1.0 + jnp.exp(-x))


def _silu(x):
    return x * _sigmoid(x)


def _softplus(x):
    return jnp.maximum(x, 0.0) + jnp.log(1.0 + jnp.exp(-jnp.abs(x)))


def _log_sigmoid(x):
    return jnp.minimum(x, 0.0) - jnp.log(1.0 + jnp.exp(-jnp.abs(x)))


def _split2(x):
    hi = pltpu.bitcast(pltpu.bitcast(x, jnp.int32) & jnp.int32(-65536), F32)
    return hi.astype(BF16), (x - hi).astype(BF16)


def _tri_sum(tri, x):
    hi, lo = _split2(x)
    return _dot(tri, hi) + _dot(tri, lo)


def _tri_masks():
    ri = lax.broadcasted_iota(jnp.int32, (CHUNK, CHUNK), 0)
    ci = lax.broadcasted_iota(jnp.int32, (CHUNK, CHUNK), 1)
    return ri, ci


def _memkv_kernel(mem_ref, nw_ref, w_ref, out_ref):
    m = mem_ref[0]
    ms = jnp.mean(m * m, axis=-1, keepdims=True)
    mn = (m * lax.rsqrt(ms + NORM_EPS) * nw_ref[0]).astype(BF16)
    out_ref[0, 0] = _dot(mn, w_ref[0]).astype(BF16)


def _memkv(mem, mem_norm_w, xa_w_kv_bf16):
    return pl.pallas_call(
        _memkv_kernel,
        grid=(DEPTH, BATCH),
        in_specs=[
            pl.BlockSpec((1, MEM_LEN, D_MODEL), lambda l, b: (b, 0, 0)),
            pl.BlockSpec((1, 1, D_MODEL), lambda l, b: (l, 0, 0)),
            pl.BlockSpec((1, D_MODEL, 2 * XA_W), lambda l, b: (l, 0, 0)),
        ],
        out_specs=pl.BlockSpec((1, 1, MEM_LEN, 2 * XA_W), lambda l, b: (l, b, 0, 0)),
        out_shape=jax.ShapeDtypeStruct((DEPTH, BATCH, MEM_LEN, 2 * XA_W), BF16),
        compiler_params=pltpu.CompilerParams(dimension_semantics=("parallel", "parallel")),
        name="memkv",
    )(mem, mem_norm_w.reshape(DEPTH, 1, D_MODEL), xa_w_kv_bf16)


def _inproj_kernel(x_ref, xp_ref, xn_ref, nw_ref, w_ref, cw_ref, w2_ref, gb_ref, gp_ref,
                   gla_ref, v1_ref, dec_ref, zg_ref, q2_ref, k2_ref, v2_ref, gcol_ref, grow_ref,
                   xaq_ref, ext_ref, conv_ref):
    nw = nw_ref[...]

    def norm(x):
        ms = jnp.mean(x * x, axis=-1, keepdims=True)
        return (x * lax.rsqrt(ms + NORM_EPS) * nw).astype(BF16)

    h = norm(x_ref[...])
    h_halo = norm(jnp.concatenate([xp_ref[...], xn_ref[...]], axis=0))

    def proj(hh, lo, width):
        return _dot(hh, w_ref[:, lo:lo + width])

    blocks_per_seq = SEQ // TM_IN
    j = lax.rem(pl.program_id(0), blocks_per_seq)
    cw = cw_ref[...]
    base = HALO - GDN_CONV // 2

    def gdn_mm(grp):
        lo = C_DQKV + grp * HW
        halo = proj(h_halo, lo, HW)
        return proj(h, lo, HW), halo[0:HALO], halo[HALO:]

    def gdn_vpu(grp, res):
        main, prev, nxt = res
        for hd in range(HEADS):
            slab = grp * HEADS + hd
            lanes = slice(hd * LANES, (hd + 1) * LANES)
            ext_ref[slab, 0:HALO, :] = jnp.where(j == 0, 0.0, prev[:, lanes])
            ext_ref[slab, HALO:HALO + TM_IN, :] = main[:, lanes]
            ext_ref[slab, HALO + TM_IN:2 * HALO + TM_IN, :] = jnp.where(j == blocks_per_seq - 1, 0.0,
                                                                     nxt[:, lanes])
            taps = cw[:, grp * HW + hd * LANES:grp * HW + (hd + 1) * LANES]
            for p in range(CONV_PHASES):
                acc = None
                for t in range(GDN_CONV):
                    win = ext_ref[slab, pl.ds(base + p + t, TM_IN // CONV_PHASES, stride=CONV_PHASES), :]
                    acc = win * taps[t:t + 1, :] if acc is None else acc + win * taps[t:t + 1, :]
                conv_ref[slab, pl.ds(p, TM_IN // CONV_PHASES, stride=CONV_PHASES), :] = acc
            y = _silu(conv_ref[slab])
            if grp == 2:
                v2_ref[:, lanes] = y.astype(BF16)
            else:
                ref, scale = ((q2_ref, GDN_DK ** -0.5), (k2_ref, 1.0))[grp]
                ss = jnp.sum(y * y, axis=-1, keepdims=True)
                ref[:, lanes] = (y * lax.rsqrt(ss + NORM_EPS) * scale).astype(BF16)

    def z_vpu(lo, width, res):
        zg_ref[:, lo:lo + width] = _silu(res).astype(BF16)

    def gla_z_and_misc_vpu(res):
        z_vpu(0, HW, res)
        misc_vpu(jnp.concatenate([res[:, hd * LANES + GLA_DV:(hd + 1) * LANES] for hd in range(HEADS)], axis=1))

    def xaq_vpu(res):
        xaq_ref[...] = (res * (XA_DH ** -0.5)).astype(BF16)

    lane = lax.broadcasted_iota(jnp.int32, (1, LANES), 1)
    fwd_lane = lax.rem(lane, 8) < GDN_HEADS
    ri, ci = _tri_masks()
    lower = jnp.where(ri >= ci, 1.0, 0.0).astype(BF16)
    upper = jnp.where(ri <= ci, 1.0, 0.0).astype(BF16)
    lower_upper = jnp.concatenate([lower, upper], axis=0)
    gate = {}

    def misc_vpu(m):
        logits = _dot(m.astype(BF16), w2_ref[...]) + gb_ref[...]
        gate["g"] = _log_sigmoid(logits) * (1.0 / GLA_GATE_NORMALIZER)
        is_a = (lane >= M_A) & (lane < M_A + 8 * M_A_COPIES)
        neg_a = jnp.where(is_a, -jnp.exp(gp_ref[0:1, :]), 0.0)
        gg = neg_a * _softplus(m + gp_ref[1:2, :])
        beta = _sigmoid(m)
        for c in range(TM_IN // CHUNK):
            rows = slice(c * CHUNK, (c + 1) * CHUNK)
            ggc = gg[rows]
            hi, lo = _split2(ggc)
            both = _dot(lower_upper, jnp.concatenate([hi, lo], axis=1))
            pf = both[0:CHUNK, 0:LANES] + both[0:CHUNK, LANES:]
            sf = both[CHUNK:, 0:LANES] + both[CHUNK:, LANES:]
            gc = jnp.where(fwd_lane, pf, sf)
            last = jnp.where(fwd_lane, pf[CHUNK - 1:CHUNK], sf[0:1])
            col = jnp.where(lane < G_GC, beta[rows],
                  jnp.where(lane < G_EGC, gc,
                  jnp.where(lane < G_EKEND, jnp.exp(gc),
                  jnp.where(lane < G_DEC, jnp.exp(last - gc), jnp.exp(last)))))
            gcol_ref[rows, :] = col
            grow_ref[c] = col.T[0:ROW_SLAB, :]

    def gla_v_vpu(res):
        v1_ref[...] = res.astype(BF16)

    def gla_qk_vpu(pg):
        q1 = pg[:, 0:GLA_QK_W] * (GLA_DK ** -0.5)
        k1 = pg[:, GLA_QK_W:2 * GLA_QK_W]
        g = gate["g"]
        for c in range(TM_IN // CHUNK):
            rows = slice(c * CHUNK, (c + 1) * CHUNK)
            gch = g[rows]
            q1c = q1[rows]
            k1c = k1[rows]
            bf = _tri_sum(lower, gch[:, 0:GLA_QK_W])
            br = _tri_sum(upper, gch[:, GLA_QK_W:])
            for d, (b, last) in enumerate(((bf, bf[CHUNK - 1:CHUNK]), (br, br[0:1]))):
                off = d * 3 * GLA_QK_W
                gla_ref[rows, off:off + GLA_QK_W] = (q1c * jnp.exp(b)).astype(BF16)
                gla_ref[rows, off + GLA_QK_W:off + 2 * GLA_QK_W] = (k1c * jnp.exp(-b)).astype(BF16)
                gla_ref[rows, off + 2 * GLA_QK_W:off + 3 * GLA_QK_W] = (k1c * jnp.exp(last - b)).astype(BF16)
                dec_ref[c, :, d * GLA_QK_W:(d + 1) * GLA_QK_W] = jnp.exp(last)

    part = functools.partial
    tasks = [
        (part(proj, h, C_Z, HW), gla_z_and_misc_vpu),
        (part(gdn_mm, 0), part(gdn_vpu, 0)),
        (part(gdn_mm, 1), part(gdn_vpu, 1)),
        (part(proj, h, C_Z + HW, HW), part(z_vpu, HW, HW)),
        (part(gdn_mm, 2), part(gdn_vpu, 2)),
        (part(proj, h, C_Z + 2 * HW, XA_W), part(z_vpu, 2 * HW, XA_W)),
        (part(proj, h, C_GQ, 2 * GLA_QK_W), gla_qk_vpu),
        (part(proj, h, C_GV, HW), gla_v_vpu),
        (part(proj, h, C_XQ, XA_W), xaq_vpu),
    ]
    res = tasks[0][0]()
    for t, (_, vpu) in enumerate(tasks):
        nxt_res = tasks[t + 1][0]() if t + 1 < len(tasks) else None
        vpu(res)
        res = nxt_res


def _inproj(layer, x2d, norm_w, w_all, conv_w, w2bd, gla_b, gdn_params):
    n_tok = x2d.shape[0]
    nblk = n_tok // TM_IN
    halo_blocks = TM_IN // HALO
    n_halo = n_tok // HALO
    nchunk = TM_IN // CHUNK
    tok = lambda w: pl.BlockSpec((TM_IN, w), lambda i: (i, 0))
    const = lambda shape: pl.BlockSpec((None,) + shape, lambda i: (layer,) + tuple(0 for _ in shape))
    out_shapes = (
        jax.ShapeDtypeStruct((n_tok, 6 * GLA_QK_W), BF16),
        jax.ShapeDtypeStruct((n_tok, HW), BF16),
        jax.ShapeDtypeStruct((n_tok // CHUNK, 1, 2 * GLA_QK_W), F32),
        jax.ShapeDtypeStruct((n_tok, MIX_PAD_W), BF16),
        jax.ShapeDtypeStruct((n_tok, HW), BF16),
        jax.ShapeDtypeStruct((n_tok, HW), BF16),
        jax.ShapeDtypeStruct((n_tok, HW), BF16),
        jax.ShapeDtypeStruct((n_tok, LANES), F32),
        jax.ShapeDtypeStruct((n_tok // CHUNK, ROW_SLAB, CHUNK), F32),
        jax.ShapeDtypeStruct((n_tok, XA_W), BF16),
    )
    out_specs = (
        tok(6 * GLA_QK_W), tok(HW),
        pl.BlockSpec((nchunk, 1, 2 * GLA_QK_W), lambda i: (i, 0, 0)),
        tok(MIX_PAD_W), tok(HW), tok(HW), tok(HW), tok(LANES),
        pl.BlockSpec((nchunk, ROW_SLAB, CHUNK), lambda i: (i, 0, 0)),
        tok(XA_W),
    )
    return pl.pallas_call(
        _inproj_kernel,
        grid=(nblk,),
        in_specs=[
            tok(D_MODEL),
            pl.BlockSpec((HALO, D_MODEL), lambda i: (jnp.maximum(i * halo_blocks - 1, 0), 0)),
            pl.BlockSpec((HALO, D_MODEL), lambda i: (jnp.minimum((i + 1) * halo_blocks, n_halo - 1), 0)),
            const((1, D_MODEL)),
            const((D_MODEL, IN_PAD_W)),
            const((8, 3 * HW)),
            const((LANES, 2 * GLA_QK_W)),
            const((1, 2 * GLA_QK_W)),
            const((8, LANES)),
        ],
        out_specs=out_specs,
        out_shape=out_shapes,
        scratch_shapes=[pltpu.VMEM((3 * HEADS, TM_IN + 2 * HALO, LANES), F32),
                        pltpu.VMEM((3 * HEADS, TM_IN, LANES), F32)],
        compiler_params=pltpu.CompilerParams(dimension_semantics=("parallel",),
                                             vmem_limit_bytes=VMEM_LIMIT),
        name="inproj",
    )(x2d, x2d, x2d, norm_w, w_all, conv_w, w2bd, gla_b, gdn_params)


def _gla_stages(gf_ref, gb_ref, vf_ref, vb_ref, decf_ref, decb_ref, of_ref, ob_ref, s_ref):
    ri = lax.broadcasted_iota(jnp.int32, (CHUNK, LANES), 0)
    lane_tok = lax.broadcasted_iota(jnp.int32, (CHUNK, LANES), 1)
    ci = lane_tok & (CHUNK - 1)
    incl = (ri >= ci, ri <= ci)
    left = lane_tok < GLA_DK
    lane_sq = lax.broadcasted_iota(jnp.int32, (LANES, LANES), 1) < GLA_DK
    zero_bf = jnp.zeros((CHUNK, LANES), BF16)
    cat = jnp.concatenate

    def split_heads(x):
        return cat([jnp.where(left, x, zero_bf), jnp.where(left, zero_bf, x)], axis=0)

    g_refs, v_refs = (gf_ref, gb_ref), (vf_ref, vb_ref)
    dec_refs, o_refs = (decf_ref, decb_ref), (of_ref, ob_ref)
    chunk_of = lambda d, step: step if d == 0 else NC - 1 - step
    units_of = lambda step: [(d, step, p) for d in range(2) for p in range(GLA_HEADS // 2)]
    qe, kend, v, a, intra, kv = {}, {}, {}, {}, {}, {}
    st = {(d, p): s_ref[d, p] for d in range(2) for p in range(GLA_HEADS // 2)}

    def scores(step):
        for u in units_of(step):
            d, _, p = u
            c = chunk_of(d, step)
            rows = slice(c * CHUNK, (c + 1) * CHUNK)
            qe[u] = g_refs[d][rows, p * LANES:(p + 1) * LANES]
            ke = g_refs[d][rows, GLA_QK_W + p * LANES:GLA_QK_W + (p + 1) * LANES]
            kend[u] = g_refs[d][rows, 2 * GLA_QK_W + p * LANES:2 * GLA_QK_W + (p + 1) * LANES]
            v[u] = (v_refs[d][rows, 2 * p * LANES:(2 * p + 1) * LANES],
                    v_refs[d][rows, (2 * p + 1) * LANES:(2 * p + 2) * LANES])
            a[u] = jnp.where(incl[d], _dot_nt(qe[u], split_heads(ke)), 0.0).astype(BF16)

    def values(step):
        for u in units_of(step):
            intra[u] = _dot(a[u], cat([cat([v[u][0], zero_bf], axis=1), cat([zero_bf, v[u][1]], axis=1)],
                                      axis=0))
            kv[u] = _dot_tn(cat([v[u][0], v[u][1]], axis=1), kend[u])

    def outputs(step):
        for u in units_of(step):
            d, _, p = u
            c = chunk_of(d, step)
            inter = _dot_nt(split_heads(qe[u]), st[(d, p)].astype(BF16))
            for hh in range(2):
                hd = 2 * p + hh
                o = intra[u][:, hh * LANES:(hh + 1) * LANES] + inter[hh * CHUNK:(hh + 1) * CHUNK]
                o_refs[d][c * CHUNK:(c + 1) * CHUNK, hd * LANES:(hd + 1) * LANES] = o.astype(o_refs[d].dtype)
            dec = dec_refs[d][c, :, p * LANES:(p + 1) * LANES]
            st[(d, p)] = dec * st[(d, p)] + jnp.where(lane_sq, kv[u][0:LANES], kv[u][LANES:])

    def finish():
        for (d, p), s in st.items():
            s_ref[d, p] = s

    return scores, values, outputs, finish


def _gdn_stages(qf_ref, kf_ref, vf_ref, colf_ref, rowf_ref, qb_ref, kb_ref, vb_ref, colb_ref, rowb_ref,
                of_ref, ob_ref, s_ref):
    pk = GDN_HEADS * CHUNK
    ri = lax.broadcasted_iota(jnp.int32, (CHUNK, pk), 0)
    ci = lax.broadcasted_iota(jnp.int32, (CHUNK, pk), 1) & (CHUNK - 1)
    incl = (ri >= ci, ri <= ci)
    strict = (ri > ci, ri < ci)
    eye = jnp.where(ri == ci, 1.0, 0.0)
    blk = lambda axis: lax.shift_right_logical(lax.broadcasted_iota(jnp.int32, (pk, pk), axis), 6)
    same_blk = blk(0) == blk(1)
    bd_mask = jnp.where(same_blk, 1.0, 0.0).astype(BF16)
    left = lax.broadcasted_iota(jnp.int32, (CHUNK, LANES), 1) < CHUNK
    zero_tok = jnp.zeros((CHUNK, LANES), BF16)
    zero_sq = jnp.zeros((LANES, LANES), BF16)

    refs = ((qf_ref, kf_ref, vf_ref, colf_ref, rowf_ref, of_ref),
            (qb_ref, kb_ref, vb_ref, colb_ref, rowb_ref, ob_ref))
    groups = [(d, step) for step in range(NC) for d in range(2)]
    pairs = range(GDN_HEADS // 2)
    each = lambda fn, *lists: [fn(*args) for args in zip(*lists)]
    top = lambda m: m[0:CHUNK]
    bot = lambda m: m[CHUNK:2 * CHUNK]
    cat = jnp.concatenate
    chunk_of = lambda d, step: step if d == 0 else NC - 1 - step

    def block_diag(x):
        return cat([x, x, x, x], axis=0) * bd_mask

    def diag2(a, b, zero):
        return cat([cat([a, zero], axis=1), cat([zero, b], axis=1)], axis=0)

    q, k, v, col, gc_r = [], [], [], [], []
    for d, step in groups:
        q_ref, k_ref, v_ref, col_ref, row_ref, _ = refs[d]
        c = chunk_of(d, step)
        rows = slice(c * CHUNK, (c + 1) * CHUNK)
        heads = [slice(hd * LANES, (hd + 1) * LANES) for hd in range(GDN_HEADS)]
        q.append([q_ref[rows, h] for h in heads])
        k.append([k_ref[rows, h] for h in heads])
        v.append([v_ref[rows, h] for h in heads])
        col.append(col_ref[rows, :])
        rowt = row_ref[c]
        gc_r.append(cat([rowt[8 + d * GDN_HEADS + hd:9 + d * GDN_HEADS + hd, :] for hd in range(GDN_HEADS)],
                        axis=1))

    def colv(g, base, hd):
        lane = base + groups[g][0] * GDN_HEADS + hd
        return col[g][:, lane:lane + 1]

    def col_packed(g, base):
        tiles = [jnp.where(left, jnp.broadcast_to(colv(g, base, 2 * j), (CHUNK, LANES)),
                           jnp.broadcast_to(colv(g, base, 2 * j + 1), (CHUNK, LANES))) for j in pairs]
        return cat(tiles, axis=1)

    qk, uw = {}, {}

    def rhs_of(g, hd):
        beta = colv(g, G_BETA, hd)
        return cat([(v[g][hd].astype(F32) * beta).astype(BF16),
                    (k[g][hd].astype(F32) * (beta * colv(g, G_EGC, hd))).astype(BF16)], axis=1)

    def pre(gl):
        kq = {g: [_dot_nt(cat([cat([k[g][2 * j], k[g][2 * j + 1]], axis=1),
                               cat([q[g][2 * j], q[g][2 * j + 1]], axis=1)], axis=0),
                          diag2(k[g][2 * j], k[g][2 * j + 1], zero_tok)) for j in pairs] for g in gl}
        yield
        pm, cur = {}, {}
        for g in gl:
            d = groups[g][0]
            kk = cat([top(kq[g][j]) for j in pairs], axis=1)
            qkr = cat([bot(kq[g][j]) for j in pairs], axis=1)
            decay = jnp.where(incl[d], jnp.exp(jnp.where(incl[d], col_packed(g, G_GC) - gc_r[g], 0.0)), 0.0)
            n = jnp.where(strict[d], -(kk * decay * col_packed(g, G_BETA)), 0.0)
            qk[g] = jnp.where(incl[d], qkr * decay, 0.0).astype(BF16)
            pm[g] = eye + n
            cur[g] = _dot(n.astype(BF16), block_diag(n.astype(BF16)))
        yield
        for _ in range(4):
            st = {g: _dot(cat([pm[g].astype(BF16), cur[g].astype(BF16)], axis=0),
                          block_diag(cur[g].astype(BF16))) for g in gl}
            for g in gl:
                pm[g] = pm[g] + top(st[g])
                cur[g] = bot(st[g])
            yield
        tinv = {g: pm[g] + _dot(pm[g].astype(BF16), block_diag(cur[g].astype(BF16))) for g in gl}
        yield
        for g in gl:
            uw[g] = []
            for j in pairs:
                t = tinv[g][:, j * LANES:(j + 1) * LANES]
                lhs = cat([jnp.where(left, t, 0.0), jnp.where(left, 0.0, t)], axis=0).astype(BF16)
                uw[g].append(_dot(lhs, cat([rhs_of(g, 2 * j), rhs_of(g, 2 * j + 1)], axis=0)))
        yield

    def scan(step):
        gs = [g for g, (_, st_) in enumerate(groups) if st_ == step]
        units = [(g, j) for g in gs for j in pairs]
        s = {(g, hd): s_ref[groups[g][0], hd] for g in gs for hd in range(GDN_HEADS)}
        r = {}
        for g, j in units:
            a, b = 2 * j, 2 * j + 1
            x = uw[g][j]
            lhs = cat([cat([x[0:CHUNK, LANES:].astype(BF16), x[CHUNK:, LANES:].astype(BF16)], axis=1),
                       cat([q[g][a], q[g][b]], axis=1)], axis=0)
            r[(g, j)] = _dot(lhs, diag2(s[(g, a)].astype(BF16), s[(g, b)].astype(BF16), zero_sq))
        yield
        vn = {}
        for g, j in units:
            x, rr = uw[g][j], r[(g, j)]
            vn[(g, 2 * j)] = x[0:CHUNK, 0:LANES] - rr[0:CHUNK, 0:LANES]
            vn[(g, 2 * j + 1)] = x[CHUNK:, 0:LANES] - rr[0:CHUNK, LANES:]
        for g, j in units:
            d = groups[g][0]
            a, b = 2 * j, 2 * j + 1
            intra = _dot(qk[g][:, j * LANES:(j + 1) * LANES],
                         diag2(vn[(g, a)].astype(BF16), vn[(g, b)].astype(BF16), zero_tok))
            c = chunk_of(d, step)
            for hd, lanes in ((a, slice(0, LANES)), (b, slice(LANES, 2 * LANES))):
                o = colv(g, G_EGC, hd) * r[(g, j)][CHUNK:, lanes] + intra[:, lanes]
                o_ref = refs[d][5]
                o_ref[c * CHUNK:(c + 1) * CHUNK, hd * LANES:(hd + 1) * LANES] = o.astype(o_ref.dtype)
        for g in gs:
            d = groups[g][0]
            for hd in range(GDN_HEADS):
                dec = col[g][0:1, G_DEC + d * GDN_HEADS + hd:G_DEC + d * GDN_HEADS + hd + 1]
                kv = _dot_tn(k[g][hd], (colv(g, G_EKEND, hd) * vn[(g, hd)]).astype(BF16))
                s_ref[d, hd] = dec * s[(g, hd)] + kv
        yield

    groups_of = lambda lo, hi: [g for g, (_, st_) in enumerate(groups) if lo <= st_ < hi]
    return pre, scan, groups_of


def _scan_kernel(gf_ref, gb_ref, v1f_ref, v1b_ref, decf_ref, decb_ref,
                 qf_ref, kf_ref, vf_ref, colf_ref, rowf_ref, qb_ref, kb_ref, vb_ref, colb_ref, rowb_ref,
                 o1f_ref, o1b_ref, o2f_ref, o2b_ref, s1_ref, s2_ref):
    @pl.when(pl.program_id(1) == 0)
    def _():
        s1_ref[...] = jnp.zeros_like(s1_ref)
        s2_ref[...] = jnp.zeros_like(s2_ref)

    gla_scores, gla_values, gla_outputs, gla_finish = _gla_stages(
        gf_ref, gb_ref, v1f_ref, v1b_ref, decf_ref, decb_ref, o1f_ref, o1b_ref, s1_ref)
    pre, scan, groups_of = _gdn_stages(qf_ref, kf_ref, vf_ref, colf_ref, rowf_ref,
                                       qb_ref, kb_ref, vb_ref, colb_ref, rowb_ref, o2f_ref, o2b_ref, s2_ref)

    half = NC // 2
    for _ in pre(groups_of(0, half)):
        pass
    later = pre(groups_of(half, NC))
    for step in range(half):
        for _ in scan(step):
            next(later, None)
    for _ in later:
        pass
    gla_scores(0)
    gla_step = 0
    for step in range(half, NC):
        for _ in scan(step):
            if gla_step + 1 < NC:
                gla_scores(gla_step + 1)
            gla_values(gla_step)
            gla_outputs(gla_step)
            gla_step += 1
    assert gla_step == NC
    gla_finish()


def _mix_scan(gla, v1, dec, q2, k2, v2, gcol, grow):
    nb = SEQ // TB
    n_tok = q2.shape[0]
    fwd = lambda b, i: (b * nb + i, 0)
    bwd = lambda b, i: (b * nb + nb - 1 - i, 0)
    fwd3 = lambda b, i: (b * nb + i, 0, 0)
    bwd3 = lambda b, i: (b * nb + nb - 1 - i, 0, 0)

    def gdn_specs(m2, m3):
        return [pl.BlockSpec((TB, HW), m2), pl.BlockSpec((TB, HW), m2), pl.BlockSpec((TB, HW), m2),
                pl.BlockSpec((TB, LANES), m2), pl.BlockSpec((NC, ROW_SLAB, CHUNK), m3)]

    gla_specs = [
        pl.BlockSpec((TB, 3 * GLA_QK_W), fwd),
        pl.BlockSpec((TB, 3 * GLA_QK_W), lambda b, i: (b * nb + nb - 1 - i, 1)),
        pl.BlockSpec((TB, HW), fwd),
        pl.BlockSpec((TB, HW), bwd),
        pl.BlockSpec((NC, 1, GLA_QK_W), lambda b, i: (b * nb + i, 0, 0)),
        pl.BlockSpec((NC, 1, GLA_QK_W), lambda b, i: (b * nb + nb - 1 - i, 0, 1)),
    ]
    out = jax.ShapeDtypeStruct((n_tok, HW), BF16)
    return pl.pallas_call(
        _scan_kernel,
        grid=(BATCH, nb),
        in_specs=gla_specs + gdn_specs(fwd, fwd3) + gdn_specs(bwd, bwd3),
        out_specs=(pl.BlockSpec((TB, HW), fwd), pl.BlockSpec((TB, HW), bwd),
                   pl.BlockSpec((TB, HW), fwd), pl.BlockSpec((TB, HW), bwd)),
        out_shape=(out, out, out, out),
        scratch_shapes=[pltpu.VMEM((2, GLA_HEADS // 2, LANES, LANES), F32),
                        pltpu.VMEM((2, GDN_HEADS, LANES, LANES), F32)],
        compiler_params=pltpu.CompilerParams(dimension_semantics=("parallel", "arbitrary"),
                                             vmem_limit_bytes=VMEM_LIMIT),
        name="mix_scan",
    )(gla, gla, v1, v1, dec, dec, q2, k2, v2, gcol, grow, q2, k2, v2, gcol, grow)


def _outproj_kernel(o1f_ref, o1b_ref, o2f_ref, o2b_ref, zg_ref, xaq_ref, mkv_ref, x_ref, wout_ref,
                    n1_ref, n2_ref, n3_ref, fn_ref, out_ref, *, final):
    def head_norm(o, nw_ref, width):
        parts = []
        for hd in range(HEADS):
            s = o[:, hd * LANES:(hd + 1) * LANES]
            ms = jnp.sum(s * s, axis=-1, keepdims=True) * (1.0 / width)
            parts.append(s * lax.rsqrt(ms + NORM_EPS))
        return jnp.concatenate(parts, axis=-1) * nw_ref[...]

    both = lambda f_ref, b_ref: f_ref[...].astype(F32) + b_ref[...].astype(F32)
    gate = lambda lo, width: zg_ref[:, lo:lo + width].astype(F32)
    lane_q = lax.broadcasted_iota(jnp.int32, (TM_OUT, LANES), 1)
    lane_m = lax.broadcasted_iota(jnp.int32, (MEM_LEN, LANES), 1)
    first_q = lane_q < XA_DH
    first_m = lane_m < XA_DH
    q_head = (first_q, lane_q >= XA_DH)
    m_head = (first_m, lane_m >= XA_DH)
    heads = [(p, hh) for p in range(XA_HEADS // 2) for hh in range(2)]

    sc = []
    for p, hh in heads:
        qpair = xaq_ref[:, p * LANES:(p + 1) * LANES]
        qm = jnp.where(q_head[hh], qpair, jnp.zeros_like(qpair))
        sc.append(_dot_nt(qm, mkv_ref[0, :, p * LANES:(p + 1) * LANES]))

    o1 = head_norm(both(o1f_ref, o1b_ref), n1_ref, GLA_DV) * gate(0, HW)
    y = _dot(o1.astype(BF16), wout_ref[0:HW, :])

    pv = []
    for (p, hh), s in zip(heads, sc):
        e = jnp.exp(s - jnp.max(s, axis=-1, keepdims=True))
        l = jnp.sum(e, axis=-1, keepdims=True)
        mv = mkv_ref[0, :, XA_W + p * LANES:XA_W + (p + 1) * LANES]
        mvm = jnp.where(m_head[hh], mv, jnp.zeros_like(mv))
        pv.append(_dot(e.astype(BF16), mvm) * (1.0 / l))

    o2 = head_norm(both(o2f_ref, o2b_ref), n2_ref, GDN_DV) * gate(HW, HW)
    y = y + _dot(o2.astype(BF16), wout_ref[HW:2 * HW, :])

    for p in range(XA_HEADS // 2):
        lanes = slice(p * LANES, (p + 1) * LANES)
        acc = pv[2 * p] + pv[2 * p + 1]
        sq = acc * acc
        ss0 = jnp.sum(jnp.where(first_q, sq, 0.0), axis=-1, keepdims=True)
        ss1 = jnp.sum(jnp.where(first_q, 0.0, sq), axis=-1, keepdims=True)
        ms = jnp.where(first_q, ss0, ss1) * (1.0 / XA_DH)
        o3 = acc * lax.rsqrt(ms + NORM_EPS) * n3_ref[:, lanes] * gate(2 * HW + p * LANES, LANES)
        y = y + _dot(o3.astype(BF16), wout_ref[2 * HW + p * LANES:2 * HW + (p + 1) * LANES, :])

    xo = x_ref[...] + y
    if final:
        ms = jnp.mean(xo * xo, axis=-1, keepdims=True)
        xo = xo * lax.rsqrt(ms + NORM_EPS) * fn_ref[...]
    out_ref[...] = xo


def _outproj(layer, o1f, o1b, o2f, o2b, zg, xaq, mkv, x2d, wout, n1, n2, n3, fnw):
    n_tok = x2d.shape[0]
    blocks_per_seq = SEQ // TM_OUT
    tok = lambda w: pl.BlockSpec((TM_OUT, w), lambda i: (i, 0))
    const = lambda shape: pl.BlockSpec((None,) + shape, lambda i: (layer,) + tuple(0 for _ in shape))
    return pl.pallas_call(
        functools.partial(_outproj_kernel, final=layer == DEPTH - 1),
        grid=(n_tok // TM_OUT,),
        in_specs=[
            tok(HW), tok(HW), tok(HW), tok(HW), tok(MIX_PAD_W), tok(XA_W),
            pl.BlockSpec((None, 1, MEM_LEN, 2 * XA_W), lambda i: (layer, i // blocks_per_seq, 0, 0)),
            tok(D_MODEL),
            const((MIX_PAD_W, D_MODEL)),
            const((1, HW)), const((1, HW)), const((1, XA_W)),
            pl.BlockSpec((1, D_MODEL), lambda i: (0, 0)),
        ],
        out_specs=tok(D_MODEL),
        out_shape=jax.ShapeDtypeStruct((n_tok, D_MODEL), F32),
        compiler_params=pltpu.CompilerParams(dimension_semantics=("parallel",),
                                             vmem_limit_bytes=VMEM_LIMIT),
        name="outproj",
    )(o1f, o1b, o2f, o2b, zg, xaq, mkv, x2d, wout, n1, n2, n3, fnw)


def _pad_heads(w, axis):
    shp = w.shape
    w = w.reshape(shp[:axis] + (HEADS, GLA_DV) + shp[axis + 1:])
    pad = [(0, 0)] * w.ndim
    pad[axis + 1] = (0, LANES - GLA_DV)
    w = jnp.pad(w, pad)
    return w.reshape(shp[:axis] + (HW,) + shp[axis + 1:])


_SRC_SIZES = (GLA_QK_W, GLA_QK_W, HEADS * GLA_DV, HEADS * GLA_DV, 2 * GLA_RANK, 3 * HEADS * GDN_DK,
              HEADS * GDN_DV, 2 * GDN_HEADS, 2 * GDN_HEADS, XA_W, XA_W)
(S_GQ, S_GK, S_GV, S_GZ, S_LR, S_DQKV, S_DZ, S_DB, S_DA, S_XQ, S_XZ) = (
    sum(_SRC_SIZES[:n]) for n in range(len(_SRC_SIZES)))
IN_W = sum(_SRC_SIZES)
REPACK_ROWS = 256


def _repack_kernel(wt_ref, out_ref, w_ref):
    for lo in range(0, IN_W, LANES):
        width = min(LANES, IN_W - lo)
        w_ref[:, lo:lo + width] = wt_ref[lo:lo + width, :].T

    def src(lo, width):
        return w_ref[:, lo:lo + width].astype(BF16)

    rows = w_ref.shape[0]
    zeros = lambda width: jnp.zeros((rows, width), BF16)

    def put(dst, piece):
        out_ref[:, dst:dst + piece.shape[1]] = piece

    def put_heads(dst, lo):
        for hd in range(HEADS):
            put(dst + hd * LANES, jnp.concatenate([src(lo + hd * GLA_DV, GLA_DV), zeros(LANES - GLA_DV)], axis=1))

    put(C_GQ, src(S_GQ, GLA_QK_W))
    put(C_GK, src(S_GK, GLA_QK_W))
    put_heads(C_GV, S_GV)
    for part in range(3):
        put_heads(C_DQKV + part * HW, S_DQKV + part * HEADS * GDN_DK)
    put_heads(C_Z + HW, S_DZ)
    put(C_Z + 2 * HW, src(S_XZ, XA_W))
    put(C_XQ, src(S_XQ, XA_W))
    da = src(S_DA, 2 * GDN_HEADS)
    misc = jnp.concatenate(
        [src(S_DB, 2 * GDN_HEADS)] + [da] * M_A_COPIES
        + [zeros(M_LR - M_A - 8 * M_A_COPIES), src(S_LR, 2 * GLA_RANK), zeros(LANES - M_LR - 2 * GLA_RANK)],
        axis=1)
    for hd in range(HEADS):
        put(C_Z + hd * LANES, jnp.concatenate(
            [src(S_GZ + hd * GLA_DV, GLA_DV), misc[:, hd * MISC_PER_TILE:(hd + 1) * MISC_PER_TILE]], axis=1))


def _repack_w_in(w_in):
    return pl.pallas_call(
        _repack_kernel,
        grid=(DEPTH, D_MODEL // REPACK_ROWS),
        in_specs=[pl.BlockSpec((None, IN_W, REPACK_ROWS), lambda l, r: (l, 0, r))],
        out_specs=pl.BlockSpec((None, REPACK_ROWS, IN_PAD_W), lambda l, r: (l, r, 0)),
        out_shape=jax.ShapeDtypeStruct((DEPTH, D_MODEL, IN_PAD_W), BF16),
        scratch_shapes=[pltpu.VMEM((REPACK_ROWS, IN_W), F32)],
        compiler_params=pltpu.CompilerParams(dimension_semantics=("parallel", "parallel")),
        name="repack_w_in",
    )(jnp.swapaxes(w_in, 1, 2))


def _pack_params(norm_w, w_in, gla_w2, gla_b, gla_norm_w, gdn_conv_w, gdn_a_log, gdn_dt_bias, gdn_norm_w,
                 xa_norm_w, w_out):
    hd = HEADS * GDN_DK
    zeros = lambda *shape: jnp.zeros((DEPTH,) + shape, F32)
    w_all = _repack_w_in(w_in)

    cw = jnp.transpose(gdn_conv_w, (0, 2, 1))
    cw = jnp.concatenate([_pad_heads(cw[:, :, 0:hd], 2), _pad_heads(cw[:, :, hd:2 * hd], 2),
                          _pad_heads(cw[:, :, 2 * hd:], 2)], axis=2)
    cw = jnp.pad(cw, ((0, 0), (0, 8 - GDN_CONV), (0, 0)))

    w2bd = jnp.concatenate(
        [zeros(M_LR, 2 * GLA_QK_W),
         jnp.concatenate([gla_w2[:, 0], zeros(GLA_RANK, GLA_QK_W)], axis=2),
         jnp.concatenate([zeros(GLA_RANK, GLA_QK_W), gla_w2[:, 1]], axis=2),
         zeros(LANES - M_LR - 2 * GLA_RANK, 2 * GLA_QK_W)], axis=1).astype(BF16)
    glab = gla_b.reshape(DEPTH, 1, 2 * GLA_QK_W)

    def a_slab(p):
        flat = p.reshape(DEPTH, 1, 2 * GDN_HEADS)
        return jnp.concatenate([zeros(1, M_A)] + [flat] * M_A_COPIES + [zeros(1, LANES - M_A - 8 * M_A_COPIES)],
                               axis=2)

    gparams = jnp.concatenate([a_slab(gdn_a_log), a_slab(gdn_dt_bias), zeros(6, LANES)], axis=1)

    wout = jnp.concatenate([_pad_heads(w_out[:, 0:HEADS * GLA_DV], 1),
                            _pad_heads(w_out[:, HEADS * GLA_DV:HEADS * (GLA_DV + GDN_DV)], 1),
                            w_out[:, HEADS * (GLA_DV + GDN_DV):]], axis=1).astype(BF16)
    pad_norm = lambda w: jnp.tile(jnp.pad(w, ((0, 0), (0, LANES - w.shape[1]))), (1, HEADS)).reshape(DEPTH, 1, HW)
    n1 = pad_norm(gla_norm_w)
    n2 = pad_norm(gdn_norm_w)
    n3 = jnp.tile(xa_norm_w, (1, XA_HEADS)).reshape(DEPTH, 1, XA_W)
    return norm_w.reshape(DEPTH, 1, D_MODEL), w_all, cw, w2bd, glab, gparams, wout, n1, n2, n3


def kernel(x, mem, norm_w, w_in, gla_w2, gla_b, gla_norm_w, gdn_conv_w, gdn_a_log, gdn_dt_bias,
           gdn_norm_w, mem_norm_w, xa_w_kv, xa_norm_w, w_out, final_norm_w):
    assert x.shape == (BATCH, SEQ, D_MODEL) and mem.shape == (BATCH, MEM_LEN, D_MODEL)
    mkv = _memkv(mem, mem_norm_w, xa_w_kv.astype(BF16))
    nw, w_all, cw, w2bd, glab, gparams, wout, n1, n2, n3 = _pack_params(
        norm_w, w_in, gla_w2, gla_b, gla_norm_w, gdn_conv_w, gdn_a_log, gdn_dt_bias, gdn_norm_w,
        xa_norm_w, w_out)
    h = x.reshape(BATCH * SEQ, D_MODEL)
    fnw = final_norm_w.reshape(1, D_MODEL)
    for l in range(DEPTH):
        gla, v1, dec, zg, q2, k2, v2, gcol, grow, xaq = _inproj(l, h, nw, w_all, cw, w2bd, glab, gparams)
        o1f, o1b, o2f, o2b = _mix_scan(gla, v1, dec, q2, k2, v2, gcol, grow)
        h = _outproj(l, o1f, o1b, o2f, o2b, zg, xaq, mkv, h, wout, n1, n2, n3, fnw)
    return h.reshape(BATCH, SEQ, D_MODEL)
```

```python
import functools

import jax
import jax.numpy as jnp
from jax import lax
from jax.experimental import pallas as pl
from jax.experimental.pallas import tpu as pltpu

F32 = jnp.float32
BF16 = jnp.bfloat16

D_MODEL = 1024
BATCH = 8
SEQ = 4096
DEPTH = 2
MEM_LEN = 256
CHUNK = 64
NORM_EPS = 1e-6
GLA_HEADS = 4
GLA_DK = 64
GLA_DV = 96
GLA_RANK = 16
GLA_GATE_NORMALIZER = 16.0
GDN_HEADS = 4
GDN_DK = 96
GDN_DV = 96
GDN_CONV = 5
XA_HEADS = 4
XA_DH = 64

LANES = 128
HEADS = 4
HW = HEADS * LANES
GLA_QK_W = GLA_HEADS * GLA_DK
XA_W = XA_HEADS * XA_DH
MIX_PAD_W = 2 * HW + XA_W

C_GQ = 0
C_GK = C_GQ + GLA_QK_W
C_GV = C_GK + GLA_QK_W
C_DQKV = C_GV + HW
C_Z = C_DQKV + 3 * HW
C_XQ = C_Z + MIX_PAD_W
IN_PAD_W = C_XQ + XA_W
MISC_PER_TILE = LANES - GLA_DV

M_A = 8
M_A_COPIES = 4
M_LR = 64
G_BETA, G_GC, G_EGC, G_EKEND, G_DEC = 0, 8, 16, 24, 32
ROW_SLAB = 16

TM_IN = 512
TM_OUT = 512
NC = 8
TB = NC * CHUNK
HALO = 8
CONV_PHASES = 4
VMEM_LIMIT = 56 * 1024 * 1024


def _dot(a, b):
    return jnp.dot(a, b, preferred_element_type=F32)


def _dot_nt(a, b):
    return lax.dot_general(a, b, (((1,), (1,)), ((), ())), preferred_element_type=F32)


def _dot_tn(a, b):
    return lax.dot_general(a, b, (((0,), (0,)), ((), ())), preferred_element_type=F32)


def _sigmoid(x):
    return 1.0 / (1.0 + jnp.exp(-x))


def _silu(x):
    return x * _sigmoid(x)


def _softplus(x):
    return jnp.maximum(x, 0.0) + jnp.log(1.0 + jnp.exp(-jnp.abs(x)))


def _log_sigmoid(x):
    return jnp.minimum(x, 0.0) - jnp.log(1.0 + jnp.exp(-jnp.abs(x)))


def _split2(x):
    hi = pltpu.bitcast(pltpu.bitcast(x, jnp.int32) & jnp.int32(-65536), F32)
    return hi.astype(BF16), (x - hi).astype(BF16)


def _tri_sum(tri, x):
    hi, lo = _split2(x)
    return _dot(tri, hi) + _dot(tri, lo)


def _tri_masks():
    ri = lax.broadcasted_iota(jnp.int32, (CHUNK, CHUNK), 0)
    ci = lax.broadcasted_iota(jnp.int32, (CHUNK, CHUNK), 1)
    return ri, ci


def _memkv_kernel(mem_ref, nw_ref, w_ref, out_ref):
    m = mem_ref[0]
    ms = jnp.mean(m * m, axis=-1, keepdims=True)
    mn = (m * lax.rsqrt(ms + NORM_EPS) * nw_ref[0]).astype(BF16)
    out_ref[0, 0] = _dot(mn, w_ref[0]).astype(BF16)


def _memkv(mem, mem_norm_w, xa_w_kv_bf16):
    return pl.pallas_call(
        _memkv_kernel,
        grid=(DEPTH, BATCH),
        in_specs=[
            pl.BlockSpec((1, MEM_LEN, D_MODEL), lambda l, b: (b, 0, 0)),
            pl.BlockSpec((1, 1, D_MODEL), lambda l, b: (l, 0, 0)),
            pl.BlockSpec((1, D_MODEL, 2 * XA_W), lambda l, b: (l, 0, 0)),
        ],
        out_specs=pl.BlockSpec((1, 1, MEM_LEN, 2 * XA_W), lambda l, b: (l, b, 0, 0)),
        out_shape=jax.ShapeDtypeStruct((DEPTH, BATCH, MEM_LEN, 2 * XA_W), BF16),
        compiler_params=pltpu.CompilerParams(dimension_semantics=("parallel", "parallel")),
        name="memkv",
    )(mem, mem_norm_w.reshape(DEPTH, 1, D_MODEL), xa_w_kv_bf16)


def _inproj_kernel(x_ref, xp_ref, xn_ref, nw_ref, w_ref, cw_ref, w2_ref, gb_ref, gp_ref,
                   gla_ref, v1_ref, dec_ref, zg_ref, q2_ref, k2_ref, v2_ref, gcol_ref, grow_ref,
                   xaq_ref, ext_ref, conv_ref):
    nw = nw_ref[...]

    def norm(x):
        ms = jnp.mean(x * x, axis=-1, keepdims=True)
        return (x * lax.rsqrt(ms + NORM_EPS) * nw).astype(BF16)

    h = norm(x_ref[...])
    h_halo = norm(jnp.concatenate([xp_ref[...], xn_ref[...]], axis=0))

    def proj(hh, lo, width):
        return _dot(hh, w_ref[:, lo:lo + width])

    blocks_per_seq = SEQ // TM_IN
    j = lax.rem(pl.program_id(0), blocks_per_seq)
    cw = cw_ref[...]
    base = HALO - GDN_CONV // 2

    def gdn_mm(grp):
        lo = C_DQKV + grp * HW
        halo = proj(h_halo, lo, HW)
        return proj(h, lo, HW), halo[0:HALO], halo[HALO:]

    def gdn_vpu(grp, res):
        main, prev, nxt = res
        for hd in range(HEADS):
            slab = grp * HEADS + hd
            lanes = slice(hd * LANES, (hd + 1) * LANES)
            ext_ref[slab, 0:HALO, :] = jnp.where(j == 0, 0.0, prev[:, lanes])
            ext_ref[slab, HALO:HALO + TM_IN, :] = main[:, lanes]
            ext_ref[slab, HALO + TM_IN:2 * HALO + TM_IN, :] = jnp.where(j == blocks_per_seq - 1, 0.0,
                                                                     nxt[:, lanes])
            taps = cw[:, grp * HW + hd * LANES:grp * HW + (hd + 1) * LANES]
            for p in range(CONV_PHASES):
                acc = None
                for t in range(GDN_CONV):
                    win = ext_ref[slab, pl.ds(base + p + t, TM_IN // CONV_PHASES, stride=CONV_PHASES), :]
                    acc = win * taps[t:t + 1, :] if acc is None else acc + win * taps[t:t + 1, :]
                conv_ref[slab, pl.ds(p, TM_IN // CONV_PHASES, stride=CONV_PHASES), :] = acc
            y = _silu(conv_ref[slab])
            if grp == 2:
                v2_ref[:, lanes] = y.astype(BF16)
            else:
                ref, scale = ((q2_ref, GDN_DK ** -0.5), (k2_ref, 1.0))[grp]
                ss = jnp.sum(y * y, axis=-1, keepdims=True)
                ref[:, lanes] = (y * lax.rsqrt(ss + NORM_EPS) * scale).astype(BF16)

    def z_vpu(lo, width, res):
        zg_ref[:, lo:lo + width] = _silu(res).astype(BF16)

    def gla_z_and_misc_vpu(res):
        z_vpu(0, HW, res)
        misc_vpu(jnp.concatenate([res[:, hd * LANES + GLA_DV:(hd + 1) * LANES] for hd in range(HEADS)], axis=1))

    def xaq_vpu(res):
        xaq_ref[...] = (res * (XA_DH ** -0.5)).astype(BF16)

    lane = lax.broadcasted_iota(jnp.int32, (1, LANES), 1)
    fwd_lane = lax.rem(lane, 8) < GDN_HEADS
    ri, ci = _tri_masks()
    lower = jnp.where(ri >= ci, 1.0, 0.0).astype(BF16)
    upper = jnp.where(ri <= ci, 1.0, 0.0).astype(BF16)
    lower_upper = jnp.concatenate([lower, upper], axis=0)
    gate = {}

    def misc_vpu(m):
        logits = _dot(m.astype(BF16), w2_ref[...]) + gb_ref[...]
        gate["g"] = _log_sigmoid(logits) * (1.0 / GLA_GATE_NORMALIZER)
        is_a = (lane >= M_A) & (lane < M_A + 8 * M_A_COPIES)
        neg_a = jnp.where(is_a, -jnp.exp(gp_ref[0:1, :]), 0.0)
        gg = neg_a * _softplus(m + gp_ref[1:2, :])
        beta = _sigmoid(m)
        for c in range(TM_IN // CHUNK):
            rows = slice(c * CHUNK, (c + 1) * CHUNK)
            ggc = gg[rows]
            hi, lo = _split2(ggc)
            both = _dot(lower_upper, jnp.concatenate([hi, lo], axis=1))
            pf = both[0:CHUNK, 0:LANES] + both[0:CHUNK, LANES:]
            sf = both[CHUNK:, 0:LANES] + both[CHUNK:, LANES:]
            gc = jnp.where(fwd_lane, pf, sf)
            last = jnp.where(fwd_lane, pf[CHUNK - 1:CHUNK], sf[0:1])
            col = jnp.where(lane < G_GC, beta[rows],
                  jnp.where(lane < G_EGC, gc,
                  jnp.where(lane < G_EKEND, jnp.exp(gc),
                  jnp.where(lane < G_DEC, jnp.exp(last - gc), jnp.exp(last)))))
            gcol_ref[rows, :] = col
            grow_ref[c] = col.T[0:ROW_SLAB, :]

    def gla_v_vpu(res):
        v1_ref[...] = res.astype(BF16)

    def gla_qk_vpu(pg):
        q1 = pg[:, 0:GLA_QK_W] * (GLA_DK ** -0.5)
        k1 = pg[:, GLA_QK_W:2 * GLA_QK_W]
        g = gate["g"]
        for c in range(TM_IN // CHUNK):
            rows = slice(c * CHUNK, (c + 1) * CHUNK)
            gch = g[rows]
            q1c = q1[rows]
            k1c = k1[rows]
            bf = _tri_sum(lower, gch[:, 0:GLA_QK_W])
            br = _tri_sum(upper, gch[:, GLA_QK_W:])
            for d, (b, last) in enumerate(((bf, bf[CHUNK - 1:CHUNK]), (br, br[0:1]))):
                off = d * 3 * GLA_QK_W
                gla_ref[rows, off:off + GLA_QK_W] = (q1c * jnp.exp(b)).astype(BF16)
                gla_ref[rows, off + GLA_QK_W:off + 2 * GLA_QK_W] = (k1c * jnp.exp(-b)).astype(BF16)
                gla_ref[rows, off + 2 * GLA_QK_W:off + 3 * GLA_QK_W] = (k1c * jnp.exp(last - b)).astype(BF16)
                dec_ref[c, :, d * GLA_QK_W:(d + 1) * GLA_QK_W] = jnp.exp(last)

    part = functools.partial
    tasks = [
        (part(proj, h, C_Z, HW), gla_z_and_misc_vpu),
        (part(gdn_mm, 0), part(gdn_vpu, 0)),
        (part(gdn_mm, 1), part(gdn_vpu, 1)),
        (part(proj, h, C_Z + HW, HW), part(z_vpu, HW, HW)),
        (part(gdn_mm, 2), part(gdn_vpu, 2)),
        (part(proj, h, C_Z + 2 * HW, XA_W), part(z_vpu, 2 * HW, XA_W)),
        (part(proj, h, C_GQ, 2 * GLA_QK_W), gla_qk_vpu),
        (part(proj, h, C_GV, HW), gla_v_vpu),
        (part(proj, h, C_XQ, XA_W), xaq_vpu),
    ]
    res = tasks[0][0]()
    for t, (_, vpu) in enumerate(tasks):
        nxt_res = tasks[t + 1][0]() if t + 1 < len(tasks) else None
        vpu(res)
        res = nxt_res


def _inproj(layer, x2d, norm_w, w_all, conv_w, w2bd, gla_b, gdn_params):
    n_tok = x2d.shape[0]
    nblk = n_tok // TM_IN
    halo_blocks = TM_IN // HALO
    n_halo = n_tok // HALO
    nchunk = TM_IN // CHUNK
    tok = lambda w: pl.BlockSpec((TM_IN, w), lambda i: (i, 0))
    const = lambda shape: pl.BlockSpec((None,) + shape, lambda i: (layer,) + tuple(0 for _ in shape))
    out_shapes = (
        jax.ShapeDtypeStruct((n_tok, 6 * GLA_QK_W), BF16),
        jax.ShapeDtypeStruct((n_tok, HW), BF16),
        jax.ShapeDtypeStruct((n_tok // CHUNK, 1, 2 * GLA_QK_W), F32),
        jax.ShapeDtypeStruct((n_tok, MIX_PAD_W), BF16),
        jax.ShapeDtypeStruct((n_tok, HW), BF16),
        jax.ShapeDtypeStruct((n_tok, HW), BF16),
        jax.ShapeDtypeStruct((n_tok, HW), BF16),
        jax.ShapeDtypeStruct((n_tok, LANES), F32),
        jax.ShapeDtypeStruct((n_tok // CHUNK, ROW_SLAB, CHUNK), F32),
        jax.ShapeDtypeStruct((n_tok, XA_W), BF16),
    )
    out_specs = (
        tok(6 * GLA_QK_W), tok(HW),
        pl.BlockSpec((nchunk, 1, 2 * GLA_QK_W), lambda i: (i, 0, 0)),
        tok(MIX_PAD_W), tok(HW), tok(HW), tok(HW), tok(LANES),
        pl.BlockSpec((nchunk, ROW_SLAB, CHUNK), lambda i: (i, 0, 0)),
        tok(XA_W),
    )
    return pl.pallas_call(
        _inproj_kernel,
        grid=(nblk,),
        in_specs=[
            tok(D_MODEL),
            pl.BlockSpec((HALO, D_MODEL), lambda i: (jnp.maximum(i * halo_blocks - 1, 0), 0)),
            pl.BlockSpec((HALO, D_MODEL), lambda i: (jnp.minimum((i + 1) * halo_blocks, n_halo - 1), 0)),
            const((1, D_MODEL)),
            const((D_MODEL, IN_PAD_W)),
            const((8, 3 * HW)),
            const((LANES, 2 * GLA_QK_W)),
            const((1, 2 * GLA_QK_W)),
            const((8, LANES)),
        ],
        out_specs=out_specs,
        out_shape=out_shapes,
        scratch_shapes=[pltpu.VMEM((3 * HEADS, TM_IN + 2 * HALO, LANES), F32),
                        pltpu.VMEM((3 * HEADS, TM_IN, LANES), F32)],
        compiler_params=pltpu.CompilerParams(dimension_semantics=("parallel",),
                                             vmem_limit_bytes=VMEM_LIMIT),
        name="inproj",
    )(x2d, x2d, x2d, norm_w, w_all, conv_w, w2bd, gla_b, gdn_params)


def _gla_stages(gf_ref, gb_ref, vf_ref, vb_ref, decf_ref, decb_ref, of_ref, ob_ref, s_ref):
    ri = lax.broadcasted_iota(jnp.int32, (CHUNK, LANES), 0)
    lane_tok = lax.broadcasted_iota(jnp.int32, (CHUNK, LANES), 1)
    ci = lane_tok & (CHUNK - 1)
    incl = (ri >= ci, ri <= ci)
    left = lane_tok < GLA_DK
    lane_sq = lax.broadcasted_iota(jnp.int32, (LANES, LANES), 1) < GLA_DK
    zero_bf = jnp.zeros((CHUNK, LANES), BF16)
    cat = jnp.concatenate

    def split_heads(x):
        return cat([jnp.where(left, x, zero_bf), jnp.where(left, zero_bf, x)], axis=0)

    g_refs, v_refs = (gf_ref, gb_ref), (vf_ref, vb_ref)
    dec_refs, o_refs = (decf_ref, decb_ref), (of_ref, ob_ref)
    chunk_of = lambda d, step: step if d == 0 else NC - 1 - step
    units_of = lambda step: [(d, step, p) for d in range(2) for p in range(GLA_HEADS // 2)]
    qe, kend, v, a, intra, kv = {}, {}, {}, {}, {}, {}
    st = {(d, p): s_ref[d, p] for d in range(2) for p in range(GLA_HEADS // 2)}

    def scores(step):
        for u in units_of(step):
            d, _, p = u
            c = chunk_of(d, step)
            rows = slice(c * CHUNK, (c + 1) * CHUNK)
            qe[u] = g_refs[d][rows, p * LANES:(p + 1) * LANES]
            ke = g_refs[d][rows, GLA_QK_W + p * LANES:GLA_QK_W + (p + 1) * LANES]
            kend[u] = g_refs[d][rows, 2 * GLA_QK_W + p * LANES:2 * GLA_QK_W + (p + 1) * LANES]
            v[u] = (v_refs[d][rows, 2 * p * LANES:(2 * p + 1) * LANES],
                    v_refs[d][rows, (2 * p + 1) * LANES:(2 * p + 2) * LANES])
            a[u] = jnp.where(incl[d], _dot_nt(qe[u], split_heads(ke)), 0.0).astype(BF16)

    def values(step):
        for u in units_of(step):
            intra[u] = _dot(a[u], cat([cat([v[u][0], zero_bf], axis=1), cat([zero_bf, v[u][1]], axis=1)],
                                      axis=0))
            kv[u] = _dot_tn(cat([v[u][0], v[u][1]], axis=1), kend[u])

    def outputs(step):
        for u in units_of(step):
            d, _, p = u
            c = chunk_of(d, step)
            inter = _dot_nt(split_heads(qe[u]), st[(d, p)].astype(BF16))
            for hh in range(2):
                hd = 2 * p + hh
                o = intra[u][:, hh * LANES:(hh + 1) * LANES] + inter[hh * CHUNK:(hh + 1) * CHUNK]
                o_refs[d][c * CHUNK:(c + 1) * CHUNK, hd * LANES:(hd + 1) * LANES] = o.astype(o_refs[d].dtype)
            dec = dec_refs[d][c, :, p * LANES:(p + 1) * LANES]
            st[(d, p)] = dec * st[(d, p)] + jnp.where(lane_sq, kv[u][0:LANES], kv[u][LANES:])

    def finish():
        for (d, p), s in st.items():
            s_ref[d, p] = s

    return scores, values, outputs, finish


def _gdn_stages(qf_ref, kf_ref, vf_ref, colf_ref, rowf_ref, qb_ref, kb_ref, vb_ref, colb_ref, rowb_ref,
                of_ref, ob_ref, s_ref):
    pk = GDN_HEADS * CHUNK
    ri = lax.broadcasted_iota(jnp.int32, (CHUNK, pk), 0)
    ci = lax.broadcasted_iota(jnp.int32, (CHUNK, pk), 1) & (CHUNK - 1)
    incl = (ri >= ci, ri <= ci)
    strict = (ri > ci, ri < ci)
    eye = jnp.where(ri == ci, 1.0, 0.0)
    blk = lambda axis: lax.shift_right_logical(lax.broadcasted_iota(jnp.int32, (pk, pk), axis), 6)
    same_blk = blk(0) == blk(1)
    bd_mask = jnp.where(same_blk, 1.0, 0.0).astype(BF16)
    left = lax.broadcasted_iota(jnp.int32, (CHUNK, LANES), 1) < CHUNK
    zero_tok = jnp.zeros((CHUNK, LANES), BF16)
    zero_sq = jnp.zeros((LANES, LANES), BF16)

    refs = ((qf_ref, kf_ref, vf_ref, colf_ref, rowf_ref, of_ref),
            (qb_ref, kb_ref, vb_ref, colb_ref, rowb_ref, ob_ref))
    groups = [(d, step) for step in range(NC) for d in range(2)]
    pairs = range(GDN_HEADS // 2)
    top = lambda m: m[0:CHUNK]
    bot = lambda m: m[CHUNK:2 * CHUNK]
    cat = jnp.concatenate
    chunk_of = lambda d, step: step if d == 0 else NC - 1 - step

    def block_diag(x):
        return cat([x, x, x, x], axis=0) * bd_mask

    def diag2(a, b, zero):
        return cat([cat([a, zero], axis=1), cat([zero, b], axis=1)], axis=0)

    q, k, v, col, gc_r = [], [], [], [], []
    for d, step in groups:
        q_ref, k_ref, v_ref, col_ref, row_ref, _ = refs[d]
        c = chunk_of(d, step)
        rows = slice(c * CHUNK, (c + 1) * CHUNK)
        heads = [slice(hd * LANES, (hd + 1) * LANES) for hd in range(GDN_HEADS)]
        q.append([q_ref[rows, h] for h in heads])
        k.append([k_ref[rows, h] for h in heads])
        v.append([v_ref[rows, h] for h in heads])
        col.append(col_ref[rows, :])
        rowt = row_ref[c]
        gc_r.append(cat([rowt[8 + d * GDN_HEADS + hd:9 + d * GDN_HEADS + hd, :] for hd in range(GDN_HEADS)],
                        axis=1))

    def colv(g, base, hd):
        lane = base + groups[g][0] * GDN_HEADS + hd
        return col[g][:, lane:lane + 1]

    def col_packed(g, base):
        tiles = [jnp.where(left, jnp.broadcast_to(colv(g, base, 2 * j), (CHUNK, LANES)),
                           jnp.broadcast_to(colv(g, base, 2 * j + 1), (CHUNK, LANES))) for j in pairs]
        return cat(tiles, axis=1)

    qk, uw = {}, {}

    def rhs_of(g, hd):
        beta = colv(g, G_BETA, hd)
        return cat([(v[g][hd].astype(F32) * beta).astype(BF16),
                    (k[g][hd].astype(F32) * (beta * colv(g, G_EGC, hd))).astype(BF16)], axis=1)

    def pre(gl):
        kq = {g: [_dot_nt(cat([cat([k[g][2 * j], k[g][2 * j + 1]], axis=1),
                               cat([q[g][2 * j], q[g][2 * j + 1]], axis=1)], axis=0),
                          diag2(k[g][2 * j], k[g][2 * j + 1], zero_tok)) for j in pairs] for g in gl}
        yield
        pm, cur = {}, {}
        for g in gl:
            d = groups[g][0]
            kk = cat([top(kq[g][j]) for j in pairs], axis=1)
            qkr = cat([bot(kq[g][j]) for j in pairs], axis=1)
            decay = jnp.where(incl[d], jnp.exp(jnp.where(incl[d], col_packed(g, G_GC) - gc_r[g], 0.0)), 0.0)
            n = jnp.where(strict[d], -(kk * decay * col_packed(g, G_BETA)), 0.0)
            qk[g] = jnp.where(incl[d], qkr * decay, 0.0).astype(BF16)
            pm[g] = eye + n
            cur[g] = _dot(n.astype(BF16), block_diag(n.astype(BF16)))
        yield
        for _ in range(4):
            st = {g: _dot(cat([pm[g].astype(BF16), cur[g].astype(BF16)], axis=0),
                          block_diag(cur[g].astype(BF16))) for g in gl}
            for g in gl:
                pm[g] = pm[g] + top(st[g])
                cur[g] = bot(st[g])
            yield
        tinv = {g: pm[g] + _dot(pm[g].astype(BF16), block_diag(cur[g].astype(BF16))) for g in gl}
        yield
        for g in gl:
            uw[g] = []
            for j in pairs:
                t = tinv[g][:, j * LANES:(j + 1) * LANES]
                lhs = cat([jnp.where(left, t, 0.0), jnp.where(left, 0.0, t)], axis=0).astype(BF16)
                uw[g].append(_dot(lhs, cat([rhs_of(g, 2 * j), rhs_of(g, 2 * j + 1)], axis=0)))
        yield

    def scan(step):
        gs = [g for g, (_, st_) in enumerate(groups) if st_ == step]
        units = [(g, j) for g in gs for j in pairs]
        s = {(g, hd): s_ref[groups[g][0], hd] for g in gs for hd in range(GDN_HEADS)}
        r = {}
        for g, j in units:
            a, b = 2 * j, 2 * j + 1
            x = uw[g][j]
            lhs = cat([cat([x[0:CHUNK, LANES:].astype(BF16), x[CHUNK:, LANES:].astype(BF16)], axis=1),
                       cat([q[g][a], q[g][b]], axis=1)], axis=0)
            r[(g, j)] = _dot(lhs, diag2(s[(g, a)].astype(BF16), s[(g, b)].astype(BF16), zero_sq))
        yield
        vn = {}
        for g, j in units:
            x, rr = uw[g][j], r[(g, j)]
            vn[(g, 2 * j)] = x[0:CHUNK, 0:LANES] - rr[0:CHUNK, 0:LANES]
            vn[(g, 2 * j + 1)] = x[CHUNK:, 0:LANES] - rr[0:CHUNK, LANES:]
        for g, j in units:
            d = groups[g][0]
            a, b = 2 * j, 2 * j + 1
            intra = _dot(qk[g][:, j * LANES:(j + 1) * LANES],
                         diag2(vn[(g, a)].astype(BF16), vn[(g, b)].astype(BF16), zero_tok))
            c = chunk_of(d, step)
            for hd, lanes in ((a, slice(0, LANES)), (b, slice(LANES, 2 * LANES))):
                o = colv(g, G_EGC, hd) * r[(g, j)][CHUNK:, lanes] + intra[:, lanes]
                o_ref = refs[d][5]
                o_ref[c * CHUNK:(c + 1) * CHUNK, hd * LANES:(hd + 1) * LANES] = o.astype(o_ref.dtype)
        for g in gs:
            d = groups[g][0]
            for hd in range(GDN_HEADS):
                dec = col[g][0:1, G_DEC + d * GDN_HEADS + hd:G_DEC + d * GDN_HEADS + hd + 1]
                kv = _dot_tn(k[g][hd], (colv(g, G_EKEND, hd) * vn[(g, hd)]).astype(BF16))
                s_ref[d, hd] = dec * s[(g, hd)] + kv
        yield

    groups_of = lambda lo, hi: [g for g, (_, st_) in enumerate(groups) if lo <= st_ < hi]
    return pre, scan, groups_of


def _scan_kernel(gf_ref, gb_ref, v1f_ref, v1b_ref, decf_ref, decb_ref,
                 qf_ref, kf_ref, vf_ref, colf_ref, rowf_ref, qb_ref, kb_ref, vb_ref, colb_ref, rowb_ref,
                 o1f_ref, o1b_ref, o2f_ref, o2b_ref, s1_ref, s2_ref):
    @pl.when(pl.program_id(1) == 0)
    def _():
        s1_ref[...] = jnp.zeros_like(s1_ref)
        s2_ref[...] = jnp.zeros_like(s2_ref)

    gla_scores, gla_values, gla_outputs, gla_finish = _gla_stages(
        gf_ref, gb_ref, v1f_ref, v1b_ref, decf_ref, decb_ref, o1f_ref, o1b_ref, s1_ref)
    pre, scan, groups_of = _gdn_stages(qf_ref, kf_ref, vf_ref, colf_ref, rowf_ref,
                                       qb_ref, kb_ref, vb_ref, colb_ref, rowb_ref, o2f_ref, o2b_ref, s2_ref)

    half = NC // 2
    for _ in pre(groups_of(0, half)):
        pass
    later = pre(groups_of(half, NC))
    for step in range(half):
        for _ in scan(step):
            next(later, None)
    for _ in later:
        pass
    gla_scores(0)
    gla_step = 0
    for step in range(half, NC):
        for _ in scan(step):
            if gla_step + 1 < NC:
                gla_scores(gla_step + 1)
            gla_values(gla_step)
            gla_outputs(gla_step)
            gla_step += 1
    assert gla_step == NC
    gla_finish()


def _mix_scan(gla, v1, dec, q2, k2, v2, gcol, grow):
    nb = SEQ // TB
    n_tok = q2.shape[0]
    fwd = lambda b, i: (b * nb + i, 0)
    bwd = lambda b, i: (b * nb + nb - 1 - i, 0)
    fwd3 = lambda b, i: (b * nb + i, 0, 0)
    bwd3 = lambda b, i: (b * nb + nb - 1 - i, 0, 0)

    def gdn_specs(m2, m3):
        return [pl.BlockSpec((TB, HW), m2), pl.BlockSpec((TB, HW), m2), pl.BlockSpec((TB, HW), m2),
                pl.BlockSpec((TB, LANES), m2), pl.BlockSpec((NC, ROW_SLAB, CHUNK), m3)]

    gla_specs = [
        pl.BlockSpec((TB, 3 * GLA_QK_W), fwd),
        pl.BlockSpec((TB, 3 * GLA_QK_W), lambda b, i: (b * nb + nb - 1 - i, 1)),
        pl.BlockSpec((TB, HW), fwd),
        pl.BlockSpec((TB, HW), bwd),
        pl.BlockSpec((NC, 1, GLA_QK_W), lambda b, i: (b * nb + i, 0, 0)),
        pl.BlockSpec((NC, 1, GLA_QK_W), lambda b, i: (b * nb + nb - 1 - i, 0, 1)),
    ]
    out = jax.ShapeDtypeStruct((n_tok, HW), BF16)
    return pl.pallas_call(
        _scan_kernel,
        grid=(BATCH, nb),
        in_specs=gla_specs + gdn_specs(fwd, fwd3) + gdn_specs(bwd, bwd3),
        out_specs=(pl.BlockSpec((TB, HW), fwd), pl.BlockSpec((TB, HW), bwd),
                   pl.BlockSpec((TB, HW), fwd), pl.BlockSpec((TB, HW), bwd)),
        out_shape=(out, out, out, out),
        scratch_shapes=[pltpu.VMEM((2, GLA_HEADS // 2, LANES, LANES), F32),
                        pltpu.VMEM((2, GDN_HEADS, LANES, LANES), F32)],
        compiler_params=pltpu.CompilerParams(dimension_semantics=("parallel", "arbitrary"),
                                             vmem_limit_bytes=VMEM_LIMIT),
        name="mix_scan",
    )(gla, gla, v1, v1, dec, dec, q2, k2, v2, gcol, grow, q2, k2, v2, gcol, grow)


def _outproj_kernel(o1f_ref, o1b_ref, o2f_ref, o2b_ref, zg_ref, xaq_ref, mkv_ref, x_ref, wout_ref,
                    n1_ref, n2_ref, n3_ref, fn_ref, out_ref, *, final):
    def head_norm(o, nw_ref, width):
        parts = []
        for hd in range(HEADS):
            s = o[:, hd * LANES:(hd + 1) * LANES]
            ms = jnp.sum(s * s, axis=-1, keepdims=True) * (1.0 / width)
            parts.append(s * lax.rsqrt(ms + NORM_EPS))
        return jnp.concatenate(parts, axis=-1) * nw_ref[...]

    both = lambda f_ref, b_ref: f_ref[...].astype(F32) + b_ref[...].astype(F32)
    gate = lambda lo, width: zg_ref[:, lo:lo + width].astype(F32)
    lane_q = lax.broadcasted_iota(jnp.int32, (TM_OUT, LANES), 1)
    lane_m = lax.broadcasted_iota(jnp.int32, (MEM_LEN, LANES), 1)
    first_q = lane_q < XA_DH
    first_m = lane_m < XA_DH
    q_head = (first_q, lane_q >= XA_DH)
    m_head = (first_m, lane_m >= XA_DH)
    heads = [(p, hh) for p in range(XA_HEADS // 2) for hh in range(2)]

    sc = []
    for p, hh in heads:
        qpair = xaq_ref[:, p * LANES:(p + 1) * LANES]
        qm = jnp.where(q_head[hh], qpair, jnp.zeros_like(qpair))
        sc.append(_dot_nt(qm, mkv_ref[0, :, p * LANES:(p + 1) * LANES]))

    o1 = head_norm(both(o1f_ref, o1b_ref), n1_ref, GLA_DV) * gate(0, HW)
    y = _dot(o1.astype(BF16), wout_ref[0:HW, :])

    pv = []
    for (p, hh), s in zip(heads, sc):
        e = jnp.exp(s - jnp.max(s, axis=-1, keepdims=True))
        l = jnp.sum(e, axis=-1, keepdims=True)
        mv = mkv_ref[0, :, XA_W + p * LANES:XA_W + (p + 1) * LANES]
        mvm = jnp.where(m_head[hh], mv, jnp.zeros_like(mv))
        pv.append(_dot(e.astype(BF16), mvm) * (1.0 / l))

    o2 = head_norm(both(o2f_ref, o2b_ref), n2_ref, GDN_DV) * gate(HW, HW)
    y = y + _dot(o2.astype(BF16), wout_ref[HW:2 * HW, :])

    for p in range(XA_HEADS // 2):
        lanes = slice(p * LANES, (p + 1) * LANES)
        acc = pv[2 * p] + pv[2 * p + 1]
        sq = acc * acc
        ss0 = jnp.sum(jnp.where(first_q, sq, 0.0), axis=-1, keepdims=True)
        ss1 = jnp.sum(jnp.where(first_q, 0.0, sq), axis=-1, keepdims=True)
        ms = jnp.where(first_q, ss0, ss1) * (1.0 / XA_DH)
        o3 = acc * lax.rsqrt(ms + NORM_EPS) * n3_ref[:, lanes] * gate(2 * HW + p * LANES, LANES)
        y = y + _dot(o3.astype(BF16), wout_ref[2 * HW + p * LANES:2 * HW + (p + 1) * LANES, :])

    xo = x_ref[...] + y
    if final:
        ms = jnp.mean(xo * xo, axis=-1, keepdims=True)
        xo = xo * lax.rsqrt(ms + NORM_EPS) * fn_ref[...]
    out_ref[...] = xo


def _outproj(layer, o1f, o1b, o2f, o2b, zg, xaq, mkv, x2d, wout, n1, n2, n3, fnw):
    n_tok = x2d.shape[0]
    blocks_per_seq = SEQ // TM_OUT
    tok = lambda w: pl.BlockSpec((TM_OUT, w), lambda i: (i, 0))
    const = lambda shape: pl.BlockSpec((None,) + shape, lambda i: (layer,) + tuple(0 for _ in shape))
    return pl.pallas_call(
        functools.partial(_outproj_kernel, final=layer == DEPTH - 1),
        grid=(n_tok // TM_OUT,),
        in_specs=[
            tok(HW), tok(HW), tok(HW), tok(HW), tok(MIX_PAD_W), tok(XA_W),
            pl.BlockSpec((None, 1, MEM_LEN, 2 * XA_W), lambda i: (layer, i // blocks_per_seq, 0, 0)),
            tok(D_MODEL),
            const((MIX_PAD_W, D_MODEL)),
            const((1, HW)), const((1, HW)), const((1, XA_W)),
            pl.BlockSpec((1, D_MODEL), lambda i: (0, 0)),
        ],
        out_specs=tok(D_MODEL),
        out_shape=jax.ShapeDtypeStruct((n_tok, D_MODEL), F32),
        compiler_params=pltpu.CompilerParams(dimension_semantics=("parallel",),
                                             vmem_limit_bytes=VMEM_LIMIT),
        name="outproj",
    )(o1f, o1b, o2f, o2b, zg, xaq, mkv, x2d, wout, n1, n2, n3, fnw)


def _pad_heads(w, axis):
    shp = w.shape
    w = w.reshape(shp[:axis] + (HEADS, GLA_DV) + shp[axis + 1:])
    pad = [(0, 0)] * w.ndim
    pad[axis + 1] = (0, LANES - GLA_DV)
    w = jnp.pad(w, pad)
    return w.reshape(shp[:axis] + (HW,) + shp[axis + 1:])


_SRC_SIZES = (GLA_QK_W, GLA_QK_W, HEADS * GLA_DV, HEADS * GLA_DV, 2 * GLA_RANK, 3 * HEADS * GDN_DK,
              HEADS * GDN_DV, 2 * GDN_HEADS, 2 * GDN_HEADS, XA_W, XA_W)
(S_GQ, S_GK, S_GV, S_GZ, S_LR, S_DQKV, S_DZ, S_DB, S_DA, S_XQ, S_XZ) = (
    sum(_SRC_SIZES[:n]) for n in range(len(_SRC_SIZES)))
IN_W = sum(_SRC_SIZES)
REPACK_ROWS = 256


def _repack_kernel(wt_ref, out_ref, w_ref):
    for lo in range(0, IN_W, LANES):
        width = min(LANES, IN_W - lo)
        w_ref[:, lo:lo + width] = wt_ref[lo:lo + width, :].T

    def src(lo, width):
        return w_ref[:, lo:lo + width].astype(BF16)

    rows = w_ref.shape[0]
    zeros = lambda width: jnp.zeros((rows, width), BF16)

    def put(dst, piece):
        out_ref[:, dst:dst + piece.shape[1]] = piece

    def put_heads(dst, lo):
        for hd in range(HEADS):
            put(dst + hd * LANES, jnp.concatenate([src(lo + hd * GLA_DV, GLA_DV), zeros(LANES - GLA_DV)], axis=1))

    put(C_GQ, src(S_GQ, GLA_QK_W))
    put(C_GK, src(S_GK, GLA_QK_W))
    put_heads(C_GV, S_GV)
    for part in range(3):
        put_heads(C_DQKV + part * HW, S_DQKV + part * HEADS * GDN_DK)
    put_heads(C_Z + HW, S_DZ)
    put(C_Z + 2 * HW, src(S_XZ, XA_W))
    put(C_XQ, src(S_XQ, XA_W))
    da = src(S_DA, 2 * GDN_HEADS)
    misc = jnp.concatenate(
        [src(S_DB, 2 * GDN_HEADS)] + [da] * M_A_COPIES
        + [zeros(M_LR - M_A - 8 * M_A_COPIES), src(S_LR, 2 * GLA_RANK), zeros(LANES - M_LR - 2 * GLA_RANK)],
        axis=1)
    for hd in range(HEADS):
        put(C_Z + hd * LANES, jnp.concatenate(
            [src(S_GZ + hd * GLA_DV, GLA_DV), misc[:, hd * MISC_PER_TILE:(hd + 1) * MISC_PER_TILE]], axis=1))


def _repack_w_in(w_in):
    return pl.pallas_call(
        _repack_kernel,
        grid=(DEPTH, D_MODEL // REPACK_ROWS),
        in_specs=[pl.BlockSpec((None, IN_W, REPACK_ROWS), lambda l, r: (l, 0, r))],
        out_specs=pl.BlockSpec((None, REPACK_ROWS, IN_PAD_W), lambda l, r: (l, r, 0)),
        out_shape=jax.ShapeDtypeStruct((DEPTH, D_MODEL, IN_PAD_W), BF16),
        scratch_shapes=[pltpu.VMEM((REPACK_ROWS, IN_W), F32)],
        compiler_params=pltpu.CompilerParams(dimension_semantics=("parallel", "parallel")),
        name="repack_w_in",
    )(jnp.swapaxes(w_in, 1, 2))


def _pack_params(norm_w, w_in, gla_w2, gla_b, gla_norm_w, gdn_conv_w, gdn_a_log, gdn_dt_bias, gdn_norm_w,
                 xa_norm_w, w_out):
    hd = HEADS * GDN_DK
    zeros = lambda *shape: jnp.zeros((DEPTH,) + shape, F32)
    w_all = _repack_w_in(w_in)

    cw = jnp.transpose(gdn_conv_w, (0, 2, 1))
    cw = jnp.concatenate([_pad_heads(cw[:, :, 0:hd], 2), _pad_heads(cw[:, :, hd:2 * hd], 2),
                          _pad_heads(cw[:, :, 2 * hd:], 2)], axis=2)
    cw = jnp.pad(cw, ((0, 0), (0, 8 - GDN_CONV), (0, 0)))

    w2bd = jnp.concatenate(
        [zeros(M_LR, 2 * GLA_QK_W),
         jnp.concatenate([gla_w2[:, 0], zeros(GLA_RANK, GLA_QK_W)], axis=2),
         jnp.concatenate([zeros(GLA_RANK, GLA_QK_W), gla_w2[:, 1]], axis=2),
         zeros(LANES - M_LR - 2 * GLA_RANK, 2 * GLA_QK_W)], axis=1).astype(BF16)
    glab = gla_b.reshape(DEPTH, 1, 2 * GLA_QK_W)

    def a_slab(p):
        flat = p.reshape(DEPTH, 1, 2 * GDN_HEADS)
        return jnp.concatenate([zeros(1, M_A)] + [flat] * M_A_COPIES + [zeros(1, LANES - M_A - 8 * M_A_COPIES)],
                               axis=2)

    gparams = jnp.concatenate([a_slab(gdn_a_log), a_slab(gdn_dt_bias), zeros(6, LANES)], axis=1)

    wout = jnp.concatenate([_pad_heads(w_out[:, 0:HEADS * GLA_DV], 1),
                            _pad_heads(w_out[:, HEADS * GLA_DV:HEADS * (GLA_DV + GDN_DV)], 1),
                            w_out[:, HEADS * (GLA_DV + GDN_DV):]], axis=1).astype(BF16)
    pad_norm = lambda w: jnp.tile(jnp.pad(w, ((0, 0), (0, LANES - w.shape[1]))), (1, HEADS)).reshape(DEPTH, 1, HW)
    n1 = pad_norm(gla_norm_w)
    n2 = pad_norm(gdn_norm_w)
    n3 = jnp.tile(xa_norm_w, (1, XA_HEADS)).reshape(DEPTH, 1, XA_W)
    return norm_w.reshape(DEPTH, 1, D_MODEL), w_all, cw, w2bd, glab, gparams, wout, n1, n2, n3


def kernel(x, mem, norm_w, w_in, gla_w2, gla_b, gla_norm_w, gdn_conv_w, gdn_a_log, gdn_dt_bias,
           gdn_norm_w, mem_norm_w, xa_w_kv, xa_norm_w, w_out, final_norm_w):
    assert x.shape == (BATCH, SEQ, D_MODEL) and mem.shape == (BATCH, MEM_LEN, D_MODEL)
    mkv = _memkv(mem, mem_norm_w, xa_w_kv.astype(BF16))
    nw, w_all, cw, w2bd, glab, gparams, wout, n1, n2, n3 = _pack_params(
        norm_w, w_in, gla_w2, gla_b, gla_norm_w, gdn_conv_w, gdn_a_log, gdn_dt_bias, gdn_norm_w,
        xa_norm_w, w_out)
    h = x.reshape(BATCH * SEQ, D_MODEL)
    fnw = final_norm_w.reshape(1, D_MODEL)
    for l in range(DEPTH):
        gla, v1, dec, zg, q2, k2, v2, gcol, grow, xaq = _inproj(l, h, nw, w_all, cw, w2bd, glab, gparams)
        o1f, o1b, o2f, o2b = _mix_scan(gla, v1, dec, q2, k2, v2, gcol, grow)
        h = _outproj(l, o1f, o1b, o2f, o2b, zg, xaq, mkv, h, wout, n1, n2, n3, fnw)
    return h.reshape(BATCH, SEQ, D_MODEL)
```

```python
import functools

import jax
import jax.numpy as jnp
from jax import lax
from jax.experimental import pallas as pl
from jax.experimental.pallas import tpu as pltpu

F32 = jnp.float32
BF16 = jnp.bfloat16

D_MODEL = 1024
BATCH = 8
SEQ = 4096
DEPTH = 2
MEM_LEN = 256
CHUNK = 64
NORM_EPS = 1e-6
GLA_HEADS = 4
GLA_DK = 64
GLA_DV = 96
GLA_RANK = 16
GLA_GATE_NORMALIZER = 16.0
GDN_HEADS = 4
GDN_DK = 96
GDN_DV = 96
GDN_CONV = 5
XA_HEADS = 4
XA_DH = 64

LANES = 128
HEADS = 4
HW = HEADS * LANES
GLA_QK_W = GLA_HEADS * GLA_DK
XA_W = XA_HEADS * XA_DH
MIX_PAD_W = 2 * HW + XA_W

C_GQ = 0
C_GK = C_GQ + GLA_QK_W
C_GV = C_GK + GLA_QK_W
C_DQKV = C_GV + HW
C_Z = C_DQKV + 3 * HW
C_XQ = C_Z + MIX_PAD_W
IN_PAD_W = C_XQ + XA_W
MISC_PER_TILE = LANES - GLA_DV

M_A = 8
M_A_COPIES = 4
M_LR = 64
G_BETA, G_GC, G_EGC, G_EKEND, G_DEC = 0, 8, 16, 24, 32
ROW_SLAB = 16

TM_IN = 512
TM_OUT = 512
NC = 8
TB = NC * CHUNK
HALO = 8
CONV_PHASES = 4
VMEM_LIMIT = 56 * 1024 * 1024


def _dot(a, b):
    return jnp.dot(a, b, preferred_element_type=F32)


def _dot_nt(a, b):
    return lax.dot_general(a, b, (((1,), (1,)), ((), ())), preferred_element_type=F32)


def _dot_tn(a, b):
    return lax.dot_general(a, b, (((0,), (0,)), ((), ())), preferred_element_type=F32)


def _sigmoid(x):
    return 0.5 * jnp.tanh(0.5 * x) + 0.5


def _silu(x):
    return x * _sigmoid(x)


def _softplus(x):
    return jnp.maximum(x, 0.0) + jnp.log(1.0 + jnp.exp(-jnp.abs(x)))


def _log_sigmoid(x):
    return jnp.minimum(x, 0.0) - jnp.log(1.0 + jnp.exp(-jnp.abs(x)))


def _split2(x):
    hi = pltpu.bitcast(pltpu.bitcast(x, jnp.int32) & jnp.int32(-65536), F32)
    return hi.astype(BF16), (x - hi).astype(BF16)


def _tri_sum(tri, x):
    hi, lo = _split2(x)
    return _dot(tri, hi) + _dot(tri, lo)


def _tri_masks():
    ri = lax.broadcasted_iota(jnp.int32, (CHUNK, CHUNK), 0)
    ci = lax.broadcasted_iota(jnp.int32, (CHUNK, CHUNK), 1)
    return ri, ci


def _memkv_kernel(mem_ref, nw_ref, w_ref, out_ref):
    m = mem_ref[0]
    ms = jnp.mean(m * m, axis=-1, keepdims=True)
    mn = (m * lax.rsqrt(ms + NORM_EPS) * nw_ref[0]).astype(BF16)
    out_ref[0, 0] = _dot(mn, w_ref[0]).astype(BF16)


def _memkv(mem, mem_norm_w, xa_w_kv_bf16):
    return pl.pallas_call(
        _memkv_kernel,
        grid=(DEPTH, BATCH),
        in_specs=[
            pl.BlockSpec((1, MEM_LEN, D_MODEL), lambda l, b: (b, 0, 0)),
            pl.BlockSpec((1, 1, D_MODEL), lambda l, b: (l, 0, 0)),
            pl.BlockSpec((1, D_MODEL, 2 * XA_W), lambda l, b: (l, 0, 0)),
        ],
        out_specs=pl.BlockSpec((1, 1, MEM_LEN, 2 * XA_W), lambda l, b: (l, b, 0, 0)),
        out_shape=jax.ShapeDtypeStruct((DEPTH, BATCH, MEM_LEN, 2 * XA_W), BF16),
        compiler_params=pltpu.CompilerParams(dimension_semantics=("parallel", "parallel")),
        name="memkv",
    )(mem, mem_norm_w.reshape(DEPTH, 1, D_MODEL), xa_w_kv_bf16)


def _inproj_kernel(x_ref, xp_ref, xn_ref, nw_ref, w_ref, cw_ref, w2_ref, gb_ref, gp_ref,
                   gla_ref, v1_ref, dec_ref, zg_ref, q2_ref, k2_ref, v2_ref, gcol_ref, grow_ref,
                   xaq_ref, ext_ref, conv_ref):
    nw = nw_ref[...]

    def norm(x):
        ms = jnp.mean(x * x, axis=-1, keepdims=True)
        return (x * lax.rsqrt(ms + NORM_EPS) * nw).astype(BF16)

    h = norm(x_ref[...])
    h_halo = norm(jnp.concatenate([xp_ref[...], xn_ref[...]], axis=0))

    def proj(hh, lo, width):
        return _dot(hh, w_ref[:, lo:lo + width])

    blocks_per_seq = SEQ // TM_IN
    j = lax.rem(pl.program_id(0), blocks_per_seq)
    cw = cw_ref[...]
    base = HALO - GDN_CONV // 2

    def gdn_mm(grp):
        lo = C_DQKV + grp * HW
        halo = proj(h_halo, lo, HW)
        return proj(h, lo, HW), halo[0:HALO], halo[HALO:]

    def gdn_vpu(grp, res):
        main, prev, nxt = res
        for hd in range(HEADS):
            slab = grp * HEADS + hd
            lanes = slice(hd * LANES, (hd + 1) * LANES)
            ext_ref[slab, 0:HALO, :] = jnp.where(j == 0, 0.0, prev[:, lanes])
            ext_ref[slab, HALO:HALO + TM_IN, :] = main[:, lanes]
            ext_ref[slab, HALO + TM_IN:2 * HALO + TM_IN, :] = jnp.where(j == blocks_per_seq - 1, 0.0,
                                                                     nxt[:, lanes])
            taps = cw[:, grp * HW + hd * LANES:grp * HW + (hd + 1) * LANES]
            for p in range(CONV_PHASES):
                acc = None
                for t in range(GDN_CONV):
                    win = ext_ref[slab, pl.ds(base + p + t, TM_IN // CONV_PHASES, stride=CONV_PHASES), :]
                    acc = win * taps[t:t + 1, :] if acc is None else acc + win * taps[t:t + 1, :]
                conv_ref[slab, pl.ds(p, TM_IN // CONV_PHASES, stride=CONV_PHASES), :] = acc
            y = _silu(conv_ref[slab])
            if grp == 2:
                v2_ref[:, lanes] = y.astype(BF16)
            else:
                ref, scale = ((q2_ref, GDN_DK ** -0.5), (k2_ref, 1.0))[grp]
                ss = jnp.sum(y * y, axis=-1, keepdims=True)
                ref[:, lanes] = (y * lax.rsqrt(ss + NORM_EPS) * scale).astype(BF16)

    def z_vpu(lo, width, res):
        zg_ref[:, lo:lo + width] = _silu(res).astype(BF16)

    def gla_z_and_misc_vpu(res):
        z_vpu(0, HW, res)
        misc_vpu(jnp.concatenate([res[:, hd * LANES + GLA_DV:(hd + 1) * LANES] for hd in range(HEADS)], axis=1))

    def xaq_vpu(res):
        xaq_ref[...] = (res * (XA_DH ** -0.5)).astype(BF16)

    lane = lax.broadcasted_iota(jnp.int32, (1, LANES), 1)
    fwd_lane = lax.rem(lane, 8) < GDN_HEADS
    ri, ci = _tri_masks()
    lower = jnp.where(ri >= ci, 1.0, 0.0).astype(BF16)
    upper = jnp.where(ri <= ci, 1.0, 0.0).astype(BF16)
    lower_upper = jnp.concatenate([lower, upper], axis=0)
    gate = {}

    def misc_vpu(m):
        logits = _dot(m.astype(BF16), w2_ref[...]) + gb_ref[...]
        gate["g"] = _log_sigmoid(logits) * (1.0 / GLA_GATE_NORMALIZER)
        is_a = (lane >= M_A) & (lane < M_A + 8 * M_A_COPIES)
        neg_a = jnp.where(is_a, -jnp.exp(gp_ref[0:1, :]), 0.0)
        gg = neg_a * _softplus(m + gp_ref[1:2, :])
        beta = _sigmoid(m)
        for c in range(TM_IN // CHUNK):
            rows = slice(c * CHUNK, (c + 1) * CHUNK)
            ggc = gg[rows]
            hi, lo = _split2(ggc)
            both = _dot(lower_upper, jnp.concatenate([hi, lo], axis=1))
            pf = both[0:CHUNK, 0:LANES] + both[0:CHUNK, LANES:]
            sf = both[CHUNK:, 0:LANES] + both[CHUNK:, LANES:]
            gc = jnp.where(fwd_lane, pf, sf)
            last = jnp.where(fwd_lane, pf[CHUNK - 1:CHUNK], sf[0:1])
            col = jnp.where(lane < G_GC, beta[rows],
                  jnp.where(lane < G_EGC, gc,
                  jnp.where(lane < G_EKEND, jnp.exp(gc),
                  jnp.where(lane < G_DEC, jnp.exp(last - gc), jnp.exp(last)))))
            gcol_ref[rows, :] = col
            grow_ref[c] = col.T[0:ROW_SLAB, :]

    def gla_v_vpu(res):
        v1_ref[...] = res.astype(BF16)

    def gla_qk_vpu(pg):
        q1 = pg[:, 0:GLA_QK_W] * (GLA_DK ** -0.5)
        k1 = pg[:, GLA_QK_W:2 * GLA_QK_W]
        g = gate["g"]
        for c in range(TM_IN // CHUNK):
            rows = slice(c * CHUNK, (c + 1) * CHUNK)
            gch = g[rows]
            q1c = q1[rows]
            k1c = k1[rows]
            bf = _tri_sum(lower, gch[:, 0:GLA_QK_W])
            br = _tri_sum(upper, gch[:, GLA_QK_W:])
            for d, (b, last) in enumerate(((bf, bf[CHUNK - 1:CHUNK]), (br, br[0:1]))):
                off = d * 3 * GLA_QK_W
                gla_ref[rows, off:off + GLA_QK_W] = (q1c * jnp.exp(b)).astype(BF16)
                gla_ref[rows, off + GLA_QK_W:off + 2 * GLA_QK_W] = (k1c * jnp.exp(-b)).astype(BF16)
                gla_ref[rows, off + 2 * GLA_QK_W:off + 3 * GLA_QK_W] = (k1c * jnp.exp(last - b)).astype(BF16)
                dec_ref[c, :, d * GLA_QK_W:(d + 1) * GLA_QK_W] = jnp.exp(last)

    part = functools.partial
    tasks = [
        (part(proj, h, C_Z, HW), gla_z_and_misc_vpu),
        (part(gdn_mm, 0), part(gdn_vpu, 0)),
        (part(gdn_mm, 1), part(gdn_vpu, 1)),
        (part(proj, h, C_Z + HW, HW), part(z_vpu, HW, HW)),
        (part(gdn_mm, 2), part(gdn_vpu, 2)),
        (part(proj, h, C_Z + 2 * HW, XA_W), part(z_vpu, 2 * HW, XA_W)),
        (part(proj, h, C_GQ, 2 * GLA_QK_W), gla_qk_vpu),
        (part(proj, h, C_GV, HW), gla_v_vpu),
        (part(proj, h, C_XQ, XA_W), xaq_vpu),
    ]
    res = tasks[0][0]()
    for t, (_, vpu) in enumerate(tasks):
        nxt_res = tasks[t + 1][0]() if t + 1 < len(tasks) else None
        vpu(res)
        res = nxt_res


def _inproj(layer, x2d, norm_w, w_all, conv_w, w2bd, gla_b, gdn_params):
    n_tok = x2d.shape[0]
    nblk = n_tok // TM_IN
    halo_blocks = TM_IN // HALO
    n_halo = n_tok // HALO
    nchunk = TM_IN // CHUNK
    tok = lambda w: pl.BlockSpec((TM_IN, w), lambda i: (i, 0))
    const = lambda shape: pl.BlockSpec((None,) + shape, lambda i: (layer,) + tuple(0 for _ in shape))
    out_shapes = (
        jax.ShapeDtypeStruct((n_tok, 6 * GLA_QK_W), BF16),
        jax.ShapeDtypeStruct((n_tok, HW), BF16),
        jax.ShapeDtypeStruct((n_tok // CHUNK, 1, 2 * GLA_QK_W), F32),
        jax.ShapeDtypeStruct((n_tok, MIX_PAD_W), BF16),
        jax.ShapeDtypeStruct((n_tok, HW), BF16),
        jax.ShapeDtypeStruct((n_tok, HW), BF16),
        jax.ShapeDtypeStruct((n_tok, HW), BF16),
        jax.ShapeDtypeStruct((n_tok, LANES), F32),
        jax.ShapeDtypeStruct((n_tok // CHUNK, ROW_SLAB, CHUNK), F32),
        jax.ShapeDtypeStruct((n_tok, XA_W), BF16),
    )
    out_specs = (
        tok(6 * GLA_QK_W), tok(HW),
        pl.BlockSpec((nchunk, 1, 2 * GLA_QK_W), lambda i: (i, 0, 0)),
        tok(MIX_PAD_W), tok(HW), tok(HW), tok(HW), tok(LANES),
        pl.BlockSpec((nchunk, ROW_SLAB, CHUNK), lambda i: (i, 0, 0)),
        tok(XA_W),
    )
    return pl.pallas_call(
        _inproj_kernel,
        grid=(nblk,),
        in_specs=[
            tok(D_MODEL),
            pl.BlockSpec((HALO, D_MODEL), lambda i: (jnp.maximum(i * halo_blocks - 1, 0), 0)),
            pl.BlockSpec((HALO, D_MODEL), lambda i: (jnp.minimum((i + 1) * halo_blocks, n_halo - 1), 0)),
            const((1, D_MODEL)),
            const((D_MODEL, IN_PAD_W)),
            const((8, 3 * HW)),
            const((LANES, 2 * GLA_QK_W)),
            const((1, 2 * GLA_QK_W)),
            const((8, LANES)),
        ],
        out_specs=out_specs,
        out_shape=out_shapes,
        scratch_shapes=[pltpu.VMEM((3 * HEADS, TM_IN + 2 * HALO, LANES), F32),
                        pltpu.VMEM((3 * HEADS, TM_IN, LANES), F32)],
        compiler_params=pltpu.CompilerParams(dimension_semantics=("parallel",),
                                             vmem_limit_bytes=VMEM_LIMIT),
        name="inproj",
    )(x2d, x2d, x2d, norm_w, w_all, conv_w, w2bd, gla_b, gdn_params)


def _gla_stages(gf_ref, gb_ref, vf_ref, vb_ref, decf_ref, decb_ref, of_ref, ob_ref, s_ref):
    ri = lax.broadcasted_iota(jnp.int32, (CHUNK, LANES), 0)
    lane_tok = lax.broadcasted_iota(jnp.int32, (CHUNK, LANES), 1)
    ci = lane_tok & (CHUNK - 1)
    incl = (ri >= ci, ri <= ci)
    left = lane_tok < GLA_DK
    lane_sq = lax.broadcasted_iota(jnp.int32, (LANES, LANES), 1) < GLA_DK
    zero_bf = jnp.zeros((CHUNK, LANES), BF16)
    cat = jnp.concatenate

    def split_heads(x):
        return cat([jnp.where(left, x, zero_bf), jnp.where(left, zero_bf, x)], axis=0)

    g_refs, v_refs = (gf_ref, gb_ref), (vf_ref, vb_ref)
    dec_refs, o_refs = (decf_ref, decb_ref), (of_ref, ob_ref)
    chunk_of = lambda d, step: step if d == 0 else NC - 1 - step
    units_of = lambda step: [(d, step, p) for d in range(2) for p in range(GLA_HEADS // 2)]
    qe, kend, v, a, intra, kv = {}, {}, {}, {}, {}, {}
    st = {(d, p): s_ref[d, p] for d in range(2) for p in range(GLA_HEADS // 2)}

    def scores(step):
        for u in units_of(step):
            d, _, p = u
            c = chunk_of(d, step)
            rows = slice(c * CHUNK, (c + 1) * CHUNK)
            qe[u] = g_refs[d][rows, p * LANES:(p + 1) * LANES]
            ke = g_refs[d][rows, GLA_QK_W + p * LANES:GLA_QK_W + (p + 1) * LANES]
            kend[u] = g_refs[d][rows, 2 * GLA_QK_W + p * LANES:2 * GLA_QK_W + (p + 1) * LANES]
            v[u] = (v_refs[d][rows, 2 * p * LANES:(2 * p + 1) * LANES],
                    v_refs[d][rows, (2 * p + 1) * LANES:(2 * p + 2) * LANES])
            a[u] = jnp.where(incl[d], _dot_nt(qe[u], split_heads(ke)), 0.0).astype(BF16)

    def values(step):
        for u in units_of(step):
            intra[u] = _dot(a[u], cat([cat([v[u][0], zero_bf], axis=1), cat([zero_bf, v[u][1]], axis=1)],
                                      axis=0))
            kv[u] = _dot_tn(cat([v[u][0], v[u][1]], axis=1), kend[u])

    def outputs(step):
        for u in units_of(step):
            d, _, p = u
            c = chunk_of(d, step)
            inter = _dot_nt(split_heads(qe[u]), st[(d, p)].astype(BF16))
            for hh in range(2):
                hd = 2 * p + hh
                o = intra[u][:, hh * LANES:(hh + 1) * LANES] + inter[hh * CHUNK:(hh + 1) * CHUNK]
                o_refs[d][c * CHUNK:(c + 1) * CHUNK, hd * LANES:(hd + 1) * LANES] = o.astype(o_refs[d].dtype)
            dec = dec_refs[d][c, :, p * LANES:(p + 1) * LANES]
            st[(d, p)] = dec * st[(d, p)] + jnp.where(lane_sq, kv[u][0:LANES], kv[u][LANES:])

    def finish():
        for (d, p), s in st.items():
            s_ref[d, p] = s

    return scores, values, outputs, finish


def _gdn_stages(qf_ref, kf_ref, vf_ref, colf_ref, rowf_ref, qb_ref, kb_ref, vb_ref, colb_ref, rowb_ref,
                of_ref, ob_ref, s_ref):
    pk = GDN_HEADS * CHUNK
    ri = lax.broadcasted_iota(jnp.int32, (CHUNK, pk), 0)
    ci = lax.broadcasted_iota(jnp.int32, (CHUNK, pk), 1) & (CHUNK - 1)
    incl = (ri >= ci, ri <= ci)
    strict = (ri > ci, ri < ci)
    eye = jnp.where(ri == ci, 1.0, 0.0)
    blk = lambda axis: lax.shift_right_logical(lax.broadcasted_iota(jnp.int32, (pk, pk), axis), 6)
    same_blk = blk(0) == blk(1)
    bd_mask = jnp.where(same_blk, 1.0, 0.0).astype(BF16)
    left = lax.broadcasted_iota(jnp.int32, (CHUNK, LANES), 1) < CHUNK
    zero_tok = jnp.zeros((CHUNK, LANES), BF16)
    zero_sq = jnp.zeros((LANES, LANES), BF16)

    refs = ((qf_ref, kf_ref, vf_ref, colf_ref, rowf_ref, of_ref),
            (qb_ref, kb_ref, vb_ref, colb_ref, rowb_ref, ob_ref))
    groups = [(d, step) for step in range(NC) for d in range(2)]
    pairs = range(GDN_HEADS // 2)
    top = lambda m: m[0:CHUNK]
    bot = lambda m: m[CHUNK:2 * CHUNK]
    cat = jnp.concatenate
    chunk_of = lambda d, step: step if d == 0 else NC - 1 - step

    def block_diag(x):
        return cat([x, x, x, x], axis=0) * bd_mask

    def diag2(a, b, zero):
        return cat([cat([a, zero], axis=1), cat([zero, b], axis=1)], axis=0)

    q, k, v, col, gc_r = [], [], [], [], []
    for d, step in groups:
        q_ref, k_ref, v_ref, col_ref, row_ref, _ = refs[d]
        c = chunk_of(d, step)
        rows = slice(c * CHUNK, (c + 1) * CHUNK)
        heads = [slice(hd * LANES, (hd + 1) * LANES) for hd in range(GDN_HEADS)]
        q.append([q_ref[rows, h] for h in heads])
        k.append([k_ref[rows, h] for h in heads])
        v.append([v_ref[rows, h] for h in heads])
        col.append(col_ref[rows, :])
        rowt = row_ref[c]
        gc_r.append(cat([rowt[8 + d * GDN_HEADS + hd:9 + d * GDN_HEADS + hd, :] for hd in range(GDN_HEADS)],
                        axis=1))

    def colv(g, base, hd):
        lane = base + groups[g][0] * GDN_HEADS + hd
        return col[g][:, lane:lane + 1]

    def col_packed(g, base):
        tiles = [jnp.where(left, jnp.broadcast_to(colv(g, base, 2 * j), (CHUNK, LANES)),
                           jnp.broadcast_to(colv(g, base, 2 * j + 1), (CHUNK, LANES))) for j in pairs]
        return cat(tiles, axis=1)

    qk, uw = {}, {}

    def rhs_of(g, hd):
        beta = colv(g, G_BETA, hd)
        return cat([(v[g][hd].astype(F32) * beta).astype(BF16),
                    (k[g][hd].astype(F32) * (beta * colv(g, G_EGC, hd))).astype(BF16)], axis=1)

    def pre(gl):
        kq = {g: [_dot_nt(cat([cat([k[g][2 * j], k[g][2 * j + 1]], axis=1),
                               cat([q[g][2 * j], q[g][2 * j + 1]], axis=1)], axis=0),
                          diag2(k[g][2 * j], k[g][2 * j + 1], zero_tok)) for j in pairs] for g in gl}
        yield
        pm, cur = {}, {}
        for g in gl:
            d = groups[g][0]
            kk = cat([top(kq[g][j]) for j in pairs], axis=1)
            qkr = cat([bot(kq[g][j]) for j in pairs], axis=1)
            decay = jnp.where(incl[d], jnp.exp(jnp.where(incl[d], col_packed(g, G_GC) - gc_r[g], 0.0)), 0.0)
            n = jnp.where(strict[d], -(kk * decay * col_packed(g, G_BETA)), 0.0)
            qk[g] = jnp.where(incl[d], qkr * decay, 0.0).astype(BF16)
            pm[g] = eye + n
            cur[g] = _dot(n.astype(BF16), block_diag(n.astype(BF16)))
        yield
        for _ in range(4):
            st = {g: _dot(cat([pm[g].astype(BF16), cur[g].astype(BF16)], axis=0),
                          block_diag(cur[g].astype(BF16))) for g in gl}
            for g in gl:
                pm[g] = pm[g] + top(st[g])
                cur[g] = bot(st[g])
            yield
        tinv = {g: pm[g] + _dot(pm[g].astype(BF16), block_diag(cur[g].astype(BF16))) for g in gl}
        yield
        for g in gl:
            uw[g] = []
            for j in pairs:
                t = tinv[g][:, j * LANES:(j + 1) * LANES]
                lhs = cat([jnp.where(left, t, 0.0), jnp.where(left, 0.0, t)], axis=0).astype(BF16)
                uw[g].append(_dot(lhs, cat([rhs_of(g, 2 * j), rhs_of(g, 2 * j + 1)], axis=0)))
        yield

    def scan(step):
        gs = [g for g, (_, st_) in enumerate(groups) if st_ == step]
        units = [(g, j) for g in gs for j in pairs]
        s = {(g, hd): s_ref[groups[g][0], hd] for g in gs for hd in range(GDN_HEADS)}
        r = {}
        for g, j in units:
            a, b = 2 * j, 2 * j + 1
            x = uw[g][j]
            lhs = cat([cat([x[0:CHUNK, LANES:].astype(BF16), x[CHUNK:, LANES:].astype(BF16)], axis=1),
                       cat([q[g][a], q[g][b]], axis=1)], axis=0)
            r[(g, j)] = _dot(lhs, diag2(s[(g, a)].astype(BF16), s[(g, b)].astype(BF16), zero_sq))
        yield
        vn = {}
        for g, j in units:
            x, rr = uw[g][j], r[(g, j)]
            vn[(g, 2 * j)] = x[0:CHUNK, 0:LANES] - rr[0:CHUNK, 0:LANES]
            vn[(g, 2 * j + 1)] = x[CHUNK:, 0:LANES] - rr[0:CHUNK, LANES:]
        for g, j in units:
            d = groups[g][0]
            a, b = 2 * j, 2 * j + 1
            intra = _dot(qk[g][:, j * LANES:(j + 1) * LANES],
                         diag2(vn[(g, a)].astype(BF16), vn[(g, b)].astype(BF16), zero_tok))
            c = chunk_of(d, step)
            for hd, lanes in ((a, slice(0, LANES)), (b, slice(LANES, 2 * LANES))):
                o = colv(g, G_EGC, hd) * r[(g, j)][CHUNK:, lanes] + intra[:, lanes]
                o_ref = refs[d][5]
                o_ref[c * CHUNK:(c + 1) * CHUNK, hd * LANES:(hd + 1) * LANES] = o.astype(o_ref.dtype)
        for g in gs:
            d = groups[g][0]
            for hd in range(GDN_HEADS):
                dec = col[g][0:1, G_DEC + d * GDN_HEADS + hd:G_DEC + d * GDN_HEADS + hd + 1]
                kv = _dot_tn(k[g][hd], (colv(g, G_EKEND, hd) * vn[(g, hd)]).astype(BF16))
                s_ref[d, hd] = dec * s[(g, hd)] + kv
        yield

    groups_of = lambda lo, hi: [g for g, (_, st_) in enumerate(groups) if lo <= st_ < hi]
    return pre, scan, groups_of


def _scan_kernel(gf_ref, gb_ref, v1f_ref, v1b_ref, decf_ref, decb_ref,
                 qf_ref, kf_ref, vf_ref, colf_ref, rowf_ref, qb_ref, kb_ref, vb_ref, colb_ref, rowb_ref,
                 o1f_ref, o1b_ref, o2f_ref, o2b_ref, s1_ref, s2_ref):
    @pl.when(pl.program_id(1) == 0)
    def _():
        s1_ref[...] = jnp.zeros_like(s1_ref)
        s2_ref[...] = jnp.zeros_like(s2_ref)

    gla_scores, gla_values, gla_outputs, gla_finish = _gla_stages(
        gf_ref, gb_ref, v1f_ref, v1b_ref, decf_ref, decb_ref, o1f_ref, o1b_ref, s1_ref)
    pre, scan, groups_of = _gdn_stages(qf_ref, kf_ref, vf_ref, colf_ref, rowf_ref,
                                       qb_ref, kb_ref, vb_ref, colb_ref, rowb_ref, o2f_ref, o2b_ref, s2_ref)

    half = NC // 2
    for _ in pre(groups_of(0, half)):
        pass
    later = pre(groups_of(half, NC))
    for step in range(half):
        for _ in scan(step):
            next(later, None)
    for _ in later:
        pass
    gla_scores(0)
    gla_step = 0
    for step in range(half, NC):
        for _ in scan(step):
            if gla_step + 1 < NC:
                gla_scores(gla_step + 1)
            gla_values(gla_step)
            gla_outputs(gla_step)
            gla_step += 1
    assert gla_step == NC
    gla_finish()


def _mix_scan(gla, v1, dec, q2, k2, v2, gcol, grow):
    nb = SEQ // TB
    n_tok = q2.shape[0]
    fwd = lambda b, i: (b * nb + i, 0)
    bwd = lambda b, i: (b * nb + nb - 1 - i, 0)
    fwd3 = lambda b, i: (b * nb + i, 0, 0)
    bwd3 = lambda b, i: (b * nb + nb - 1 - i, 0, 0)

    def gdn_specs(m2, m3):
        return [pl.BlockSpec((TB, HW), m2), pl.BlockSpec((TB, HW), m2), pl.BlockSpec((TB, HW), m2),
                pl.BlockSpec((TB, LANES), m2), pl.BlockSpec((NC, ROW_SLAB, CHUNK), m3)]

    gla_specs = [
        pl.BlockSpec((TB, 3 * GLA_QK_W), fwd),
        pl.BlockSpec((TB, 3 * GLA_QK_W), lambda b, i: (b * nb + nb - 1 - i, 1)),
        pl.BlockSpec((TB, HW), fwd),
        pl.BlockSpec((TB, HW), bwd),
        pl.BlockSpec((NC, 1, GLA_QK_W), lambda b, i: (b * nb + i, 0, 0)),
        pl.BlockSpec((NC, 1, GLA_QK_W), lambda b, i: (b * nb + nb - 1 - i, 0, 1)),
    ]
    out = jax.ShapeDtypeStruct((n_tok, HW), BF16)
    return pl.pallas_call(
        _scan_kernel,
        grid=(BATCH, nb),
        in_specs=gla_specs + gdn_specs(fwd, fwd3) + gdn_specs(bwd, bwd3),
        out_specs=(pl.BlockSpec((TB, HW), fwd), pl.BlockSpec((TB, HW), bwd),
                   pl.BlockSpec((TB, HW), fwd), pl.BlockSpec((TB, HW), bwd)),
        out_shape=(out, out, out, out),
        scratch_shapes=[pltpu.VMEM((2, GLA_HEADS // 2, LANES, LANES), F32),
                        pltpu.VMEM((2, GDN_HEADS, LANES, LANES), F32)],
        compiler_params=pltpu.CompilerParams(dimension_semantics=("parallel", "arbitrary"),
                                             vmem_limit_bytes=VMEM_LIMIT),
        name="mix_scan",
    )(gla, gla, v1, v1, dec, dec, q2, k2, v2, gcol, grow, q2, k2, v2, gcol, grow)


def _outproj_kernel(o1f_ref, o1b_ref, o2f_ref, o2b_ref, zg_ref, xaq_ref, mkv_ref, x_ref, wout_ref,
                    n1_ref, n2_ref, n3_ref, fn_ref, out_ref, *, final):
    def head_norm(o, nw_ref, width):
        parts = []
        for hd in range(HEADS):
            s = o[:, hd * LANES:(hd + 1) * LANES]
            ms = jnp.sum(s * s, axis=-1, keepdims=True) * (1.0 / width)
            parts.append(s * lax.rsqrt(ms + NORM_EPS))
        return jnp.concatenate(parts, axis=-1) * nw_ref[...]

    both = lambda f_ref, b_ref: f_ref[...].astype(F32) + b_ref[...].astype(F32)
    gate = lambda lo, width: zg_ref[:, lo:lo + width].astype(F32)
    lane_q = lax.broadcasted_iota(jnp.int32, (TM_OUT, LANES), 1)
    lane_m = lax.broadcasted_iota(jnp.int32, (MEM_LEN, LANES), 1)
    first_q = lane_q < XA_DH
    first_m = lane_m < XA_DH
    q_head = (first_q, lane_q >= XA_DH)
    m_head = (first_m, lane_m >= XA_DH)
    heads = [(p, hh) for p in range(XA_HEADS // 2) for hh in range(2)]

    sc = []
    for p, hh in heads:
        qpair = xaq_ref[:, p * LANES:(p + 1) * LANES]
        qm = jnp.where(q_head[hh], qpair, jnp.zeros_like(qpair))
        sc.append(_dot_nt(qm, mkv_ref[0, :, p * LANES:(p + 1) * LANES]))

    o1 = head_norm(both(o1f_ref, o1b_ref), n1_ref, GLA_DV) * gate(0, HW)
    y = _dot(o1.astype(BF16), wout_ref[0:HW, :])

    pv = []
    for (p, hh), s in zip(heads, sc):
        e = jnp.exp(s - jnp.max(s, axis=-1, keepdims=True))
        l = jnp.sum(e, axis=-1, keepdims=True)
        mv = mkv_ref[0, :, XA_W + p * LANES:XA_W + (p + 1) * LANES]
        mvm = jnp.where(m_head[hh], mv, jnp.zeros_like(mv))
        pv.append(_dot(e.astype(BF16), mvm) * (1.0 / l))

    o2 = head_norm(both(o2f_ref, o2b_ref), n2_ref, GDN_DV) * gate(HW, HW)
    y = y + _dot(o2.astype(BF16), wout_ref[HW:2 * HW, :])

    for p in range(XA_HEADS // 2):
        lanes = slice(p * LANES, (p + 1) * LANES)
        acc = pv[2 * p] + pv[2 * p + 1]
        sq = acc * acc
        ss0 = jnp.sum(jnp.where(first_q, sq, 0.0), axis=-1, keepdims=True)
        ss1 = jnp.sum(jnp.where(first_q, 0.0, sq), axis=-1, keepdims=True)
        ms = jnp.where(first_q, ss0, ss1) * (1.0 / XA_DH)
        o3 = acc * lax.rsqrt(ms + NORM_EPS) * n3_ref[:, lanes] * gate(2 * HW + p * LANES, LANES)
        y = y + _dot(o3.astype(BF16), wout_ref[2 * HW + p * LANES:2 * HW + (p + 1) * LANES, :])

    xo = x_ref[...] + y
    if final:
        ms = jnp.mean(xo * xo, axis=-1, keepdims=True)
        xo = xo * lax.rsqrt(ms + NORM_EPS) * fn_ref[...]
    out_ref[...] = xo


def _outproj(layer, o1f, o1b, o2f, o2b, zg, xaq, mkv, x2d, wout, n1, n2, n3, fnw):
    n_tok = x2d.shape[0]
    blocks_per_seq = SEQ // TM_OUT
    tok = lambda w: pl.BlockSpec((TM_OUT, w), lambda i: (i, 0))
    const = lambda shape: pl.BlockSpec((None,) + shape, lambda i: (layer,) + tuple(0 for _ in shape))
    return pl.pallas_call(
        functools.partial(_outproj_kernel, final=layer == DEPTH - 1),
        grid=(n_tok // TM_OUT,),
        in_specs=[
            tok(HW), tok(HW), tok(HW), tok(HW), tok(MIX_PAD_W), tok(XA_W),
            pl.BlockSpec((None, 1, MEM_LEN, 2 * XA_W), lambda i: (layer, i // blocks_per_seq, 0, 0)),
            tok(D_MODEL),
            const((MIX_PAD_W, D_MODEL)),
            const((1, HW)), const((1, HW)), const((1, XA_W)),
            pl.BlockSpec((1, D_MODEL), lambda i: (0, 0)),
        ],
        out_specs=tok(D_MODEL),
        out_shape=jax.ShapeDtypeStruct((n_tok, D_MODEL), F32),
        compiler_params=pltpu.CompilerParams(dimension_semantics=("parallel",),
                                             vmem_limit_bytes=VMEM_LIMIT),
        name="outproj",
    )(o1f, o1b, o2f, o2b, zg, xaq, mkv, x2d, wout, n1, n2, n3, fnw)


def _pad_heads(w, axis):
    shp = w.shape
    w = w.reshape(shp[:axis] + (HEADS, GLA_DV) + shp[axis + 1:])
    pad = [(0, 0)] * w.ndim
    pad[axis + 1] = (0, LANES - GLA_DV)
    w = jnp.pad(w, pad)
    return w.reshape(shp[:axis] + (HW,) + shp[axis + 1:])


_SRC_SIZES = (GLA_QK_W, GLA_QK_W, HEADS * GLA_DV, HEADS * GLA_DV, 2 * GLA_RANK, 3 * HEADS * GDN_DK,
              HEADS * GDN_DV, 2 * GDN_HEADS, 2 * GDN_HEADS, XA_W, XA_W)
(S_GQ, S_GK, S_GV, S_GZ, S_LR, S_DQKV, S_DZ, S_DB, S_DA, S_XQ, S_XZ) = (
    sum(_SRC_SIZES[:n]) for n in range(len(_SRC_SIZES)))
IN_W = sum(_SRC_SIZES)
REPACK_ROWS = 256


def _repack_kernel(wt_ref, out_ref, w_ref):
    for lo in range(0, IN_W, LANES):
        width = min(LANES, IN_W - lo)
        w_ref[:, lo:lo + width] = wt_ref[lo:lo + width, :].T

    def src(lo, width):
        return w_ref[:, lo:lo + width].astype(BF16)

    rows = w_ref.shape[0]
    zeros = lambda width: jnp.zeros((rows, width), BF16)

    def put(dst, piece):
        out_ref[:, dst:dst + piece.shape[1]] = piece

    def put_heads(dst, lo):
        for hd in range(HEADS):
            put(dst + hd * LANES, jnp.concatenate([src(lo + hd * GLA_DV, GLA_DV), zeros(LANES - GLA_DV)], axis=1))

    put(C_GQ, src(S_GQ, GLA_QK_W))
    put(C_GK, src(S_GK, GLA_QK_W))
    put_heads(C_GV, S_GV)
    for part in range(3):
        put_heads(C_DQKV + part * HW, S_DQKV + part * HEADS * GDN_DK)
    put_heads(C_Z + HW, S_DZ)
    put(C_Z + 2 * HW, src(S_XZ, XA_W))
    put(C_XQ, src(S_XQ, XA_W))
    da = src(S_DA, 2 * GDN_HEADS)
    misc = jnp.concatenate(
        [src(S_DB, 2 * GDN_HEADS)] + [da] * M_A_COPIES
        + [zeros(M_LR - M_A - 8 * M_A_COPIES), src(S_LR, 2 * GLA_RANK), zeros(LANES - M_LR - 2 * GLA_RANK)],
        axis=1)
    for hd in range(HEADS):
        put(C_Z + hd * LANES, jnp.concatenate(
            [src(S_GZ + hd * GLA_DV, GLA_DV), misc[:, hd * MISC_PER_TILE:(hd + 1) * MISC_PER_TILE]], axis=1))


def _repack_w_in(w_in):
    return pl.pallas_call(
        _repack_kernel,
        grid=(DEPTH, D_MODEL // REPACK_ROWS),
        in_specs=[pl.BlockSpec((None, IN_W, REPACK_ROWS), lambda l, r: (l, 0, r))],
        out_specs=pl.BlockSpec((None, REPACK_ROWS, IN_PAD_W), lambda l, r: (l, r, 0)),
        out_shape=jax.ShapeDtypeStruct((DEPTH, D_MODEL, IN_PAD_W), BF16),
        scratch_shapes=[pltpu.VMEM((REPACK_ROWS, IN_W), F32)],
        compiler_params=pltpu.CompilerParams(dimension_semantics=("parallel", "parallel")),
        name="repack_w_in",
    )(jnp.swapaxes(w_in, 1, 2))


def _pack_params(norm_w, w_in, gla_w2, gla_b, gla_norm_w, gdn_conv_w, gdn_a_log, gdn_dt_bias, gdn_norm_w,
                 xa_norm_w, w_out):
    hd = HEADS * GDN_DK
    zeros = lambda *shape: jnp.zeros((DEPTH,) + shape, F32)
    w_all = _repack_w_in(w_in)

    cw = jnp.transpose(gdn_conv_w, (0, 2, 1))
    cw = jnp.concatenate([_pad_heads(cw[:, :, 0:hd], 2), _pad_heads(cw[:, :, hd:2 * hd], 2),
                          _pad_heads(cw[:, :, 2 * hd:], 2)], axis=2)
    cw = jnp.pad(cw, ((0, 0), (0, 8 - GDN_CONV), (0, 0)))

    w2bd = jnp.concatenate(
        [zeros(M_LR, 2 * GLA_QK_W),
         jnp.concatenate([gla_w2[:, 0], zeros(GLA_RANK, GLA_QK_W)], axis=2),
         jnp.concatenate([zeros(GLA_RANK, GLA_QK_W), gla_w2[:, 1]], axis=2),
         zeros(LANES - M_LR - 2 * GLA_RANK, 2 * GLA_QK_W)], axis=1).astype(BF16)
    glab = gla_b.reshape(DEPTH, 1, 2 * GLA_QK_W)

    def a_slab(p):
        flat = p.reshape(DEPTH, 1, 2 * GDN_HEADS)
        return jnp.concatenate([zeros(1, M_A)] + [flat] * M_A_COPIES + [zeros(1, LANES - M_A - 8 * M_A_COPIES)],
                               axis=2)

    gparams = jnp.concatenate([a_slab(gdn_a_log), a_slab(gdn_dt_bias), zeros(6, LANES)], axis=1)

    wout = jnp.concatenate([_pad_heads(w_out[:, 0:HEADS * GLA_DV], 1),
                            _pad_heads(w_out[:, HEADS * GLA_DV:HEADS * (GLA_DV + GDN_DV)], 1),
                            w_out[:, HEADS * (GLA_DV + GDN_DV):]], axis=1).astype(BF16)
    pad_norm = lambda w: jnp.tile(jnp.pad(w, ((0, 0), (0, LANES - w.shape[1]))), (1, HEADS)).reshape(DEPTH, 1, HW)
    n1 = pad_norm(gla_norm_w)
    n2 = pad_norm(gdn_norm_w)
    n3 = jnp.tile(xa_norm_w, (1, XA_HEADS)).reshape(DEPTH, 1, XA_W)
    return norm_w.reshape(DEPTH, 1, D_MODEL), w_all, cw, w2bd, glab, gparams, wout, n1, n2, n3


def kernel(x, mem, norm_w, w_in, gla_w2, gla_b, gla_norm_w, gdn_conv_w, gdn_a_log, gdn_dt_bias,
           gdn_norm_w, mem_norm_w, xa_w_kv, xa_norm_w, w_out, final_norm_w):
    assert x.shape == (BATCH, SEQ, D_MODEL) and mem.shape == (BATCH, MEM_LEN, D_MODEL)
    mkv = _memkv(mem, mem_norm_w, xa_w_kv.astype(BF16))
    nw, w_all, cw, w2bd, glab, gparams, wout, n1, n2, n3 = _pack_params(
        norm_w, w_in, gla_w2, gla_b, gla_norm_w, gdn_conv_w, gdn_a_log, gdn_dt_bias, gdn_norm_w,
        xa_norm_w, w_out)
    h = x.reshape(BATCH * SEQ, D_MODEL)
    fnw = final_norm_w.reshape(1, D_MODEL)
    for l in range(DEPTH):
        gla, v1, dec, zg, q2, k2, v2, gcol, grow, xaq = _inproj(l, h, nw, w_all, cw, w2bd, glab, gparams)
        o1f, o1b, o2f, o2b = _mix_scan(gla, v1, dec, q2, k2, v2, gcol, grow)
        h = _outproj(l, o1f, o1b, o2f, o2b, zg, xaq, mkv, h, wout, n1, n2, n3, fnw)
    return h.reshape(BATCH, SEQ, D_MODEL)
```

```python
import functools

import jax
import jax.numpy as jnp
from jax import lax
from jax.experimental import pallas as pl
from jax.experimental.pallas import tpu as pltpu

F32 = jnp.float32
BF16 = jnp.bfloat16

D_MODEL = 1024
BATCH = 8
SEQ = 4096
DEPTH = 2
MEM_LEN = 256
CHUNK = 64
NORM_EPS = 1e-6
GLA_HEADS = 4
GLA_DK = 64
GLA_DV = 96
GLA_RANK = 16
GLA_GATE_NORMALIZER = 16.0
GDN_HEADS = 4
GDN_DK = 96
GDN_DV = 96
GDN_CONV = 5
XA_HEADS = 4
XA_DH = 64

LANES = 128
HEADS = 4
HW = HEADS * LANES
GLA_QK_W = GLA_HEADS * GLA_DK
XA_W = XA_HEADS * XA_DH
MIX_PAD_W = 2 * HW + XA_W

C_GQ = 0
C_GK = C_GQ + GLA_QK_W
C_GV = C_GK + GLA_QK_W
C_DQKV = C_GV + HW
C_Z = C_DQKV + 3 * HW
C_XQ = C_Z + MIX_PAD_W
IN_PAD_W = C_XQ + XA_W
MISC_PER_TILE = LANES - GLA_DV

M_A = 8
M_A_COPIES = 4
M_LR = 64
G_BETA, G_GC, G_EGC, G_EKEND, G_DEC = 0, 8, 16, 24, 32
ROW_SLAB = 16

TM_IN = 512
TM_OUT = 512
NC = 8
TB = NC * CHUNK
HALO = 8
CONV_PHASES = 2
VMEM_LIMIT = 56 * 1024 * 1024


def _dot(a, b):
    return jnp.dot(a, b, preferred_element_type=F32)


def _dot_nt(a, b):
    return lax.dot_general(a, b, (((1,), (1,)), ((), ())), preferred_element_type=F32)


def _dot_tn(a, b):
    return lax.dot_general(a, b, (((0,), (0,)), ((), ())), preferred_element_type=F32)


def _sigmoid(x):
    return 0.5 * jnp.tanh(0.5 * x) + 0.5


def _silu(x):
    return x * _sigmoid(x)


def _softplus(x):
    return jnp.maximum(x, 0.0) + jnp.log(1.0 + jnp.exp(-jnp.abs(x)))


def _log_sigmoid(x):
    return jnp.minimum(x, 0.0) - jnp.log(1.0 + jnp.exp(-jnp.abs(x)))


def _split2(x):
    hi = pltpu.bitcast(pltpu.bitcast(x, jnp.int32) & jnp.int32(-65536), F32)
    return hi.astype(BF16), (x - hi).astype(BF16)


def _tri_sum(tri, x):
    hi, lo = _split2(x)
    return _dot(tri, hi) + _dot(tri, lo)


def _tri_masks():
    ri = lax.broadcasted_iota(jnp.int32, (CHUNK, CHUNK), 0)
    ci = lax.broadcasted_iota(jnp.int32, (CHUNK, CHUNK), 1)
    return ri, ci


def _memkv_kernel(mem_ref, nw_ref, w_ref, out_ref):
    m = mem_ref[0]
    ms = jnp.mean(m * m, axis=-1, keepdims=True)
    mn = (m * lax.rsqrt(ms + NORM_EPS) * nw_ref[0]).astype(BF16)
    out_ref[0, 0] = _dot(mn, w_ref[0]).astype(BF16)


def _memkv(mem, mem_norm_w, xa_w_kv_bf16):
    return pl.pallas_call(
        _memkv_kernel,
        grid=(DEPTH, BATCH),
        in_specs=[
            pl.BlockSpec((1, MEM_LEN, D_MODEL), lambda l, b: (b, 0, 0)),
            pl.BlockSpec((1, 1, D_MODEL), lambda l, b: (l, 0, 0)),
            pl.BlockSpec((1, D_MODEL, 2 * XA_W), lambda l, b: (l, 0, 0)),
        ],
        out_specs=pl.BlockSpec((1, 1, MEM_LEN, 2 * XA_W), lambda l, b: (l, b, 0, 0)),
        out_shape=jax.ShapeDtypeStruct((DEPTH, BATCH, MEM_LEN, 2 * XA_W), BF16),
        compiler_params=pltpu.CompilerParams(dimension_semantics=("parallel", "parallel")),
        name="memkv",
    )(mem, mem_norm_w.reshape(DEPTH, 1, D_MODEL), xa_w_kv_bf16)


def _inproj_kernel(x_ref, xp_ref, xn_ref, nw_ref, w_ref, cw_ref, w2_ref, gb_ref, gp_ref,
                   gla_ref, v1_ref, dec_ref, zg_ref, q2_ref, k2_ref, v2_ref, gcol_ref, grow_ref,
                   xaq_ref, ext_ref, conv_ref):
    nw = nw_ref[...]

    def norm(x):
        ms = jnp.mean(x * x, axis=-1, keepdims=True)
        return (x * lax.rsqrt(ms + NORM_EPS) * nw).astype(BF16)

    h = norm(x_ref[...])
    h_halo = norm(jnp.concatenate([xp_ref[...], xn_ref[...]], axis=0))

    def proj(hh, lo, width):
        return _dot(hh, w_ref[:, lo:lo + width])

    blocks_per_seq = SEQ // TM_IN
    j = lax.rem(pl.program_id(0), blocks_per_seq)
    cw = cw_ref[...]
    base = HALO - GDN_CONV // 2

    def gdn_mm(grp):
        lo = C_DQKV + grp * HW
        halo = proj(h_halo, lo, HW)
        return proj(h, lo, HW), halo[0:HALO], halo[HALO:]

    def gdn_vpu(grp, res):
        main, prev, nxt = res
        for hd in range(HEADS):
            slab = grp * HEADS + hd
            lanes = slice(hd * LANES, (hd + 1) * LANES)
            ext_ref[slab, 0:HALO, :] = jnp.where(j == 0, 0.0, prev[:, lanes])
            ext_ref[slab, HALO:HALO + TM_IN, :] = main[:, lanes]
            ext_ref[slab, HALO + TM_IN:2 * HALO + TM_IN, :] = jnp.where(j == blocks_per_seq - 1, 0.0,
                                                                     nxt[:, lanes])
            taps = cw[:, grp * HW + hd * LANES:grp * HW + (hd + 1) * LANES]
            for p in range(CONV_PHASES):
                acc = None
                for t in range(GDN_CONV):
                    win = ext_ref[slab, pl.ds(base + p + t, TM_IN // CONV_PHASES, stride=CONV_PHASES), :]
                    acc = win * taps[t:t + 1, :] if acc is None else acc + win * taps[t:t + 1, :]
                conv_ref[slab, pl.ds(p, TM_IN // CONV_PHASES, stride=CONV_PHASES), :] = acc
            y = _silu(conv_ref[slab])
            if grp == 2:
                v2_ref[:, lanes] = y.astype(BF16)
            else:
                ref, scale = ((q2_ref, GDN_DK ** -0.5), (k2_ref, 1.0))[grp]
                ss = jnp.sum(y * y, axis=-1, keepdims=True)
                ref[:, lanes] = (y * lax.rsqrt(ss + NORM_EPS) * scale).astype(BF16)

    def z_vpu(lo, width, res):
        zg_ref[:, lo:lo + width] = _silu(res).astype(BF16)

    def gla_z_and_misc_vpu(res):
        z_vpu(0, HW, res)
        misc_vpu(jnp.concatenate([res[:, hd * LANES + GLA_DV:(hd + 1) * LANES] for hd in range(HEADS)], axis=1))

    def xaq_vpu(res):
        xaq_ref[...] = (res * (XA_DH ** -0.5)).astype(BF16)

    lane = lax.broadcasted_iota(jnp.int32, (1, LANES), 1)
    fwd_lane = lax.rem(lane, 8) < GDN_HEADS
    ri, ci = _tri_masks()
    lower = jnp.where(ri >= ci, 1.0, 0.0).astype(BF16)
    upper = jnp.where(ri <= ci, 1.0, 0.0).astype(BF16)
    lower_upper = jnp.concatenate([lower, upper], axis=0)
    gate = {}

    def misc_vpu(m):
        logits = _dot(m.astype(BF16), w2_ref[...]) + gb_ref[...]
        gate["g"] = _log_sigmoid(logits) * (1.0 / GLA_GATE_NORMALIZER)
        is_a = (lane >= M_A) & (lane < M_A + 8 * M_A_COPIES)
        neg_a = jnp.where(is_a, -jnp.exp(gp_ref[0:1, :]), 0.0)
        gg = neg_a * _softplus(m + gp_ref[1:2, :])
        beta = _sigmoid(m)
        for c in range(TM_IN // CHUNK):
            rows = slice(c * CHUNK, (c + 1) * CHUNK)
            ggc = gg[rows]
            hi, lo = _split2(ggc)
            both = _dot(lower_upper, jnp.concatenate([hi, lo], axis=1))
            pf = both[0:CHUNK, 0:LANES] + both[0:CHUNK, LANES:]
            sf = both[CHUNK:, 0:LANES] + both[CHUNK:, LANES:]
            gc = jnp.where(fwd_lane, pf, sf)
            last = jnp.where(fwd_lane, pf[CHUNK - 1:CHUNK], sf[0:1])
            col = jnp.where(lane < G_GC, beta[rows],
                  jnp.where(lane < G_EGC, gc,
                  jnp.where(lane < G_EKEND, jnp.exp(gc),
                  jnp.where(lane < G_DEC, jnp.exp(last - gc), jnp.exp(last)))))
            gcol_ref[rows, :] = col
            grow_ref[c] = col.T[0:ROW_SLAB, :]

    def gla_v_vpu(res):
        v1_ref[...] = res.astype(BF16)

    def gla_qk_vpu(pg):
        q1 = pg[:, 0:GLA_QK_W] * (GLA_DK ** -0.5)
        k1 = pg[:, GLA_QK_W:2 * GLA_QK_W]
        g = gate["g"]
        for c in range(TM_IN // CHUNK):
            rows = slice(c * CHUNK, (c + 1) * CHUNK)
            gch = g[rows]
            q1c = q1[rows]
            k1c = k1[rows]
            bf = _tri_sum(lower, gch[:, 0:GLA_QK_W])
            br = _tri_sum(upper, gch[:, GLA_QK_W:])
            for d, (b, last) in enumerate(((bf, bf[CHUNK - 1:CHUNK]), (br, br[0:1]))):
                off = d * 3 * GLA_QK_W
                gla_ref[rows, off:off + GLA_QK_W] = (q1c * jnp.exp(b)).astype(BF16)
                gla_ref[rows, off + GLA_QK_W:off + 2 * GLA_QK_W] = (k1c * jnp.exp(-b)).astype(BF16)
                gla_ref[rows, off + 2 * GLA_QK_W:off + 3 * GLA_QK_W] = (k1c * jnp.exp(last - b)).astype(BF16)
                dec_ref[c, :, d * GLA_QK_W:(d + 1) * GLA_QK_W] = jnp.exp(last)

    part = functools.partial
    tasks = [
        (part(proj, h, C_Z, HW), gla_z_and_misc_vpu),
        (part(gdn_mm, 0), part(gdn_vpu, 0)),
        (part(gdn_mm, 1), part(gdn_vpu, 1)),
        (part(proj, h, C_Z + HW, HW + XA_W), part(z_vpu, HW, HW + XA_W)),
        (part(gdn_mm, 2), part(gdn_vpu, 2)),
        (part(proj, h, C_GQ, 2 * GLA_QK_W), gla_qk_vpu),
        (part(proj, h, C_GV, HW), gla_v_vpu),
        (part(proj, h, C_XQ, XA_W), xaq_vpu),
    ]
    res = tasks[0][0]()
    for t, (_, vpu) in enumerate(tasks):
        nxt_res = tasks[t + 1][0]() if t + 1 < len(tasks) else None
        vpu(res)
        res = nxt_res


def _inproj(layer, x2d, norm_w, w_all, conv_w, w2bd, gla_b, gdn_params):
    n_tok = x2d.shape[0]
    nblk = n_tok // TM_IN
    halo_blocks = TM_IN // HALO
    n_halo = n_tok // HALO
    nchunk = TM_IN // CHUNK
    tok = lambda w: pl.BlockSpec((TM_IN, w), lambda i: (i, 0))
    const = lambda shape: pl.BlockSpec((None,) + shape, lambda i: (layer,) + tuple(0 for _ in shape))
    out_shapes = (
        jax.ShapeDtypeStruct((n_tok, 6 * GLA_QK_W), BF16),
        jax.ShapeDtypeStruct((n_tok, HW), BF16),
        jax.ShapeDtypeStruct((n_tok // CHUNK, 1, 2 * GLA_QK_W), F32),
        jax.ShapeDtypeStruct((n_tok, MIX_PAD_W), BF16),
        jax.ShapeDtypeStruct((n_tok, HW), BF16),
        jax.ShapeDtypeStruct((n_tok, HW), BF16),
        jax.ShapeDtypeStruct((n_tok, HW), BF16),
        jax.ShapeDtypeStruct((n_tok, LANES), F32),
        jax.ShapeDtypeStruct((n_tok // CHUNK, ROW_SLAB, CHUNK), F32),
        jax.ShapeDtypeStruct((n_tok, XA_W), BF16),
    )
    out_specs = (
        tok(6 * GLA_QK_W), tok(HW),
        pl.BlockSpec((nchunk, 1, 2 * GLA_QK_W), lambda i: (i, 0, 0)),
        tok(MIX_PAD_W), tok(HW), tok(HW), tok(HW), tok(LANES),
        pl.BlockSpec((nchunk, ROW_SLAB, CHUNK), lambda i: (i, 0, 0)),
        tok(XA_W),
    )
    return pl.pallas_call(
        _inproj_kernel,
        grid=(nblk,),
        in_specs=[
            tok(D_MODEL),
            pl.BlockSpec((HALO, D_MODEL), lambda i: (jnp.maximum(i * halo_blocks - 1, 0), 0)),
            pl.BlockSpec((HALO, D_MODEL), lambda i: (jnp.minimum((i + 1) * halo_blocks, n_halo - 1), 0)),
            const((1, D_MODEL)),
            const((D_MODEL, IN_PAD_W)),
            const((8, 3 * HW)),
            const((LANES, 2 * GLA_QK_W)),
            const((1, 2 * GLA_QK_W)),
            const((8, LANES)),
        ],
        out_specs=out_specs,
        out_shape=out_shapes,
        scratch_shapes=[pltpu.VMEM((3 * HEADS, TM_IN + 2 * HALO, LANES), F32),
                        pltpu.VMEM((3 * HEADS, TM_IN, LANES), F32)],
        compiler_params=pltpu.CompilerParams(dimension_semantics=("parallel",),
                                             vmem_limit_bytes=VMEM_LIMIT),
        name="inproj",
    )(x2d, x2d, x2d, norm_w, w_all, conv_w, w2bd, gla_b, gdn_params)


def _gla_stages(gf_ref, gb_ref, vf_ref, vb_ref, decf_ref, decb_ref, of_ref, ob_ref, s_ref):
    ri = lax.broadcasted_iota(jnp.int32, (CHUNK, LANES), 0)
    lane_tok = lax.broadcasted_iota(jnp.int32, (CHUNK, LANES), 1)
    ci = lane_tok & (CHUNK - 1)
    incl = (ri >= ci, ri <= ci)
    left = lane_tok < GLA_DK
    lane_sq = lax.broadcasted_iota(jnp.int32, (LANES, LANES), 1) < GLA_DK
    zero_bf = jnp.zeros((CHUNK, LANES), BF16)
    cat = jnp.concatenate

    def split_heads(x):
        return cat([jnp.where(left, x, zero_bf), jnp.where(left, zero_bf, x)], axis=0)

    g_refs, v_refs = (gf_ref, gb_ref), (vf_ref, vb_ref)
    dec_refs, o_refs = (decf_ref, decb_ref), (of_ref, ob_ref)
    chunk_of = lambda d, step: step if d == 0 else NC - 1 - step
    units_of = lambda step: [(d, step, p) for d in range(2) for p in range(GLA_HEADS // 2)]
    qe, kend, v, a, intra, kv = {}, {}, {}, {}, {}, {}
    st = {(d, p): s_ref[d, p] for d in range(2) for p in range(GLA_HEADS // 2)}

    def scores(step):
        for u in units_of(step):
            d, _, p = u
            c = chunk_of(d, step)
            rows = slice(c * CHUNK, (c + 1) * CHUNK)
            qe[u] = g_refs[d][rows, p * LANES:(p + 1) * LANES]
            ke = g_refs[d][rows, GLA_QK_W + p * LANES:GLA_QK_W + (p + 1) * LANES]
            kend[u] = g_refs[d][rows, 2 * GLA_QK_W + p * LANES:2 * GLA_QK_W + (p + 1) * LANES]
            v[u] = (v_refs[d][rows, 2 * p * LANES:(2 * p + 1) * LANES],
                    v_refs[d][rows, (2 * p + 1) * LANES:(2 * p + 2) * LANES])
            a[u] = jnp.where(incl[d], _dot_nt(qe[u], split_heads(ke)), 0.0).astype(BF16)

    def values(step):
        for u in units_of(step):
            intra[u] = _dot(a[u], cat([cat([v[u][0], zero_bf], axis=1), cat([zero_bf, v[u][1]], axis=1)],
                                      axis=0))
            kv[u] = _dot_tn(cat([v[u][0], v[u][1]], axis=1), kend[u])

    def outputs(step):
        for u in units_of(step):
            d, _, p = u
            c = chunk_of(d, step)
            inter = _dot_nt(split_heads(qe[u]), st[(d, p)].astype(BF16))
            for hh in range(2):
                hd = 2 * p + hh
                o = intra[u][:, hh * LANES:(hh + 1) * LANES] + inter[hh * CHUNK:(hh + 1) * CHUNK]
                o_refs[d][c * CHUNK:(c + 1) * CHUNK, hd * LANES:(hd + 1) * LANES] = o.astype(o_refs[d].dtype)
            dec = dec_refs[d][c, :, p * LANES:(p + 1) * LANES]
            st[(d, p)] = dec * st[(d, p)] + jnp.where(lane_sq, kv[u][0:LANES], kv[u][LANES:])

    def finish():
        for (d, p), s in st.items():
            s_ref[d, p] = s

    return scores, values, outputs, finish


def _gdn_stages(qf_ref, kf_ref, vf_ref, colf_ref, rowf_ref, qb_ref, kb_ref, vb_ref, colb_ref, rowb_ref,
                of_ref, ob_ref, s_ref):
    pk = GDN_HEADS * CHUNK
    ri = lax.broadcasted_iota(jnp.int32, (CHUNK, pk), 0)
    ci = lax.broadcasted_iota(jnp.int32, (CHUNK, pk), 1) & (CHUNK - 1)
    incl = (ri >= ci, ri <= ci)
    strict = (ri > ci, ri < ci)
    eye = jnp.where(ri == ci, 1.0, 0.0)
    blk = lambda axis: lax.shift_right_logical(lax.broadcasted_iota(jnp.int32, (pk, pk), axis), 6)
    same_blk = blk(0) == blk(1)
    bd_mask = jnp.where(same_blk, 1.0, 0.0).astype(BF16)
    left = lax.broadcasted_iota(jnp.int32, (CHUNK, LANES), 1) < CHUNK
    zero_tok = jnp.zeros((CHUNK, LANES), BF16)
    zero_sq = jnp.zeros((LANES, LANES), BF16)

    refs = ((qf_ref, kf_ref, vf_ref, colf_ref, rowf_ref, of_ref),
            (qb_ref, kb_ref, vb_ref, colb_ref, rowb_ref, ob_ref))
    groups = [(d, step) for step in range(NC) for d in range(2)]
    pairs = range(GDN_HEADS // 2)
    top = lambda m: m[0:CHUNK]
    bot = lambda m: m[CHUNK:2 * CHUNK]
    cat = jnp.concatenate
    chunk_of = lambda d, step: step if d == 0 else NC - 1 - step

    def block_diag(x):
        return cat([x, x, x, x], axis=0) * bd_mask

    def diag2(a, b, zero):
        return cat([cat([a, zero], axis=1), cat([zero, b], axis=1)], axis=0)

    q, k, v, col, gc_r = [], [], [], [], []
    for d, step in groups:
        q_ref, k_ref, v_ref, col_ref, row_ref, _ = refs[d]
        c = chunk_of(d, step)
        rows = slice(c * CHUNK, (c + 1) * CHUNK)
        heads = [slice(hd * LANES, (hd + 1) * LANES) for hd in range(GDN_HEADS)]
        q.append([q_ref[rows, h] for h in heads])
        k.append([k_ref[rows, h] for h in heads])
        v.append([v_ref[rows, h] for h in heads])
        col.append(col_ref[rows, :])
        rowt = row_ref[c]
        gc_r.append(cat([rowt[8 + d * GDN_HEADS + hd:9 + d * GDN_HEADS + hd, :] for hd in range(GDN_HEADS)],
                        axis=1))

    def colv(g, base, hd):
        lane = base + groups[g][0] * GDN_HEADS + hd
        return col[g][:, lane:lane + 1]

    def col_packed(g, base):
        tiles = [jnp.where(left, jnp.broadcast_to(colv(g, base, 2 * j), (CHUNK, LANES)),
                           jnp.broadcast_to(colv(g, base, 2 * j + 1), (CHUNK, LANES))) for j in pairs]
        return cat(tiles, axis=1)

    qk, uw = {}, {}

    def rhs_of(g, hd):
        beta = colv(g, G_BETA, hd)
        return cat([(v[g][hd].astype(F32) * beta).astype(BF16),
                    (k[g][hd].astype(F32) * (beta * colv(g, G_EGC, hd))).astype(BF16)], axis=1)

    def pre(gl):
        kq = {g: [_dot_nt(cat([cat([k[g][2 * j], k[g][2 * j + 1]], axis=1),
                               cat([q[g][2 * j], q[g][2 * j + 1]], axis=1)], axis=0),
                          diag2(k[g][2 * j], k[g][2 * j + 1], zero_tok)) for j in pairs] for g in gl}
        yield
        pm, cur = {}, {}
        for g in gl:
            d = groups[g][0]
            kk = cat([top(kq[g][j]) for j in pairs], axis=1)
            qkr = cat([bot(kq[g][j]) for j in pairs], axis=1)
            decay = jnp.where(incl[d], jnp.exp(jnp.where(incl[d], col_packed(g, G_GC) - gc_r[g], 0.0)), 0.0)
            n = jnp.where(strict[d], -(kk * decay * col_packed(g, G_BETA)), 0.0)
            qk[g] = jnp.where(incl[d], qkr * decay, 0.0).astype(BF16)
            pm[g] = eye + n
            cur[g] = _dot(n.astype(BF16), block_diag(n.astype(BF16)))
        yield
        for _ in range(4):
            st = {g: _dot(cat([pm[g].astype(BF16), cur[g].astype(BF16)], axis=0),
                          block_diag(cur[g].astype(BF16))) for g in gl}
            for g in gl:
                pm[g] = pm[g] + top(st[g])
                cur[g] = bot(st[g])
            yield
        tinv = {g: pm[g] + _dot(pm[g].astype(BF16), block_diag(cur[g].astype(BF16))) for g in gl}
        yield
        for g in gl:
            uw[g] = []
            for j in pairs:
                t = tinv[g][:, j * LANES:(j + 1) * LANES]
                lhs = cat([jnp.where(left, t, 0.0), jnp.where(left, 0.0, t)], axis=0).astype(BF16)
                uw[g].append(_dot(lhs, cat([rhs_of(g, 2 * j), rhs_of(g, 2 * j + 1)], axis=0)))
        yield

    def scan(step):
        gs = [g for g, (_, st_) in enumerate(groups) if st_ == step]
        units = [(g, j) for g in gs for j in pairs]
        s = {(g, hd): s_ref[groups[g][0], hd] for g in gs for hd in range(GDN_HEADS)}
        r = {}
        for g, j in units:
            a, b = 2 * j, 2 * j + 1
            x = uw[g][j]
            lhs = cat([cat([x[0:CHUNK, LANES:].astype(BF16), x[CHUNK:, LANES:].astype(BF16)], axis=1),
                       cat([q[g][a], q[g][b]], axis=1)], axis=0)
            r[(g, j)] = _dot(lhs, diag2(s[(g, a)].astype(BF16), s[(g, b)].astype(BF16), zero_sq))
        yield
        vn = {}
        for g, j in units:
            x, rr = uw[g][j], r[(g, j)]
            vn[(g, 2 * j)] = x[0:CHUNK, 0:LANES] - rr[0:CHUNK, 0:LANES]
            vn[(g, 2 * j + 1)] = x[CHUNK:, 0:LANES] - rr[0:CHUNK, LANES:]
        for g, j in units:
            d = groups[g][0]
            a, b = 2 * j, 2 * j + 1
            intra = _dot(qk[g][:, j * LANES:(j + 1) * LANES],
                         diag2(vn[(g, a)].astype(BF16), vn[(g, b)].astype(BF16), zero_tok))
            c = chunk_of(d, step)
            for hd, lanes in ((a, slice(0, LANES)), (b, slice(LANES, 2 * LANES))):
                o = colv(g, G_EGC, hd) * r[(g, j)][CHUNK:, lanes] + intra[:, lanes]
                o_ref = refs[d][5]
                o_ref[c * CHUNK:(c + 1) * CHUNK, hd * LANES:(hd + 1) * LANES] = o.astype(o_ref.dtype)
        for g in gs:
            d = groups[g][0]
            for hd in range(GDN_HEADS):
                dec = col[g][0:1, G_DEC + d * GDN_HEADS + hd:G_DEC + d * GDN_HEADS + hd + 1]
                kv = _dot_tn(k[g][hd], (colv(g, G_EKEND, hd) * vn[(g, hd)]).astype(BF16))
                s_ref[d, hd] = dec * s[(g, hd)] + kv
        yield

    groups_of = lambda lo, hi: [g for g, (_, st_) in enumerate(groups) if lo <= st_ < hi]
    return pre, scan, groups_of


def _scan_kernel(gf_ref, gb_ref, v1f_ref, v1b_ref, decf_ref, decb_ref,
                 qf_ref, kf_ref, vf_ref, colf_ref, rowf_ref, qb_ref, kb_ref, vb_ref, colb_ref, rowb_ref,
                 o1f_ref, o1b_ref, o2f_ref, o2b_ref, s1_ref, s2_ref):
    @pl.when(pl.program_id(1) == 0)
    def _():
        s1_ref[...] = jnp.zeros_like(s1_ref)
        s2_ref[...] = jnp.zeros_like(s2_ref)

    gla_scores, gla_values, gla_outputs, gla_finish = _gla_stages(
        gf_ref, gb_ref, v1f_ref, v1b_ref, decf_ref, decb_ref, o1f_ref, o1b_ref, s1_ref)
    pre, scan, groups_of = _gdn_stages(qf_ref, kf_ref, vf_ref, colf_ref, rowf_ref,
                                       qb_ref, kb_ref, vb_ref, colb_ref, rowb_ref, o2f_ref, o2b_ref, s2_ref)

    half = NC // 2
    for _ in pre(groups_of(0, half)):
        pass
    later = pre(groups_of(half, NC))
    for step in range(half):
        for _ in scan(step):
            next(later, None)
    for _ in later:
        pass
    gla_scores(0)
    gla_step = 0
    for step in range(half, NC):
        for _ in scan(step):
            if gla_step + 1 < NC:
                gla_scores(gla_step + 1)
            gla_values(gla_step)
            gla_outputs(gla_step)
            gla_step += 1
    assert gla_step == NC
    gla_finish()


def _mix_scan(gla, v1, dec, q2, k2, v2, gcol, grow):
    nb = SEQ // TB
    n_tok = q2.shape[0]
    fwd = lambda b, i: (b * nb + i, 0)
    bwd = lambda b, i: (b * nb + nb - 1 - i, 0)
    fwd3 = lambda b, i: (b * nb + i, 0, 0)
    bwd3 = lambda b, i: (b * nb + nb - 1 - i, 0, 0)

    def gdn_specs(m2, m3):
        return [pl.BlockSpec((TB, HW), m2), pl.BlockSpec((TB, HW), m2), pl.BlockSpec((TB, HW), m2),
                pl.BlockSpec((TB, LANES), m2), pl.BlockSpec((NC, ROW_SLAB, CHUNK), m3)]

    gla_specs = [
        pl.BlockSpec((TB, 3 * GLA_QK_W), fwd),
        pl.BlockSpec((TB, 3 * GLA_QK_W), lambda b, i: (b * nb + nb - 1 - i, 1)),
        pl.BlockSpec((TB, HW), fwd),
        pl.BlockSpec((TB, HW), bwd),
        pl.BlockSpec((NC, 1, GLA_QK_W), lambda b, i: (b * nb + i, 0, 0)),
        pl.BlockSpec((NC, 1, GLA_QK_W), lambda b, i: (b * nb + nb - 1 - i, 0, 1)),
    ]
    out = jax.ShapeDtypeStruct((n_tok, HW), BF16)
    return pl.pallas_call(
        _scan_kernel,
        grid=(BATCH, nb),
        in_specs=gla_specs + gdn_specs(fwd, fwd3) + gdn_specs(bwd, bwd3),
        out_specs=(pl.BlockSpec((TB, HW), fwd), pl.BlockSpec((TB, HW), bwd),
                   pl.BlockSpec((TB, HW), fwd), pl.BlockSpec((TB, HW), bwd)),
        out_shape=(out, out, out, out),
        scratch_shapes=[pltpu.VMEM((2, GLA_HEADS // 2, LANES, LANES), F32),
                        pltpu.VMEM((2, GDN_HEADS, LANES, LANES), F32)],
        compiler_params=pltpu.CompilerParams(dimension_semantics=("parallel", "arbitrary"),
                                             vmem_limit_bytes=VMEM_LIMIT),
        name="mix_scan",
    )(gla, gla, v1, v1, dec, dec, q2, k2, v2, gcol, grow, q2, k2, v2, gcol, grow)


def _outproj_kernel(o1f_ref, o1b_ref, o2f_ref, o2b_ref, zg_ref, xaq_ref, mkv_ref, x_ref, wout_ref,
                    n1_ref, n2_ref, n3_ref, fn_ref, out_ref, *, final):
    def head_norm(o, nw_ref, width):
        parts = []
        for hd in range(HEADS):
            s = o[:, hd * LANES:(hd + 1) * LANES]
            ms = jnp.sum(s * s, axis=-1, keepdims=True) * (1.0 / width)
            parts.append(s * lax.rsqrt(ms + NORM_EPS))
        return jnp.concatenate(parts, axis=-1) * nw_ref[...]

    both = lambda f_ref, b_ref: f_ref[...].astype(F32) + b_ref[...].astype(F32)
    gate = lambda lo, width: zg_ref[:, lo:lo + width].astype(F32)
    lane_q = lax.broadcasted_iota(jnp.int32, (TM_OUT, LANES), 1)
    lane_m = lax.broadcasted_iota(jnp.int32, (MEM_LEN, LANES), 1)
    first_q = lane_q < XA_DH
    first_m = lane_m < XA_DH
    q_head = (first_q, lane_q >= XA_DH)
    m_head = (first_m, lane_m >= XA_DH)
    heads = [(p, hh) for p in range(XA_HEADS // 2) for hh in range(2)]

    sc = []
    for p, hh in heads:
        qpair = xaq_ref[:, p * LANES:(p + 1) * LANES]
        qm = jnp.where(q_head[hh], qpair, jnp.zeros_like(qpair))
        sc.append(_dot_nt(qm, mkv_ref[0, :, p * LANES:(p + 1) * LANES]))

    o1 = head_norm(both(o1f_ref, o1b_ref), n1_ref, GLA_DV) * gate(0, HW)
    y = _dot(o1.astype(BF16), wout_ref[0:HW, :])

    pv = []
    for (p, hh), s in zip(heads, sc):
        e = jnp.exp(s - jnp.max(s, axis=-1, keepdims=True))
        l = jnp.sum(e, axis=-1, keepdims=True)
        mv = mkv_ref[0, :, XA_W + p * LANES:XA_W + (p + 1) * LANES]
        mvm = jnp.where(m_head[hh], mv, jnp.zeros_like(mv))
        pv.append(_dot(e.astype(BF16), mvm) * (1.0 / l))

    o2 = head_norm(both(o2f_ref, o2b_ref), n2_ref, GDN_DV) * gate(HW, HW)
    y = y + _dot(o2.astype(BF16), wout_ref[HW:2 * HW, :])

    for p in range(XA_HEADS // 2):
        lanes = slice(p * LANES, (p + 1) * LANES)
        acc = pv[2 * p] + pv[2 * p + 1]
        sq = acc * acc
        ss0 = jnp.sum(jnp.where(first_q, sq, 0.0), axis=-1, keepdims=True)
        ss1 = jnp.sum(jnp.where(first_q, 0.0, sq), axis=-1, keepdims=True)
        ms = jnp.where(first_q, ss0, ss1) * (1.0 / XA_DH)
        o3 = acc * lax.rsqrt(ms + NORM_EPS) * n3_ref[:, lanes] * gate(2 * HW + p * LANES, LANES)
        y = y + _dot(o3.astype(BF16), wout_ref[2 * HW + p * LANES:2 * HW + (p + 1) * LANES, :])

    xo = x_ref[...] + y
    if final:
        ms = jnp.mean(xo * xo, axis=-1, keepdims=True)
        xo = xo * lax.rsqrt(ms + NORM_EPS) * fn_ref[...]
    out_ref[...] = xo


def _outproj(layer, o1f, o1b, o2f, o2b, zg, xaq, mkv, x2d, wout, n1, n2, n3, fnw):
    n_tok = x2d.shape[0]
    blocks_per_seq = SEQ // TM_OUT
    tok = lambda w: pl.BlockSpec((TM_OUT, w), lambda i: (i, 0))
    const = lambda shape: pl.BlockSpec((None,) + shape, lambda i: (layer,) + tuple(0 for _ in shape))
    return pl.pallas_call(
        functools.partial(_outproj_kernel, final=layer == DEPTH - 1),
        grid=(n_tok // TM_OUT,),
        in_specs=[
            tok(HW), tok(HW), tok(HW), tok(HW), tok(MIX_PAD_W), tok(XA_W),
            pl.BlockSpec((None, 1, MEM_LEN, 2 * XA_W), lambda i: (layer, i // blocks_per_seq, 0, 0)),
            tok(D_MODEL),
            const((MIX_PAD_W, D_MODEL)),
            const((1, HW)), const((1, HW)), const((1, XA_W)),
            pl.BlockSpec((1, D_MODEL), lambda i: (0, 0)),
        ],
        out_specs=tok(D_MODEL),
        out_shape=jax.ShapeDtypeStruct((n_tok, D_MODEL), F32),
        compiler_params=pltpu.CompilerParams(dimension_semantics=("parallel",),
                                             vmem_limit_bytes=VMEM_LIMIT),
        name="outproj",
    )(o1f, o1b, o2f, o2b, zg, xaq, mkv, x2d, wout, n1, n2, n3, fnw)


def _pad_heads(w, axis):
    shp = w.shape
    w = w.reshape(shp[:axis] + (HEADS, GLA_DV) + shp[axis + 1:])
    pad = [(0, 0)] * w.ndim
    pad[axis + 1] = (0, LANES - GLA_DV)
    w = jnp.pad(w, pad)
    return w.reshape(shp[:axis] + (HW,) + shp[axis + 1:])


_SRC_SIZES = (GLA_QK_W, GLA_QK_W, HEADS * GLA_DV, HEADS * GLA_DV, 2 * GLA_RANK, 3 * HEADS * GDN_DK,
              HEADS * GDN_DV, 2 * GDN_HEADS, 2 * GDN_HEADS, XA_W, XA_W)
(S_GQ, S_GK, S_GV, S_GZ, S_LR, S_DQKV, S_DZ, S_DB, S_DA, S_XQ, S_XZ) = (
    sum(_SRC_SIZES[:n]) for n in range(len(_SRC_SIZES)))
IN_W = sum(_SRC_SIZES)
REPACK_ROWS = 256


def _repack_kernel(wt_ref, out_ref, w_ref):
    for lo in range(0, IN_W, LANES):
        width = min(LANES, IN_W - lo)
        w_ref[:, lo:lo + width] = wt_ref[lo:lo + width, :].T

    def src(lo, width):
        return w_ref[:, lo:lo + width].astype(BF16)

    rows = w_ref.shape[0]
    zeros = lambda width: jnp.zeros((rows, width), BF16)

    def put(dst, piece):
        out_ref[:, dst:dst + piece.shape[1]] = piece

    def put_heads(dst, lo):
        for hd in range(HEADS):
            put(dst + hd * LANES, jnp.concatenate([src(lo + hd * GLA_DV, GLA_DV), zeros(LANES - GLA_DV)], axis=1))

    put(C_GQ, src(S_GQ, GLA_QK_W))
    put(C_GK, src(S_GK, GLA_QK_W))
    put_heads(C_GV, S_GV)
    for part in range(3):
        put_heads(C_DQKV + part * HW, S_DQKV + part * HEADS * GDN_DK)
    put_heads(C_Z + HW, S_DZ)
    put(C_Z + 2 * HW, src(S_XZ, XA_W))
    put(C_XQ, src(S_XQ, XA_W))
    da = src(S_DA, 2 * GDN_HEADS)
    misc = jnp.concatenate(
        [src(S_DB, 2 * GDN_HEADS)] + [da] * M_A_COPIES
        + [zeros(M_LR - M_A - 8 * M_A_COPIES), src(S_LR, 2 * GLA_RANK), zeros(LANES - M_LR - 2 * GLA_RANK)],
        axis=1)
    for hd in range(HEADS):
        put(C_Z + hd * LANES, jnp.concatenate(
            [src(S_GZ + hd * GLA_DV, GLA_DV), misc[:, hd * MISC_PER_TILE:(hd + 1) * MISC_PER_TILE]], axis=1))


def _repack_w_in(w_in):
    return pl.pallas_call(
        _repack_kernel,
        grid=(DEPTH, D_MODEL // REPACK_ROWS),
        in_specs=[pl.BlockSpec((None, IN_W, REPACK_ROWS), lambda l, r: (l, 0, r))],
        out_specs=pl.BlockSpec((None, REPACK_ROWS, IN_PAD_W), lambda l, r: (l, r, 0)),
        out_shape=jax.ShapeDtypeStruct((DEPTH, D_MODEL, IN_PAD_W), BF16),
        scratch_shapes=[pltpu.VMEM((REPACK_ROWS, IN_W), F32)],
        compiler_params=pltpu.CompilerParams(dimension_semantics=("parallel", "parallel")),
        name="repack_w_in",
    )(jnp.swapaxes(w_in, 1, 2))


def _pack_params(norm_w, w_in, gla_w2, gla_b, gla_norm_w, gdn_conv_w, gdn_a_log, gdn_dt_bias, gdn_norm_w,
                 xa_norm_w, w_out):
    hd = HEADS * GDN_DK
    zeros = lambda *shape: jnp.zeros((DEPTH,) + shape, F32)
    w_all = _repack_w_in(w_in)

    cw = jnp.transpose(gdn_conv_w, (0, 2, 1))
    cw = jnp.concatenate([_pad_heads(cw[:, :, 0:hd], 2), _pad_heads(cw[:, :, hd:2 * hd], 2),
                          _pad_heads(cw[:, :, 2 * hd:], 2)], axis=2)
    cw = jnp.pad(cw, ((0, 0), (0, 8 - GDN_CONV), (0, 0)))

    w2bd = jnp.concatenate(
        [zeros(M_LR, 2 * GLA_QK_W),
         jnp.concatenate([gla_w2[:, 0], zeros(GLA_RANK, GLA_QK_W)], axis=2),
         jnp.concatenate([zeros(GLA_RANK, GLA_QK_W), gla_w2[:, 1]], axis=2),
         zeros(LANES - M_LR - 2 * GLA_RANK, 2 * GLA_QK_W)], axis=1).astype(BF16)
    glab = gla_b.reshape(DEPTH, 1, 2 * GLA_QK_W)

    def a_slab(p):
        flat = p.reshape(DEPTH, 1, 2 * GDN_HEADS)
        return jnp.concatenate([zeros(1, M_A)] + [flat] * M_A_COPIES + [zeros(1, LANES - M_A - 8 * M_A_COPIES)],
                               axis=2)

    gparams = jnp.concatenate([a_slab(gdn_a_log), a_slab(gdn_dt_bias), zeros(6, LANES)], axis=1)

    wout = jnp.concatenate([_pad_heads(w_out[:, 0:HEADS * GLA_DV], 1),
                            _pad_heads(w_out[:, HEADS * GLA_DV:HEADS * (GLA_DV + GDN_DV)], 1),
                            w_out[:, HEADS * (GLA_DV + GDN_DV):]], axis=1).astype(BF16)
    pad_norm = lambda w: jnp.tile(jnp.pad(w, ((0, 0), (0, LANES - w.shape[1]))), (1, HEADS)).reshape(DEPTH, 1, HW)
    n1 = pad_norm(gla_norm_w)
    n2 = pad_norm(gdn_norm_w)
    n3 = jnp.tile(xa_norm_w, (1, XA_HEADS)).reshape(DEPTH, 1, XA_W)
    return norm_w.reshape(DEPTH, 1, D_MODEL), w_all, cw, w2bd, glab, gparams, wout, n1, n2, n3


def kernel(x, mem, norm_w, w_in, gla_w2, gla_b, gla_norm_w, gdn_conv_w, gdn_a_log, gdn_dt_bias,
           gdn_norm_w, mem_norm_w, xa_w_kv, xa_norm_w, w_out, final_norm_w):
    assert x.shape == (BATCH, SEQ, D_MODEL) and mem.shape == (BATCH, MEM_LEN, D_MODEL)
    mkv = _memkv(mem, mem_norm_w, xa_w_kv.astype(BF16))
    nw, w_all, cw, w2bd, glab, gparams, wout, n1, n2, n3 = _pack_params(
        norm_w, w_in, gla_w2, gla_b, gla_norm_w, gdn_conv_w, gdn_a_log, gdn_dt_bias, gdn_norm_w,
        xa_norm_w, w_out)
    h = x.reshape(BATCH * SEQ, D_MODEL)
    fnw = final_norm_w.reshape(1, D_MODEL)
    for l in range(DEPTH):
        gla, v1, dec, zg, q2, k2, v2, gcol, grow, xaq = _inproj(l, h, nw, w_all, cw, w2bd, glab, gparams)
        o1f, o1b, o2f, o2b = _mix_scan(gla, v1, dec, q2, k2, v2, gcol, grow)
        h = _outproj(l, o1f, o1b, o2f, o2b, zg, xaq, mkv, h, wout, n1, n2, n3, fnw)
    return h.reshape(BATCH, SEQ, D_MODEL)
```
